```python
import math
import jax, jax.numpy as jnp
from jax import lax
import numpy as np

D_MODEL = 1024
BATCH = 8
SEQ = 4096
DEPTH = 1
DEC_BATCH = 16
DEC_SEQ = 4096
PAST_LEN = 128

N_HEADS = 16
N_KV_HEADS = 4
HEAD_DIM = 64
Q_PER_KV = N_HEADS // N_KV_HEADS
WINDOW = 128
BLOCK = 128
N_BUCKETS = 32
MAX_DISTANCE = 128
ATTN_Q = N_HEADS * HEAD_DIM
ATTN_KV = N_KV_HEADS * HEAD_DIM
NEG_INF = -1e30
SSM_WIDTH = D_MODEL
SSM_GROUP = 16
SSM_GROUPS = SSM_WIDTH // SSM_GROUP
SSM_STATE = 64
N_DIR = 2
DT_MIN = 0.001
DT_MAX = 0.1
D_FF = 4 * D_MODEL
EPS = 1e-6
IN_SPLITS = (ATTN_Q, ATTN_Q + ATTN_KV, ATTN_Q + 2 * ATTN_KV,
             ATTN_Q + 2 * ATTN_KV + SSM_WIDTH,
             ATTN_Q + 2 * ATTN_KV + SSM_WIDTH + D_MODEL)
IN_COLS = ATTN_Q + 2 * ATTN_KV + SSM_WIDTH + 2 * D_MODEL

kernel_name = "hybrid_s5_window_gqa_encoder"


def rms_norm(x, g):
    xf = x.astype(jnp.float32)
    y = xf * lax.rsqrt(jnp.mean(xf * xf, axis=-1, keepdims=True) + EPS)
    return (y * g.astype(jnp.float32)).astype(x.dtype)


def t5_bucket(rel):
    nb = N_BUCKETS // 2
    ret = (rel > 0).astype(np.int32) * nb
    n = np.abs(rel)
    max_exact = nb // 2
    n_safe = np.maximum(n, 1).astype(np.float32)
    large = max_exact + (np.log(n_safe / max_exact) / math.log(MAX_DISTANCE / max_exact)
                         * (nb - max_exact)).astype(np.int32)
    large = np.minimum(large, nb - 1)
    return (ret + np.where(n < max_exact, n, large)).astype(np.int32)


def band_bias(rel_table):
    qi = np.arange(BLOCK)[:, None]
    kj = np.arange(3 * BLOCK)[None, :]
    buckets = t5_bucket(kj - BLOCK - qi)
    bias = rel_table.astype(jnp.float32)[buckets]
    bias = jnp.transpose(bias, (2, 0, 1))
    return bias.reshape(N_KV_HEADS, Q_PER_KV, BLOCK, 3 * BLOCK)


def band_mask(seq_len):
    n_blocks = seq_len // BLOCK
    nb = jnp.arange(n_blocks)[:, None, None]
    qi = jnp.arange(BLOCK)[None, :, None]
    kj = jnp.arange(3 * BLOCK)[None, None, :]
    rel = kj - BLOCK - qi
    key_pos = nb * BLOCK - BLOCK + kj
    return (jnp.abs(rel) <= WINDOW) & (key_pos >= 0) & (key_pos < seq_len)


def windowed_attention(q, k, v, sink, bias, mask):
    seq_len = q.shape[1]
    n_blocks = seq_len // BLOCK
    scale = HEAD_DIM ** -0.5
    sink_b = sink.astype(jnp.float32).reshape(1, N_KV_HEADS, Q_PER_KV, 1, 1)

    def neighbours(t):
        tb = t.reshape(n_blocks, BLOCK, N_KV_HEADS, HEAD_DIM)
        tp = jnp.pad(tb, ((1, 1), (0, 0), (0, 0), (0, 0)))
        return jnp.concatenate([tp[:-2], tp[1:-1], tp[2:]], axis=1)

    def attend_one(args):
        qs, ks, vs = args
        qb = qs.reshape(n_blocks, BLOCK, N_KV_HEADS, Q_PER_KV, HEAD_DIM)
        kw, vw = neighbours(ks), neighbours(vs)
        s = jnp.einsum('nqkgd,njkd->nkgqj', qb, kw).astype(jnp.float32) * scale + bias
        s = jnp.where(mask[:, None, None], s, NEG_INF)
        sink_col = jnp.broadcast_to(sink_b, s.shape[:-1] + (1,))
        p = jax.nn.softmax(jnp.concatenate([s, sink_col], axis=-1), axis=-1)[..., :-1]
        o = jnp.einsum('nkgqj,njkd->nqkgd', p.astype(vs.dtype), vw)
        return o.reshape(seq_len, ATTN_Q)

    return lax.map(attend_one, (q, k, v))


def s5_discretize(lam_re, lam_im, log_dt, b_re, b_im):
    dt = jnp.exp(log_dt.astype(jnp.float32))[..., None]
    lr, li = lam_re.astype(jnp.float32), lam_im.astype(jnp.float32)
    mag = jnp.exp(lr * dt)
    ang = li * dt
    a_re, a_im = mag * jnp.cos(ang), mag * jnp.sin(ang)
    den = lr * lr + li * li
    nr, ni = a_re - 1.0, a_im
    c_re = (nr * lr + ni * li) / den
    c_im = (ni * lr - nr * li) / den
    br, bi = b_re.astype(jnp.float32), b_im.astype(jnp.float32)
    bb_re = c_re[..., None] * br - c_im[..., None] * bi
    bb_im = c_re[..., None] * bi + c_im[..., None] * br
    return a_re, a_im, bb_re, bb_im


def _complex_combine(e1, e2):
    a1r, a1i, b1r, b1i = e1
    a2r, a2i, b2r, b2i = e2
    return (a2r * a1r - a2i * a1i,
            a2r * a1i + a2i * a1r,
            a2r * b1r - a2i * b1i + b2r,
            a2r * b1i + a2i * b1r + b2i)


def s5_branch(u, lam_re, lam_im, log_dt, b_re, b_im, c_re, c_im, d_skip, w_glu, b_glu):
    a_re, a_im, bb_re, bb_im = s5_discretize(lam_re, lam_im, log_dt, b_re, b_im)
    cr, ci = c_re.astype(jnp.float32), c_im.astype(jnp.float32)
    dsk = d_skip.astype(jnp.float32)

    def one_direction(us, d, reverse):
        bu_re = jnp.einsum('lgc,gpc->lgp', us, bb_re[d])
        bu_im = jnp.einsum('lgc,gpc->lgp', us, bb_im[d])
        ar = jnp.broadcast_to(a_re[d], bu_re.shape)
        ai = jnp.broadcast_to(a_im[d], bu_re.shape)
        _, _, h_re, h_im = lax.associative_scan(_complex_combine, (ar, ai, bu_re, bu_im),
                                                reverse=reverse, axis=0)
        return (jnp.einsum('lgp,gcp->lgc', h_re, cr[d])
                - jnp.einsum('lgp,gcp->lgc', h_im, ci[d]))

    def one_seq(us):
        seq_len = us.shape[0]
        uf = us.astype(jnp.float32)
        ug = uf.reshape(seq_len, SSM_GROUPS, SSM_GROUP)
        y = one_direction(ug, 0, False) + one_direction(ug, 1, True)
        return (y.reshape(seq_len, SSM_WIDTH) + dsk * uf).astype(us.dtype)

    y = jax.nn.gelu(lax.map(one_seq, u))
    return y * jax.nn.sigmoid(y @ w_glu + b_glu)


def encoder_layer(x, bias, ln1, w_in, q_gain, k_gain, sink, lam_re, lam_im, log_dt,
                  b_re, b_im, c_re, c_im, d_skip, w_glu, b_glu, w_out, ln2, w_up, w_down):
    bsz, seq_len, _ = x.shape
    h = rms_norm(x, ln1)
    proj = h @ w_in
    q, k, v, u, g_att, g_ssm = jnp.split(proj, IN_SPLITS, axis=-1)
    q = rms_norm(q.reshape(bsz, seq_len, N_HEADS, HEAD_DIM), q_gain)
    k = rms_norm(k.reshape(bsz, seq_len, N_KV_HEADS, HEAD_DIM), k_gain)
    v = v.reshape(bsz, seq_len, N_KV_HEADS, HEAD_DIM)
    y_att = windowed_attention(q, k, v, sink, bias, band_mask(seq_len))
    y_ssm = s5_branch(u, lam_re, lam_im, log_dt, b_re, b_im, c_re, c_im,
                      d_skip, w_glu, b_glu)
    mixed = jax.nn.sigmoid(g_att) * y_att + jax.nn.sigmoid(g_ssm) * y_ssm
    x = x + mixed @ w_out
    h2 = rms_norm(x, ln2)
    return x + jnp.square(jax.nn.relu(h2 @ w_up)) @ w_down


def trunk(x, rel_table, ln1, w_in, q_gain, k_gain, sink, lam_re, lam_im, log_dt,
          b_re, b_im, c_re, c_im, d_skip, w_glu, b_glu, w_out, ln2, w_up, w_down):
    bias = band_bias(rel_table)
    for l in range(DEPTH):
        x = encoder_layer(x, bias, ln1[l], w_in[l], q_gain[l], k_gain[l], sink[l],
                          lam_re[l], lam_im[l], log_dt[l], b_re[l], b_im[l],
                          c_re[l], c_im[l], d_skip[l], w_glu[l], b_glu[l], w_out[l],
                          ln2[l], w_up[l], w_down[l])
    return x


def setup_inputs(seed: int = 0) -> dict:
    key = jax.random.key(seed)
    ks = jax.random.split(key, 24)
    nrm = jax.random.normal
    f32 = jnp.float32
    G, P, C = SSM_GROUPS, SSM_STATE, SSM_GROUP
    lam_im_base = jnp.broadcast_to(jnp.pi * jnp.arange(P, dtype=f32), (DEPTH, N_DIR, G, P))
    return {
        "x_prompt": nrm(ks[0], (BATCH, SEQ, D_MODEL), f32),
        "x_sample": nrm(ks[1], (DEC_BATCH, DEC_SEQ, D_MODEL), f32),
        "rel_table": 0.2 * nrm(ks[2], (N_BUCKETS, N_HEADS), f32),
        "ln1": 1.0 + 0.02 * nrm(ks[3], (DEPTH, D_MODEL), f32),
        "w_in": nrm(ks[4], (DEPTH, D_MODEL, IN_COLS), f32) * D_MODEL ** -0.5,
        "q_gain": 1.0 + 0.02 * nrm(ks[5], (DEPTH, HEAD_DIM), f32),
        "k_gain": 1.0 + 0.02 * nrm(ks[6], (DEPTH, HEAD_DIM), f32),
        "sink": 0.5 * nrm(ks[7], (DEPTH, N_HEADS), f32),
        "lam_re": -0.5 + 0.01 * nrm(ks[8], (DEPTH, N_DIR, G, P), f32),
        "lam_im": lam_im_base + 0.01 * nrm(ks[9], (DEPTH, N_DIR, G, P), f32),
        "log_dt": jax.random.uniform(ks[10], (DEPTH, N_DIR, G), f32,
                                     math.log(DT_MIN), math.log(DT_MAX)),
        "b_re": nrm(ks[11], (DEPTH, N_DIR, G, P, C), f32) * (2.0 * C) ** -0.5,
        "b_im": nrm(ks[12], (DEPTH, N_DIR, G, P, C), f32) * (2.0 * C) ** -0.5,
        "c_re": nrm(ks[13], (DEPTH, N_DIR, G, C, P), f32) * (2.0 * P) ** -0.5,
        "c_im": nrm(ks[14], (DEPTH, N_DIR, G, C, P), f32) * (2.0 * P) ** -0.5,
        "d_skip": nrm(ks[15], (DEPTH, SSM_WIDTH), f32),
        "w_glu": nrm(ks[16], (DEPTH, D_MODEL, D_MODEL), f32) * D_MODEL ** -0.5,
        "b_glu": 0.01 * nrm(ks[17], (DEPTH, D_MODEL), f32),
        "w_out": nrm(ks[18], (DEPTH, D_MODEL, D_MODEL), f32) * D_MODEL ** -0.5,
        "ln2": 1.0 + 0.02 * nrm(ks[19], (DEPTH, D_MODEL), f32),
        "w_up": nrm(ks[20], (DEPTH, D_MODEL, D_FF), f32) * D_MODEL ** -0.5,
        "w_down": nrm(ks[21], (DEPTH, D_FF, D_MODEL), f32) * D_FF ** -0.5,
    }


def reference(x_prompt, x_sample, rel_table, ln1, w_in, q_gain, k_gain, sink, lam_re, lam_im,
              log_dt, b_re, b_im, c_re, c_im, d_skip, w_glu, b_glu, w_out, ln2, w_up, w_down):
    y_prompt = trunk(x_prompt, rel_table, ln1, w_in, q_gain, k_gain, sink, lam_re, lam_im,
                     log_dt, b_re, b_im, c_re, c_im, d_skip, w_glu, b_glu, w_out, ln2,
                     w_up, w_down)
    y_sample = trunk(x_sample, rel_table, ln1, w_in, q_gain, k_gain, sink, lam_re, lam_im,
                     log_dt, b_re, b_im, c_re, c_im, d_skip, w_glu, b_glu, w_out, ln2,
                     w_up, w_down)
    return (y_prompt, y_sample)
```

```python
import functools
import math

import numpy as np
import jax
import jax.numpy as jnp
from jax import lax
from jax.experimental import pallas as pl
from jax.experimental.pallas import tpu as pltpu

D_MODEL = 1024
N_HEADS = 16
N_KV_HEADS = 4
HEAD_DIM = 64
Q_PER_KV = N_HEADS // N_KV_HEADS
WINDOW = 128
BLOCK = 128
N_BUCKETS = 32
MAX_DISTANCE = 128
ATTN_Q = N_HEADS * HEAD_DIM
ATTN_KV = N_KV_HEADS * HEAD_DIM
NEG_INF = -1e30
SSM_WIDTH = D_MODEL
SSM_GROUP = 16
SSM_GROUPS = SSM_WIDTH // SSM_GROUP
SSM_STATE = 64
D_FF = 4 * D_MODEL
EPS = 1e-6
IN_COLS = ATTN_Q + 2 * ATTN_KV + SSM_WIDTH + 2 * D_MODEL

LANES = 128
CHUNK = 16
SLABS = SSM_WIDTH // LANES
GROUPS_PER_SLAB = LANES // SSM_GROUP
SLAB_STATE = GROUPS_PER_SLAB * SSM_STATE
CHUNK_COLS = CHUNK * LANES
VMEM_LIMIT = 56 * 1024 * 1024

BF16 = jnp.bfloat16
F32 = jnp.float32


def _const_spec(shape):
    nd = len(shape)
    return pl.BlockSpec(shape, lambda *_: (0,) * nd, pipeline_mode=pl.Buffered(1))


def _split_dot(r, expand):
    hi = r.astype(BF16)
    lo = (r - hi.astype(F32)).astype(BF16)
    return (jnp.dot(hi, expand, preferred_element_type=F32)
            + jnp.dot(lo, expand, preferred_element_type=F32))


def _in_proj_kernel(x_ref, ln1_ref, w_ref, qg_ref, kg_ref, sq_ref, eq_ref, sk_ref, ek_ref,
                    q_ref, k_ref, v_ref, u_ref, g_ref):
    x = x_ref[...]
    ms = jnp.mean(x * x, axis=-1, keepdims=True)
    h = (x * lax.rsqrt(ms + EPS) * ln1_ref[...]).astype(BF16)

    def head_norm(t, s_ref, e_ref, gain):
        msq = jnp.dot((t * t).astype(BF16), s_ref[...], preferred_element_type=F32)
        r = lax.rsqrt(msq + EPS)
        return t * _split_dot(r, e_ref[...]) * gain

    c0 = 0
    q = jnp.dot(h, w_ref[:, c0:c0 + ATTN_Q], preferred_element_type=F32)
    q_ref[...] = (head_norm(q, sq_ref, eq_ref, qg_ref[...]) * (HEAD_DIM ** -0.5)).astype(BF16)
    c0 += ATTN_Q
    k = jnp.dot(h, w_ref[:, c0:c0 + ATTN_KV], preferred_element_type=F32)
    k_ref[...] = head_norm(k, sk_ref, ek_ref, kg_ref[...]).astype(BF16)
    c0 += ATTN_KV
    v_ref[...] = jnp.dot(h, w_ref[:, c0:c0 + ATTN_KV], preferred_element_type=F32).astype(BF16)
    c0 += ATTN_KV
    u_ref[...] = jnp.dot(h, w_ref[:, c0:c0 + SSM_WIDTH], preferred_element_type=F32)
    c0 += SSM_WIDTH
    g_ref[...] = jnp.dot(h, w_ref[:, c0:c0 + 2 * D_MODEL], preferred_element_type=F32).astype(BF16)


def _in_proj(x2, ln1, w_in, qg, kg, sq, eq, sk, ek, tm):
    n = x2.shape[0]
    row = lambda c: pl.BlockSpec((tm, c), lambda i: (i, 0))
    return pl.pallas_call(
        _in_proj_kernel,
        grid=(n // tm,),
        in_specs=[row(D_MODEL), _const_spec((1, D_MODEL)), _const_spec((D_MODEL, IN_COLS)),
                  _const_spec((1, ATTN_Q)), _const_spec((1, ATTN_KV)),
                  _const_spec((ATTN_Q, LANES)), _const_spec((LANES, ATTN_Q)),
                  _const_spec((ATTN_KV, LANES)), _const_spec((LANES, ATTN_KV))],
        out_specs=[row(ATTN_Q), row(ATTN_KV), row(ATTN_KV), row(SSM_WIDTH), row(2 * D_MODEL)],
        out_shape=[jax.ShapeDtypeStruct((n, ATTN_Q), BF16),
                   jax.ShapeDtypeStruct((n, ATTN_KV), BF16),
                   jax.ShapeDtypeStruct((n, ATTN_KV), BF16),
                   jax.ShapeDtypeStruct((n, SSM_WIDTH), F32),
                   jax.ShapeDtypeStruct((n, 2 * D_MODEL), BF16)],
        compiler_params=pltpu.CompilerParams(dimension_semantics=("parallel",),
                                             vmem_limit_bytes=VMEM_LIMIT),
        name="in_proj",
    )(x2, ln1, w_in, qg, kg, sq, eq, sk, ek)


def _attn_kernel(sink_ref, q_ref, kp_ref, kc_ref, kn_ref, vp_ref, vc_ref, vn_ref, bias_ref, o_ref):
    n = pl.program_id(1)
    last = pl.num_programs(1) - 1
    kcat = jnp.concatenate([kp_ref[0], kc_ref[0], kn_ref[0]], axis=0)
    vcat = jnp.concatenate([vp_ref[0], vc_ref[0], vn_ref[0]], axis=0)
    col = lax.broadcasted_iota(jnp.int32, (1, 3 * BLOCK), 1)
    valid = jnp.logical_and(jnp.logical_or(col >= BLOCK, n > 0),
                            jnp.logical_or(col < 2 * BLOCK, n < last))
    outs = []
    for h in range(N_HEADS):
        kv = h // Q_PER_KV
        qh = q_ref[0, :, h * HEAD_DIM:(h + 1) * HEAD_DIM]
        kh = kcat[:, kv * HEAD_DIM:(kv + 1) * HEAD_DIM]
        vh = vcat[:, kv * HEAD_DIM:(kv + 1) * HEAD_DIM]
        s = lax.dot_general(qh, kh, (((1,), (1,)), ((), ())), preferred_element_type=F32)
        s = jnp.where(valid, s + bias_ref[h], NEG_INF)
        sink = sink_ref[h]
        m = jnp.maximum(jnp.max(s, axis=-1, keepdims=True), sink)
        p = jnp.exp(s - m)
        den = jnp.sum(p, axis=-1, keepdims=True) + jnp.exp(sink - m)
        o = jnp.dot(p.astype(BF16), vh, preferred_element_type=F32)
        outs.append(o / den)
    o_ref[0] = jnp.concatenate(outs, axis=-1).astype(BF16)


def _attn(q, k, v, bias, sink):
    b, s, _ = q.shape
    nb = s // BLOCK
    kv_spec = lambda f: pl.BlockSpec((1, BLOCK, ATTN_KV), f)
    prev = lambda bi, n, *_: (bi, jnp.maximum(n - 1, 0), 0)
    cur = lambda bi, n, *_: (bi, n, 0)
    nxt = lambda bi, n, *_: (bi, jnp.minimum(n + 1, nb - 1), 0)
    grid_spec = pltpu.PrefetchScalarGridSpec(
        num_scalar_prefetch=1,
        grid=(b, nb),
        in_specs=[pl.BlockSpec((1, BLOCK, ATTN_Q), cur),
                  kv_spec(prev), kv_spec(cur), kv_spec(nxt),
                  kv_spec(prev), kv_spec(cur), kv_spec(nxt),
                  pl.BlockSpec((N_HEADS, BLOCK, 3 * BLOCK), lambda *_: (0, 0, 0),
                               pipeline_mode=pl.Buffered(1))],
        out_specs=pl.BlockSpec((1, BLOCK, ATTN_Q), cur),
    )
    return pl.pallas_call(
        _attn_kernel,
        grid_spec=grid_spec,
        out_shape=jax.ShapeDtypeStruct((b, s, ATTN_Q), BF16),
        compiler_params=pltpu.CompilerParams(dimension_semantics=("parallel", "parallel"),
                                             vmem_limit_bytes=VMEM_LIMIT),
        name="attn",
    )(sink, q, k, k, k, v, v, v, bias)


def _s5_kernel(u_ref, m_ref, enc_ref, dec_ref, coef_ref, dsk_ref, y_ref):
    seq = u_ref.shape[0]
    nch = seq // CHUNK
    levels = int(math.log2(nch))
    parts = [u_ref[pl.ds(s, nch, stride=CHUNK), :] for s in range(CHUNK)]
    ucat = jnp.concatenate([p.astype(BF16) for p in parts], axis=1)
    y = jnp.dot(ucat, m_ref[0], preferred_element_type=F32)
    sloc = jnp.dot(ucat, enc_ref[0], preferred_element_type=F32)

    row = lax.broadcasted_iota(jnp.int32, (nch, 1), 0)

    def shifted(x, sh, forward):
        if forward:
            return jnp.where(row >= sh, pltpu.roll(x, sh, axis=0), 0.0)
        return jnp.where(row < nch - sh, pltpu.roll(x, nch - sh, axis=0), 0.0)

    carried = []
    for d in range(2):
        forward = d == 0
        base = d * 2 * SLAB_STATE
        gr = sloc[:, base:base + SLAB_STATE]
        gi = sloc[:, base + SLAB_STATE:base + 2 * SLAB_STATE]
        for lv in range(levels):
            ar = coef_ref[0, (d * levels + lv) * 2:(d * levels + lv) * 2 + 1, :]
            ai = coef_ref[0, (d * levels + lv) * 2 + 1:(d * levels + lv) * 2 + 2, :]
            sr = shifted(gr, 1 << lv, forward)
            si = shifted(gi, 1 << lv, forward)
            gr, gi = gr + (ar * sr - ai * si), gi + (ar * si + ai * sr)
        carried += [shifted(gr, 1, forward), shifted(gi, 1, forward)]
    xcat = jnp.concatenate(carried, axis=1).astype(BF16)
    y = y + jnp.dot(xcat, dec_ref[0], preferred_element_type=F32)
    dsk = dsk_ref[0]
    for t in range(CHUNK):
        y_ref[pl.ds(t, nch, stride=CHUNK), :] = y[:, t * LANES:(t + 1) * LANES] + dsk * parts[t]


def _s5(u2, m, enc, dec, coef, dsk, seq):
    n = u2.shape[0]
    b = n // seq
    per_slab = lambda a: pl.BlockSpec((1,) + a.shape[1:], lambda sl, bi: (sl,) + (0,) * (a.ndim - 1),
                                      pipeline_mode=pl.Buffered(1))
    io = pl.BlockSpec((seq, LANES), lambda sl, bi: (bi, sl))
    return pl.pallas_call(
        _s5_kernel,
        grid=(SLABS, b),
        in_specs=[io, per_slab(m), per_slab(enc), per_slab(dec), per_slab(coef), per_slab(dsk)],
        out_specs=io,
        out_shape=jax.ShapeDtypeStruct((n, SSM_WIDTH), F32),
        compiler_params=pltpu.CompilerParams(dimension_semantics=("parallel", "parallel"),
                                             vmem_limit_bytes=VMEM_LIMIT),
        name="s5",
    )(u2, m, enc, dec, coef, dsk)


def _cmul(ar, ai, br, bi):
    return ar * br - ai * bi, ar * bi + ai * br


def _s5_weights(lam_re, lam_im, log_dt, b_re, b_im, c_re, c_im, levels):
    dt = jnp.exp(log_dt.astype(F32))[..., None]
    lr, li = lam_re.astype(F32), lam_im.astype(F32)
    mag = jnp.exp(lr * dt)
    ang = li * dt
    a_re, a_im = mag * jnp.cos(ang), mag * jnp.sin(ang)
    den = lr * lr + li * li
    nr, ni = a_re - 1.0, a_im
    z_re = (nr * lr + ni * li) / den
    z_im = (ni * lr - nr * li) / den
    br, bi = b_re.astype(F32), b_im.astype(F32)
    bb_re = z_re[..., None] * br - z_im[..., None] * bi
    bb_im = z_re[..., None] * bi + z_im[..., None] * br
    cr = jnp.swapaxes(c_re.astype(F32), -1, -2)
    ci = jnp.swapaxes(c_im.astype(F32), -1, -2)

    pr, pi = [jnp.ones_like(a_re)], [jnp.zeros_like(a_im)]
    for _ in range(CHUNK):
        nr_, ni_ = _cmul(pr[-1], pi[-1], a_re, a_im)
        pr.append(nr_)
        pi.append(ni_)
    pw_re, pw_im = jnp.stack(pr, 0), jnp.stack(pi, 0)

    ab_re, ab_im = _cmul(pw_re[:CHUNK, ..., None], pw_im[:CHUNK, ..., None], bb_re[None], bb_im[None])
    kern = (jnp.einsum('dgpc,kdgpe->kdgce', cr, ab_re, precision='highest')
            - jnp.einsum('dgpc,kdgpe->kdgce', ci, ab_im, precision='highest'))
    tau_tab = jnp.concatenate([kern[::-1, 1][:-1], (kern[0, 0] + kern[0, 1])[None], kern[1:, 0]], 0)
    s_idx = np.arange(CHUNK)[:, None]
    t_idx = np.arange(CHUNK)[None, :]
    kst = tau_tab[t_idx - s_idx + CHUNK - 1]
    kst = kst.reshape(CHUNK, CHUNK, SLABS, GROUPS_PER_SLAB, SSM_GROUP, SSM_GROUP)
    eye = jnp.eye(GROUPS_PER_SLAB, dtype=F32)
    m = jnp.einsum('stagce,gh->asgethc', kst, eye).reshape(SLABS, CHUNK_COLS, CHUNK_COLS)

    k_enc = np.stack([CHUNK - 1 - np.arange(CHUNK), np.arange(CHUNK)], 0)
    d_ix = np.arange(2)[:, None]
    e_re, e_im = _cmul(pw_re[k_enc, d_ix][..., None], pw_im[k_enc, d_ix][..., None],
                       bb_re[:, None], bb_im[:, None])
    e = jnp.stack([e_re, e_im], 2)
    e = e.reshape(2, CHUNK, 2, SLABS, GROUPS_PER_SLAB, SSM_STATE, SSM_GROUP)
    enc = jnp.einsum('dsragpe,gh->asgedrhp', e, eye).reshape(SLABS, CHUNK_COLS, 4 * SLAB_STATE)

    k_dec = np.stack([np.arange(CHUNK) + 1, CHUNK - np.arange(CHUNK)], 0)
    a_t_re, a_t_im = pw_re[k_dec, d_ix][..., None], pw_im[k_dec, d_ix][..., None]
    f_re, f_im = _cmul(a_t_re, a_t_im, cr[:, None], ci[:, None])
    f = jnp.stack([f_re, -f_im], 1)
    f = f.reshape(2, 2, CHUNK, SLABS, GROUPS_PER_SLAB, SSM_STATE, SSM_GROUP)
    dec = jnp.einsum('drtagpc,gh->adrgpthc', f, eye).reshape(SLABS, 4 * SLAB_STATE, CHUNK_COLS)

    sr_, si_ = pw_re[CHUNK], pw_im[CHUNK]
    coefs = []
    for _ in range(levels):
        coefs.append(jnp.stack([sr_, si_], 1))
        sr_, si_ = _cmul(sr_, si_, sr_, si_)
    coef = jnp.stack(coefs, 1)
    coef = coef.reshape(2, levels, 2, SLABS, SLAB_STATE)
    coef = jnp.transpose(coef, (3, 0, 1, 2, 4)).reshape(SLABS, 2 * levels * 2, SLAB_STATE)
    return m.astype(BF16), enc.astype(BF16), dec.astype(BF16), coef


def _post_kernel(x_ref, ya_ref, ys_ref, g_ref, wglu_ref, bglu_ref, wout_ref, ln2_ref,
                 wup_ref, wdown_ref, o_ref):
    y = jax.nn.gelu(ys_ref[...])
    z = jnp.dot(y.astype(BF16), wglu_ref[...], preferred_element_type=F32) + bglu_ref[...]
    y = y * jax.nn.sigmoid(z)
    g = g_ref[...].astype(F32)
    mixed = (jax.nn.sigmoid(g[:, :D_MODEL]) * ya_ref[...].astype(F32)
             + jax.nn.sigmoid(g[:, D_MODEL:]) * y)
    x = x_ref[...] + jnp.dot(mixed.astype(BF16), wout_ref[...], preferred_element_type=F32)
    ms = jnp.mean(x * x, axis=-1, keepdims=True)
    h = (x * lax.rsqrt(ms + EPS) * ln2_ref[...]).astype(BF16)
    up = jnp.dot(h, wup_ref[...], preferred_element_type=F32)
    act = jnp.square(jnp.maximum(up, 0.0)).astype(BF16)
    o_ref[...] = x + jnp.dot(act, wdown_ref[...], preferred_element_type=F32)


def _post(x2, ya, ys, g, w_glu, b_glu, w_out, ln2, w_up, w_down, tm):
    n = x2.shape[0]
    row = lambda c: pl.BlockSpec((tm, c), lambda i: (i, 0))
    return pl.pallas_call(
        _post_kernel,
        grid=(n // tm,),
        in_specs=[row(D_MODEL), row(D_MODEL), row(D_MODEL), row(2 * D_MODEL),
                  _const_spec((D_MODEL, D_MODEL)), _const_spec((1, D_MODEL)),
                  _const_spec((D_MODEL, D_MODEL)), _const_spec((1, D_MODEL)),
                  _const_spec((D_MODEL, D_FF)), _const_spec((D_FF, D_MODEL))],
        out_specs=row(D_MODEL),
        out_shape=jax.ShapeDtypeStruct((n, D_MODEL), F32),
        compiler_params=pltpu.CompilerParams(dimension_semantics=("parallel",),
                                             vmem_limit_bytes=VMEM_LIMIT),
        name="post",
    )(x2, ya, ys, g, w_glu, b_glu, w_out, ln2, w_up, w_down)


def _t5_bucket(rel):
    nb = N_BUCKETS // 2
    ret = (rel > 0).astype(np.int32) * nb
    n = np.abs(rel)
    max_exact = nb // 2
    n_safe = np.maximum(n, 1).astype(np.float32)
    large = max_exact + (np.log(n_safe / max_exact) / math.log(MAX_DISTANCE / max_exact)
                         * (nb - max_exact)).astype(np.int32)
    large = np.minimum(large, nb - 1)
    return (ret + np.where(n < max_exact, n, large)).astype(np.int32)


def _band_bias(rel_table):
    qi = np.arange(BLOCK)[:, None]
    kj = np.arange(3 * BLOCK)[None, :]
    rel = kj - BLOCK - qi
    bias = jnp.transpose(rel_table.astype(F32)[_t5_bucket(rel)], (2, 0, 1))
    return jnp.where(jnp.asarray(np.abs(rel) <= WINDOW)[None], bias, NEG_INF)


def _head_mats(width):
    heads = width // HEAD_DIM
    sel = (np.arange(width)[:, None] // HEAD_DIM == np.arange(LANES)[None, :]).astype(np.float32)
    assert heads <= LANES
    return jnp.asarray(sel / HEAD_DIM, BF16), jnp.asarray(sel.T, BF16)


def _layer(x, bias, sink, ln1, w_in, qg, kg, s5w, dsk, w_glu, b_glu, w_out, ln2, w_up, w_down,
           tm_in, tm_post):
    b, s, _ = x.shape
    x2 = x.reshape(b * s, D_MODEL)
    sq, eq = _head_mats(ATTN_Q)
    sk, ek = _head_mats(ATTN_KV)
    q, k, v, u, g = _in_proj(x2, ln1, w_in, qg, kg, sq, eq, sk, ek, tm_in)
    ya = _attn(q.reshape(b, s, ATTN_Q), k.reshape(b, s, ATTN_KV), v.reshape(b, s, ATTN_KV),
               bias, sink)
    ys = _s5(u, *s5w, dsk, s)
    out = _post(x2, ya.reshape(b * s, ATTN_Q), ys, g, w_glu, b_glu, w_out, ln2, w_up, w_down,
                tm_post)
    return out.reshape(b, s, D_MODEL)


def kernel(x_prompt, x_sample, rel_table, ln1, w_in, q_gain, k_gain, sink, lam_re, lam_im, log_dt,
           b_re, b_im, c_re, c_im, d_skip, w_glu, b_glu, w_out, ln2, w_up, w_down):
    assert ln1.shape[0] == 1, "single layer"
    bias = _band_bias(rel_table)
    outs = []
    s5w_by_levels = {}
    for x in (x_prompt, x_sample):
        seq = x.shape[1]
        levels = int(math.log2(seq // CHUNK))
        assert CHUNK << levels == seq and seq % BLOCK == 0
        if levels not in s5w_by_levels:
            s5w_by_levels[levels] = _s5_weights(lam_re[0], lam_im[0], log_dt[0], b_re[0], b_im[0],
                                                c_re[0], c_im[0], levels)
        s5w = s5w_by_levels[levels]
        outs.append(_layer(
            x, bias, sink[0].astype(F32),
            ln1[0].astype(F32)[None], w_in[0].astype(BF16),
            jnp.tile(q_gain[0].astype(F32), N_HEADS)[None],
            jnp.tile(k_gain[0].astype(F32), N_KV_HEADS)[None],
            s5w, d_skip[0].astype(F32).reshape(SLABS, 1, LANES),
            w_glu[0].astype(BF16), b_glu[0].astype(F32)[None], w_out[0].astype(BF16),
            ln2[0].astype(F32)[None], w_up[0].astype(BF16), w_down[0].astype(BF16),
            tm_in=min(512, x.shape[0] * seq), tm_post=min(256, x.shape[0] * seq)))
    return tuple(outs)
```

```python
import math

import numpy as np
import jax
import jax.numpy as jnp
from jax import lax
from jax.experimental import pallas as pl
from jax.experimental.pallas import tpu as pltpu

D_MODEL = 1024
N_HEADS = 16
N_KV_HEADS = 4
HEAD_DIM = 64
Q_PER_KV = N_HEADS // N_KV_HEADS
WINDOW = 128
BLOCK = 128
N_BUCKETS = 32
MAX_DISTANCE = 128
ATTN_Q = N_HEADS * HEAD_DIM
ATTN_KV = N_KV_HEADS * HEAD_DIM
NEG_INF = -1e30
SSM_WIDTH = D_MODEL
SSM_GROUP = 16
SSM_GROUPS = SSM_WIDTH // SSM_GROUP
SSM_STATE = 64
D_FF = 4 * D_MODEL
EPS = 1e-6
IN_COLS = ATTN_Q + 2 * ATTN_KV + SSM_WIDTH + 2 * D_MODEL

LANES = 128
CHUNK = 16
SLABS = SSM_WIDTH // LANES
GROUPS_PER_SLAB = LANES // SSM_GROUP
SLAB_STATE = GROUPS_PER_SLAB * SSM_STATE
CHUNK_COLS = CHUNK * LANES
VMEM_LIMIT = 56 * 1024 * 1024

LOG2E = math.log2(math.e)
Q_SCALE = HEAD_DIM ** -0.5 * LOG2E

BF16 = jnp.bfloat16
F32 = jnp.float32


def _const_spec(shape):
    nd = len(shape)
    return pl.BlockSpec(shape, lambda *_: (0,) * nd, pipeline_mode=pl.Buffered(1))


def _split_dot(r, expand):
    hi = r.astype(BF16)
    lo = (r - hi.astype(F32)).astype(BF16)
    return (jnp.dot(hi, expand, preferred_element_type=F32)
            + jnp.dot(lo, expand, preferred_element_type=F32))


def _in_proj_kernel(x_ref, ln1_ref, w_ref, qg_ref, kg_ref, sq_ref, eq_ref, sk_ref, ek_ref,
                    q_ref, kv_ref, u_ref, g_ref):
    x = x_ref[...]
    ms = jnp.mean(x * x, axis=-1, keepdims=True)
    h = (x * lax.rsqrt(ms + EPS) * ln1_ref[...]).astype(BF16)

    def head_norm(t, s_ref, e_ref, gain):
        msq = jnp.dot((t * t).astype(BF16), s_ref[...], preferred_element_type=F32)
        r = lax.rsqrt(msq + EPS)
        return t * _split_dot(r, e_ref[...]) * gain

    c0 = 0
    q = jnp.dot(h, w_ref[:, c0:c0 + ATTN_Q], preferred_element_type=F32)
    q_ref[...] = (head_norm(q, sq_ref, eq_ref, qg_ref[...]) * Q_SCALE).astype(BF16)
    c0 += ATTN_Q
    k = jnp.dot(h, w_ref[:, c0:c0 + ATTN_KV], preferred_element_type=F32)
    k = head_norm(k, sk_ref, ek_ref, kg_ref[...])
    c0 += ATTN_KV
    v = jnp.dot(h, w_ref[:, c0:c0 + ATTN_KV], preferred_element_type=F32)
    c0 += ATTN_KV
    kv_ref[:, 0 * ATTN_KV:1 * ATTN_KV] = k.astype(BF16)
    kv_ref[:, 1 * ATTN_KV:2 * ATTN_KV] = pltpu.roll(k, HEAD_DIM, axis=1).astype(BF16)
    kv_ref[:, 2 * ATTN_KV:3 * ATTN_KV] = v.astype(BF16)
    kv_ref[:, 3 * ATTN_KV:4 * ATTN_KV] = pltpu.roll(v, HEAD_DIM, axis=1).astype(BF16)
    u_ref[...] = jnp.dot(h, w_ref[:, c0:c0 + SSM_WIDTH], preferred_element_type=F32)
    c0 += SSM_WIDTH
    g_ref[...] = jnp.dot(h, w_ref[:, c0:c0 + 2 * D_MODEL], preferred_element_type=F32).astype(BF16)


def _in_proj(x2, ln1, w_in, qg, kg, sq, eq, sk, ek, tm):
    n = x2.shape[0]
    row = lambda c: pl.BlockSpec((tm, c), lambda i: (i, 0))
    return pl.pallas_call(
        _in_proj_kernel,
        grid=(n // tm,),
        in_specs=[row(D_MODEL), _const_spec((1, D_MODEL)), _const_spec((D_MODEL, IN_COLS)),
                  _const_spec((1, ATTN_Q)), _const_spec((1, ATTN_KV)),
                  _const_spec((ATTN_Q, LANES)), _const_spec((LANES, ATTN_Q)),
                  _const_spec((ATTN_KV, LANES)), _const_spec((LANES, ATTN_KV))],
        out_specs=[row(ATTN_Q), row(4 * ATTN_KV), row(SSM_WIDTH), row(2 * D_MODEL)],
        out_shape=[jax.ShapeDtypeStruct((n, ATTN_Q), BF16),
                   jax.ShapeDtypeStruct((n, 4 * ATTN_KV), BF16),
                   jax.ShapeDtypeStruct((n, SSM_WIDTH), F32),
                   jax.ShapeDtypeStruct((n, 2 * D_MODEL), BF16)],
        compiler_params=pltpu.CompilerParams(dimension_semantics=("parallel",),
                                             vmem_limit_bytes=VMEM_LIMIT),
        name="in_proj",
    )(x2, ln1, w_in, qg, kg, sq, eq, sk, ek)


_ROLLED_TILE = (0, 1, 1, 0)


def _attn_kernel(sink_ref, q_ref, kvp_ref, kvc_ref, kvn_ref, bias_ref, ones_ref, o_ref):
    kvx = jnp.concatenate([kvp_ref[0], kvc_ref[0], kvn_ref[0]], axis=0)
    low = lax.broadcasted_iota(jnp.int32, (1, LANES), 1) < HEAD_DIM
    zero = jnp.zeros((), BF16)

    def halves(base, kv):
        plain = kvx[:, base + (kv // 2) * LANES:base + (kv // 2 + 1) * LANES]
        r0 = base + ATTN_KV + _ROLLED_TILE[kv] * LANES
        rolled = kvx[:, r0:r0 + LANES]
        lo_src, hi_src = (plain, rolled) if kv % 2 == 0 else (rolled, plain)
        return jnp.where(low, lo_src, zero), jnp.where(low, zero, hi_src)

    outs = []
    for kv in range(N_KV_HEADS):
        ke, ko = halves(0, kv)
        ve, vo = halves(2 * ATTN_KV, kv)
        keys = jnp.concatenate([ke, ko], axis=0)
        vals = jnp.concatenate([jnp.concatenate([ve, vo], axis=0), ones_ref[...]], axis=1)
        q2 = jnp.concatenate([q_ref[0, :, (2 * kv) * LANES:(2 * kv + 1) * LANES],
                              q_ref[0, :, (2 * kv + 1) * LANES:(2 * kv + 2) * LANES]], axis=0)
        s2 = lax.dot_general(q2, keys, (((1,), (1,)), ((), ())), preferred_element_type=F32)
        for half in range(2):
            j = 2 * kv + half
            s = s2[half * BLOCK:(half + 1) * BLOCK] + bias_ref[0, j]
            se, so = s[:, :3 * BLOCK], s[:, 3 * BLOCK:]
            sink_e, sink_o = sink_ref[2 * j], sink_ref[2 * j + 1]
            me = jnp.maximum(jnp.max(se, axis=-1, keepdims=True), sink_e)
            mo = jnp.maximum(jnp.max(so, axis=-1, keepdims=True), sink_o)
            p = jnp.concatenate([jnp.exp2(se - me), jnp.exp2(so - mo)], axis=1).astype(BF16)
            od = jnp.dot(p, vals, preferred_element_type=F32)
            den = od[:, LANES:] + jnp.where(low, jnp.exp2(sink_e - me), jnp.exp2(sink_o - mo))
            outs.append(od[:, :LANES] / den)
    o_ref[0] = jnp.concatenate(outs, axis=-1).astype(BF16)


def _attn(q, kvx, bias, ones, sink):
    b, s, _ = q.shape
    nb = s // BLOCK
    assert nb >= 2
    kv_spec = lambda f: pl.BlockSpec((1, BLOCK, 4 * ATTN_KV), f)
    prev = lambda bi, n, *_: (bi, jnp.maximum(n - 1, 0), 0)
    cur = lambda bi, n, *_: (bi, n, 0)
    nxt = lambda bi, n, *_: (bi, jnp.minimum(n + 1, nb - 1), 0)
    edge = lambda bi, n, *_: (jnp.where(n == 0, 0, jnp.where(n == nb - 1, 2, 1)), 0, 0, 0)
    grid_spec = pltpu.PrefetchScalarGridSpec(
        num_scalar_prefetch=1,
        grid=(b, nb),
        in_specs=[pl.BlockSpec((1, BLOCK, ATTN_Q), cur),
                  kv_spec(prev), kv_spec(cur), kv_spec(nxt),
                  pl.BlockSpec((1, N_HEADS // 2, BLOCK, 6 * BLOCK), edge),
                  pl.BlockSpec((6 * BLOCK, LANES), lambda *_: (0, 0), pipeline_mode=pl.Buffered(1))],
        out_specs=pl.BlockSpec((1, BLOCK, ATTN_Q), cur),
    )
    return pl.pallas_call(
        _attn_kernel,
        grid_spec=grid_spec,
        out_shape=jax.ShapeDtypeStruct((b, s, ATTN_Q), BF16),
        compiler_params=pltpu.CompilerParams(dimension_semantics=("parallel", "arbitrary"),
                                             vmem_limit_bytes=VMEM_LIMIT),
        name="attn",
    )(sink, q, kvx, kvx, kvx, bias, ones)


def _s5_kernel(u_ref, m_ref, enc_ref, dec_ref, coef_ref, dsk_ref, y_ref):
    seq = u_ref.shape[0]
    nch = seq // CHUNK
    levels = int(math.log2(nch))
    parts = [u_ref[pl.ds(s, nch, stride=CHUNK), :] for s in range(CHUNK)]
    ucat = jnp.concatenate([p.astype(BF16) for p in parts], axis=1)
    y = jnp.dot(ucat, m_ref[0], preferred_element_type=F32)
    sloc = jnp.dot(ucat, enc_ref[0], preferred_element_type=F32)

    row = lax.broadcasted_iota(jnp.int32, (nch, 1), 0)

    def shifted(x, sh, forward):
        if forward:
            return jnp.where(row >= sh, pltpu.roll(x, sh, axis=0), 0.0)
        return jnp.where(row < nch - sh, pltpu.roll(x, nch - sh, axis=0), 0.0)

    carried = []
    for d in range(2):
        forward = d == 0
        base = d * 2 * SLAB_STATE
        gr = sloc[:, base:base + SLAB_STATE]
        gi = sloc[:, base + SLAB_STATE:base + 2 * SLAB_STATE]
        for lv in range(levels):
            ar = coef_ref[0, (d * levels + lv) * 2:(d * levels + lv) * 2 + 1, :]
            ai = coef_ref[0, (d * levels + lv) * 2 + 1:(d * levels + lv) * 2 + 2, :]
            sr = shifted(gr, 1 << lv, forward)
            si = shifted(gi, 1 << lv, forward)
            gr, gi = gr + (ar * sr - ai * si), gi + (ar * si + ai * sr)
        carried += [shifted(gr, 1, forward), shifted(gi, 1, forward)]
    xcat = jnp.concatenate(carried, axis=1).astype(BF16)
    y = y + jnp.dot(xcat, dec_ref[0], preferred_element_type=F32)
    dsk = dsk_ref[0]
    for t in range(CHUNK):
        y_ref[pl.ds(t, nch, stride=CHUNK), :] = y[:, t * LANES:(t + 1) * LANES] + dsk * parts[t]


def _s5(u2, m, enc, dec, coef, dsk, seq):
    n = u2.shape[0]
    b = n // seq
    per_slab = lambda a: pl.BlockSpec((1,) + a.shape[1:], lambda sl, bi: (sl,) + (0,) * (a.ndim - 1),
                                      pipeline_mode=pl.Buffered(1))
    io = pl.BlockSpec((seq, LANES), lambda sl, bi: (bi, sl))
    return pl.pallas_call(
        _s5_kernel,
        grid=(SLABS, b),
        in_specs=[io, per_slab(m), per_slab(enc), per_slab(dec), per_slab(coef), per_slab(dsk)],
        out_specs=io,
        out_shape=jax.ShapeDtypeStruct((n, SSM_WIDTH), F32),
        compiler_params=pltpu.CompilerParams(dimension_semantics=("parallel", "parallel"),
                                             vmem_limit_bytes=VMEM_LIMIT),
        name="s5",
    )(u2, m, enc, dec, coef, dsk)


def _cmul(ar, ai, br, bi):
    return ar * br - ai * bi, ar * bi + ai * br


def _dot_nt_f32(a, b):
    return lax.dot_general(a, b, (((1,), (1,)), ((), ())), precision=lax.Precision.HIGHEST,
                           preferred_element_type=F32)


def _s5_gen_kernel(bm_ref, cm_ref, apow_ref, m_ref, enc_ref, dec_ref, lag_ref):
    i = pl.program_id(1)

    def power(d, k):
        return apow_ref[0, d, 0, pl.ds(k, 1), :], apow_ref[0, d, 1, pl.ds(k, 1), :]

    @pl.when(i == 0)
    def _():
        for d in range(2):
            c_cat = jnp.concatenate([cm_ref[0, d, 0], -cm_ref[0, d, 1]], axis=1)
            for k in range(CHUNK):
                e_re, e_im = _cmul(bm_ref[0, d, 0], bm_ref[0, d, 1], *power(d, k))
                lag_ref[d, k] = _dot_nt_f32(jnp.concatenate([e_re, e_im], axis=1), c_cat)

    for d in range(2):
        col = d * 2 * SLAB_STATE
        k_enc = CHUNK - 1 - i if d == 0 else i
        e_re, e_im = _cmul(bm_ref[0, d, 0], bm_ref[0, d, 1], *power(d, k_enc))
        enc_ref[0, :, col:col + SLAB_STATE] = e_re.astype(BF16)
        enc_ref[0, :, col + SLAB_STATE:col + 2 * SLAB_STATE] = e_im.astype(BF16)
        k_dec = i + 1 if d == 0 else CHUNK - i
        f_re, f_im = _cmul(cm_ref[0, d, 0], cm_ref[0, d, 1], *power(d, k_dec))
        dec_ref[0, col:col + SLAB_STATE, :] = f_re.T.astype(BF16)
        dec_ref[0, col + SLAB_STATE:col + 2 * SLAB_STATE, :] = (-f_im).T.astype(BF16)

    both = lag_ref[0, 0] + lag_ref[1, 0]
    for t in range(CHUNK):
        fwd = lag_ref[0, jnp.clip(t - i, 0, CHUNK - 1)]
        bwd = lag_ref[1, jnp.clip(i - t, 0, CHUNK - 1)]
        blk = jnp.where(t > i, fwd, jnp.where(t < i, bwd, both))
        m_ref[0, :, t * LANES:(t + 1) * LANES] = blk.astype(BF16)


def _s5_gen(bm, cm, apow):
    small = lambda a: pl.BlockSpec((1,) + a.shape[1:], lambda sl, i: (sl,) + (0,) * (a.ndim - 1))
    rows = pl.BlockSpec((1, LANES, CHUNK_COLS), lambda sl, i: (sl, i, 0))
    wide = 4 * SLAB_STATE
    return pl.pallas_call(
        _s5_gen_kernel,
        grid=(SLABS, CHUNK),
        in_specs=[small(bm), small(cm), small(apow)],
        out_specs=[rows, pl.BlockSpec((1, LANES, wide), lambda sl, i: (sl, i, 0)),
                   pl.BlockSpec((1, wide, LANES), lambda sl, i: (sl, 0, i))],
        out_shape=[jax.ShapeDtypeStruct((SLABS, CHUNK_COLS, CHUNK_COLS), BF16),
                   jax.ShapeDtypeStruct((SLABS, CHUNK_COLS, wide), BF16),
                   jax.ShapeDtypeStruct((SLABS, wide, CHUNK_COLS), BF16)],
        scratch_shapes=[pltpu.VMEM((2, CHUNK, LANES, LANES), F32)],
        compiler_params=pltpu.CompilerParams(dimension_semantics=("parallel", "arbitrary"),
                                             vmem_limit_bytes=VMEM_LIMIT),
        name="s5_gen",
    )(bm, cm, apow)


def _slab_tiles(t_re, t_im):
    t = jnp.stack([t_re, t_im], 1).astype(F32)
    t = t.reshape(2, 2, SLABS, GROUPS_PER_SLAB, SSM_GROUP, SSM_STATE)
    eye = jnp.eye(GROUPS_PER_SLAB, dtype=F32)
    t = t[:, :, :, :, :, None, :] * eye[:, None, :, None]
    t = t.reshape(2, 2, SLABS, LANES, SLAB_STATE)
    return jnp.transpose(t, (2, 0, 1, 3, 4))


def _s5_weights(lam_re, lam_im, log_dt, b_re, b_im, c_re, c_im, levels):
    dt = jnp.exp(log_dt.astype(F32))[..., None]
    lr, li = lam_re.astype(F32), lam_im.astype(F32)
    mag = jnp.exp(lr * dt)
    ang = li * dt
    a_re, a_im = mag * jnp.cos(ang), mag * jnp.sin(ang)
    den = lr * lr + li * li
    nr, ni = a_re - 1.0, a_im
    z_re = (nr * lr + ni * li) / den
    z_im = (ni * lr - nr * li) / den
    br, bi = b_re.astype(F32), b_im.astype(F32)
    bb_re = z_re[..., None] * br - z_im[..., None] * bi
    bb_im = z_re[..., None] * bi + z_im[..., None] * br

    pr, pi = [jnp.ones_like(a_re)], [jnp.zeros_like(a_im)]
    for _ in range(CHUNK):
        nr_, ni_ = _cmul(pr[-1], pi[-1], a_re, a_im)
        pr.append(nr_)
        pi.append(ni_)
    apow = jnp.stack([jnp.stack(pr, 1), jnp.stack(pi, 1)], 1)
    apow = apow.reshape(2, 2, CHUNK + 1, SLABS, SLAB_STATE)
    apow = jnp.transpose(apow, (3, 0, 1, 2, 4))

    bm = _slab_tiles(jnp.swapaxes(bb_re, -1, -2), jnp.swapaxes(bb_im, -1, -2))
    cm = _slab_tiles(c_re, c_im)
    m, enc, dec = _s5_gen(bm, cm, apow)

    sr_, si_ = pr[CHUNK], pi[CHUNK]
    coefs = []
    for _ in range(levels):
        coefs.append(jnp.stack([sr_, si_], 1))
        sr_, si_ = _cmul(sr_, si_, sr_, si_)
    coef = jnp.stack(coefs, 1)
    coef = coef.reshape(2, levels, 2, SLABS, SLAB_STATE)
    coef = jnp.transpose(coef, (3, 0, 1, 2, 4)).reshape(SLABS, 2 * levels * 2, SLAB_STATE)
    return m, enc, dec, coef


def _post_kernel(x_ref, ya_ref, ys_ref, g_ref, wglu_ref, bglu_ref, wout_ref, ln2_ref,
                 wup_ref, wdown_ref, o_ref):
    y = jax.nn.gelu(ys_ref[...])
    z = jnp.dot(y.astype(BF16), wglu_ref[...], preferred_element_type=F32) + bglu_ref[...]
    y = y * jax.nn.sigmoid(z)
    g = g_ref[...].astype(F32)
    mixed = (jax.nn.sigmoid(g[:, :D_MODEL]) * ya_ref[...].astype(F32)
             + jax.nn.sigmoid(g[:, D_MODEL:]) * y)
    x = x_ref[...] + jnp.dot(mixed.astype(BF16), wout_ref[...], preferred_element_type=F32)
    ms = jnp.mean(x * x, axis=-1, keepdims=True)
    h = (x * lax.rsqrt(ms + EPS) * ln2_ref[...]).astype(BF16)
    up = jnp.dot(h, wup_ref[...], preferred_element_type=F32)
    act = jnp.square(jnp.maximum(up, 0.0)).astype(BF16)
    o_ref[...] = x + jnp.dot(act, wdown_ref[...], preferred_element_type=F32)


def _post(x2, ya, ys, g, w_glu, b_glu, w_out, ln2, w_up, w_down, tm):
    n = x2.shape[0]
    row = lambda c: pl.BlockSpec((tm, c), lambda i: (i, 0))
    return pl.pallas_call(
        _post_kernel,
        grid=(n // tm,),
        in_specs=[row(D_MODEL), row(D_MODEL), row(D_MODEL), row(2 * D_MODEL),
                  _const_spec((D_MODEL, D_MODEL)), _const_spec((1, D_MODEL)),
                  _const_spec((D_MODEL, D_MODEL)), _const_spec((1, D_MODEL)),
                  _const_spec((D_MODEL, D_FF)), _const_spec((D_FF, D_MODEL))],
        out_specs=row(D_MODEL),
        out_shape=jax.ShapeDtypeStruct((n, D_MODEL), F32),
        compiler_params=pltpu.CompilerParams(dimension_semantics=("parallel",),
                                             vmem_limit_bytes=VMEM_LIMIT),
        name="post",
    )(x2, ya, ys, g, w_glu, b_glu, w_out, ln2, w_up, w_down)


def _t5_bucket(rel):
    nb = N_BUCKETS // 2
    ret = (rel > 0).astype(np.int32) * nb
    n = np.abs(rel)
    max_exact = nb // 2
    n_safe = np.maximum(n, 1).astype(np.float32)
    large = max_exact + (np.log(n_safe / max_exact) / math.log(MAX_DISTANCE / max_exact)
                         * (nb - max_exact)).astype(np.int32)
    large = np.minimum(large, nb - 1)
    return (ret + np.where(n < max_exact, n, large)).astype(np.int32)


def _band_bias(rel_table):
    qi = np.arange(BLOCK)[:, None]
    kj = np.arange(3 * BLOCK)[None, :]
    rel = kj - BLOCK - qi
    bias = jnp.transpose(rel_table.astype(F32)[_t5_bucket(rel)], (2, 0, 1)) * LOG2E
    in_band = np.abs(rel) <= WINDOW
    keep = np.stack([in_band & (kj >= BLOCK), in_band, in_band & (kj < 2 * BLOCK)], 0)
    bias = jnp.where(jnp.asarray(keep)[:, None], bias[None], NEG_INF)
    bias = bias.reshape(3, N_HEADS // 2, 2, BLOCK, 3 * BLOCK)
    return jnp.transpose(bias, (0, 1, 3, 2, 4)).reshape(3, N_HEADS // 2, BLOCK, 6 * BLOCK)


def _sum_mat():
    first = np.arange(6 * BLOCK)[:, None] < 3 * BLOCK
    low = np.arange(LANES)[None, :] < HEAD_DIM
    return jnp.asarray(first == low, BF16)


def _head_mats(width):
    heads = width // HEAD_DIM
    sel = (np.arange(width)[:, None] // HEAD_DIM == np.arange(LANES)[None, :]).astype(np.float32)
    assert heads <= LANES
    return jnp.asarray(sel / HEAD_DIM, BF16), jnp.asarray(sel.T, BF16)


def _layer(x, bias, sink, ln1, w_in, qg, kg, s5w, dsk, w_glu, b_glu, w_out, ln2, w_up, w_down,
           tm_in, tm_post):
    b, s, _ = x.shape
    x2 = x.reshape(b * s, D_MODEL)
    sq, eq = _head_mats(ATTN_Q)
    sk, ek = _head_mats(ATTN_KV)
    q, kvx, u, g = _in_proj(x2, ln1, w_in, qg, kg, sq, eq, sk, ek, tm_in)
    ya = _attn(q.reshape(b, s, ATTN_Q), kvx.reshape(b, s, 4 * ATTN_KV), bias, _sum_mat(), sink)
    ys = _s5(u, *s5w, dsk, s)
    out = _post(x2, ya.reshape(b * s, ATTN_Q), ys, g, w_glu, b_glu, w_out, ln2, w_up, w_down,
                tm_post)
    return out.reshape(b, s, D_MODEL)


def kernel(x_prompt, x_sample, rel_table, ln1, w_in, q_gain, k_gain, sink, lam_re, lam_im, log_dt,
           b_re, b_im, c_re, c_im, d_skip, w_glu, b_glu, w_out, ln2, w_up, w_down):
    assert ln1.shape[0] == 1, "single layer"
    bias = _band_bias(rel_table)
    outs = []
    s5w_by_levels = {}
    for x in (x_prompt, x_sample):
        seq = x.shape[1]
        levels = int(math.log2(seq // CHUNK))
        assert CHUNK << levels == seq and seq % BLOCK == 0
        if levels not in s5w_by_levels:
            s5w_by_levels[levels] = _s5_weights(lam_re[0], lam_im[0], log_dt[0], b_re[0], b_im[0],
                                                c_re[0], c_im[0], levels)
        s5w = s5w_by_levels[levels]
        outs.append(_layer(
            x, bias, sink[0].astype(F32) * LOG2E,
            ln1[0].astype(F32)[None], w_in[0].astype(BF16),
            jnp.tile(q_gain[0].astype(F32), N_HEADS)[None],
            jnp.tile(k_gain[0].astype(F32), N_KV_HEADS)[None],
            s5w, d_skip[0].astype(F32).reshape(SLABS, 1, LANES),
            w_glu[0].astype(BF16), b_glu[0].astype(F32)[None], w_out[0].astype(BF16),
            ln2[0].astype(F32)[None], w_up[0].astype(BF16), w_down[0].astype(BF16),
            tm_in=min(512, x.shape[0] * seq), tm_post=min(256, x.shape[0] * seq)))
    return tuple(outs)
```

```python
import math

import numpy as np
import jax
import jax.numpy as jnp
from jax import lax
from jax.experimental import pallas as pl
from jax.experimental.pallas import tpu as pltpu

D_MODEL = 1024
N_HEADS = 16
N_KV_HEADS = 4
HEAD_DIM = 64
Q_PER_KV = N_HEADS // N_KV_HEADS
WINDOW = 128
BLOCK = 128
N_BUCKETS = 32
MAX_DISTANCE = 128
ATTN_Q = N_HEADS * HEAD_DIM
ATTN_KV = N_KV_HEADS * HEAD_DIM
NEG_INF = -1e30
SSM_WIDTH = D_MODEL
SSM_GROUP = 16
SSM_GROUPS = SSM_WIDTH // SSM_GROUP
SSM_STATE = 64
D_FF = 4 * D_MODEL
EPS = 1e-6
IN_COLS = ATTN_Q + 2 * ATTN_KV + SSM_WIDTH + 2 * D_MODEL

LANES = 128
CHUNK = 16
SLABS = SSM_WIDTH // LANES
GROUPS_PER_SLAB = LANES // SSM_GROUP
SLAB_STATE = GROUPS_PER_SLAB * SSM_STATE
CHUNK_COLS = CHUNK * LANES
VMEM_LIMIT = 56 * 1024 * 1024

LOG2E = math.log2(math.e)
Q_SCALE = HEAD_DIM ** -0.5 * LOG2E

BF16 = jnp.bfloat16
F32 = jnp.float32


def _const_spec(shape):
    nd = len(shape)
    return pl.BlockSpec(shape, lambda *_: (0,) * nd, pipeline_mode=pl.Buffered(1))


def _split_dot(r, expand):
    hi = r.astype(BF16)
    lo = (r - hi.astype(F32)).astype(BF16)
    return (jnp.dot(hi, expand, preferred_element_type=F32)
            + jnp.dot(lo, expand, preferred_element_type=F32))


def _in_proj_kernel(x_ref, ln1_ref, w_ref, qg_ref, kg_ref, sq_ref, eq_ref, sk_ref, ek_ref,
                    q_ref, kv_ref, u_ref, g_ref):
    x = x_ref[...]
    ms = jnp.mean(x * x, axis=-1, keepdims=True)
    h = (x * lax.rsqrt(ms + EPS) * ln1_ref[...]).astype(BF16)

    def head_norm(t, s_ref, e_ref, gain):
        msq = jnp.dot((t * t).astype(BF16), s_ref[...], preferred_element_type=F32)
        r = lax.rsqrt(msq + EPS)
        return t * _split_dot(r, e_ref[...]) * gain

    c0 = 0
    q = jnp.dot(h, w_ref[:, c0:c0 + ATTN_Q], preferred_element_type=F32)
    q_ref[...] = (head_norm(q, sq_ref, eq_ref, qg_ref[...]) * Q_SCALE).astype(BF16)
    c0 += ATTN_Q
    k = jnp.dot(h, w_ref[:, c0:c0 + ATTN_KV], preferred_element_type=F32)
    k = head_norm(k, sk_ref, ek_ref, kg_ref[...])
    c0 += ATTN_KV
    v = jnp.dot(h, w_ref[:, c0:c0 + ATTN_KV], preferred_element_type=F32)
    c0 += ATTN_KV
    kv_ref[:, 0 * ATTN_KV:1 * ATTN_KV] = k.astype(BF16)
    kv_ref[:, 1 * ATTN_KV:2 * ATTN_KV] = pltpu.roll(k, HEAD_DIM, axis=1).astype(BF16)
    kv_ref[:, 2 * ATTN_KV:3 * ATTN_KV] = v.astype(BF16)
    kv_ref[:, 3 * ATTN_KV:4 * ATTN_KV] = pltpu.roll(v, HEAD_DIM, axis=1).astype(BF16)
    u_ref[...] = jnp.dot(h, w_ref[:, c0:c0 + SSM_WIDTH], preferred_element_type=F32)
    c0 += SSM_WIDTH
    g_ref[...] = jnp.dot(h, w_ref[:, c0:c0 + 2 * D_MODEL], preferred_element_type=F32).astype(BF16)


def _in_proj(x2, ln1, w_in, qg, kg, sq, eq, sk, ek, tm):
    n = x2.shape[0]
    row = lambda c: pl.BlockSpec((tm, c), lambda i: (i, 0))
    return pl.pallas_call(
        _in_proj_kernel,
        grid=(n // tm,),
        in_specs=[row(D_MODEL), _const_spec((1, D_MODEL)), _const_spec((D_MODEL, IN_COLS)),
                  _const_spec((1, ATTN_Q)), _const_spec((1, ATTN_KV)),
                  _const_spec((ATTN_Q, LANES)), _const_spec((LANES, ATTN_Q)),
                  _const_spec((ATTN_KV, LANES)), _const_spec((LANES, ATTN_KV))],
        out_specs=[row(ATTN_Q), row(4 * ATTN_KV), row(SSM_WIDTH), row(2 * D_MODEL)],
        out_shape=[jax.ShapeDtypeStruct((n, ATTN_Q), BF16),
                   jax.ShapeDtypeStruct((n, 4 * ATTN_KV), BF16),
                   jax.ShapeDtypeStruct((n, SSM_WIDTH), F32),
                   jax.ShapeDtypeStruct((n, 2 * D_MODEL), BF16)],
        compiler_params=pltpu.CompilerParams(dimension_semantics=("parallel",),
                                             vmem_limit_bytes=VMEM_LIMIT),
        name="in_proj",
    )(x2, ln1, w_in, qg, kg, sq, eq, sk, ek)


_ROLLED_TILE = (0, 1, 1, 0)


def _attn_kernel(sink_ref, q_ref, kvp_ref, kvc_ref, kvn_ref, bias_ref, ones_ref, o_ref):
    kvx = jnp.concatenate([kvp_ref[0], kvc_ref[0], kvn_ref[0]], axis=0)
    low = lax.broadcasted_iota(jnp.int32, (1, LANES), 1) < HEAD_DIM
    zero = jnp.zeros((), BF16)

    def halves(base, kv):
        plain = kvx[:, base + (kv // 2) * LANES:base + (kv // 2 + 1) * LANES]
        r0 = base + ATTN_KV + _ROLLED_TILE[kv] * LANES
        rolled = kvx[:, r0:r0 + LANES]
        lo_src, hi_src = (plain, rolled) if kv % 2 == 0 else (rolled, plain)
        return jnp.where(low, lo_src, zero), jnp.where(low, zero, hi_src)

    top = lax.broadcasted_iota(jnp.int32, (2 * BLOCK, 1), 0) < BLOCK
    outs = []
    for kv in range(N_KV_HEADS):
        ke, ko = halves(0, kv)
        ve, vo = halves(2 * ATTN_KV, kv)
        keys = jnp.concatenate([ke, ko], axis=0)
        vals = jnp.concatenate([jnp.concatenate([ve, vo], axis=0), ones_ref[...]], axis=1)
        q2 = jnp.concatenate([q_ref[0, :, (2 * kv) * LANES:(2 * kv + 1) * LANES],
                              q_ref[0, :, (2 * kv + 1) * LANES:(2 * kv + 2) * LANES]], axis=0)
        s = lax.dot_general(q2, keys, (((1,), (1,)), ((), ())), preferred_element_type=F32)
        s = s + bias_ref[0, kv]
        se, so = s[:, :3 * BLOCK], s[:, 3 * BLOCK:]
        sink_e = jnp.where(top, sink_ref[4 * kv], sink_ref[4 * kv + 2])
        sink_o = jnp.where(top, sink_ref[4 * kv + 1], sink_ref[4 * kv + 3])
        me = jnp.maximum(jnp.max(se, axis=-1, keepdims=True), sink_e)
        mo = jnp.maximum(jnp.max(so, axis=-1, keepdims=True), sink_o)
        p = jnp.concatenate([jnp.exp2(se - me), jnp.exp2(so - mo)], axis=1).astype(BF16)
        od = jnp.dot(p, vals, preferred_element_type=F32)
        den = od[:, LANES:] + jnp.where(low, jnp.exp2(sink_e - me), jnp.exp2(sink_o - mo))
        o = od[:, :LANES] / den
        outs += [o[:BLOCK], o[BLOCK:]]
    o_ref[0] = jnp.concatenate(outs, axis=-1).astype(BF16)


def _attn(q, kvx, bias, ones, sink):
    b, s, _ = q.shape
    nb = s // BLOCK
    assert nb >= 2
    kv_spec = lambda f: pl.BlockSpec((1, BLOCK, 4 * ATTN_KV), f)
    prev = lambda bi, n, *_: (bi, jnp.maximum(n - 1, 0), 0)
    cur = lambda bi, n, *_: (bi, n, 0)
    nxt = lambda bi, n, *_: (bi, jnp.minimum(n + 1, nb - 1), 0)
    edge = lambda bi, n, *_: (jnp.where(n == 0, 0, jnp.where(n == nb - 1, 2, 1)), 0, 0, 0)
    grid_spec = pltpu.PrefetchScalarGridSpec(
        num_scalar_prefetch=1,
        grid=(b, nb),
        in_specs=[pl.BlockSpec((1, BLOCK, ATTN_Q), cur),
                  kv_spec(prev), kv_spec(cur), kv_spec(nxt),
                  pl.BlockSpec((1, N_KV_HEADS, 2 * BLOCK, 6 * BLOCK), edge),
                  pl.BlockSpec((6 * BLOCK, LANES), lambda *_: (0, 0), pipeline_mode=pl.Buffered(1))],
        out_specs=pl.BlockSpec((1, BLOCK, ATTN_Q), cur),
    )
    return pl.pallas_call(
        _attn_kernel,
        grid_spec=grid_spec,
        out_shape=jax.ShapeDtypeStruct((b, s, ATTN_Q), BF16),
        compiler_params=pltpu.CompilerParams(dimension_semantics=("parallel", "arbitrary"),
                                             vmem_limit_bytes=VMEM_LIMIT),
        name="attn",
    )(sink, q, kvx, kvx, kvx, bias, ones)


def _s5_kernel(u_ref, m_ref, enc_ref, dec_ref, coef_ref, dsk_ref, y_ref):
    seq = u_ref.shape[0]
    nch = seq // CHUNK
    levels = int(math.log2(nch))
    parts = [u_ref[pl.ds(s, nch, stride=CHUNK), :] for s in range(CHUNK)]
    ucat = jnp.concatenate([p.astype(BF16) for p in parts], axis=1)
    y = jnp.dot(ucat, m_ref[0], preferred_element_type=F32)
    sloc = jnp.dot(ucat, enc_ref[0], preferred_element_type=F32)

    row = lax.broadcasted_iota(jnp.int32, (nch, 1), 0)

    def shifted(x, sh, forward):
        if forward:
            return jnp.where(row >= sh, pltpu.roll(x, sh, axis=0), 0.0)
        return jnp.where(row < nch - sh, pltpu.roll(x, nch - sh, axis=0), 0.0)

    carried = []
    for d in range(2):
        forward = d == 0
        base = d * 2 * SLAB_STATE
        gr = sloc[:, base:base + SLAB_STATE]
        gi = sloc[:, base + SLAB_STATE:base + 2 * SLAB_STATE]
        for lv in range(levels):
            ar = coef_ref[0, (d * levels + lv) * 2:(d * levels + lv) * 2 + 1, :]
            ai = coef_ref[0, (d * levels + lv) * 2 + 1:(d * levels + lv) * 2 + 2, :]
            sr = shifted(gr, 1 << lv, forward)
            si = shifted(gi, 1 << lv, forward)
            gr, gi = gr + (ar * sr - ai * si), gi + (ar * si + ai * sr)
        carried += [shifted(gr, 1, forward), shifted(gi, 1, forward)]
    xcat = jnp.concatenate(carried, axis=1).astype(BF16)
    y = y + jnp.dot(xcat, dec_ref[0], preferred_element_type=F32)
    dsk = dsk_ref[0]
    for t in range(CHUNK):
        y_ref[pl.ds(t, nch, stride=CHUNK), :] = y[:, t * LANES:(t + 1) * LANES] + dsk * parts[t]


def _s5(u2, m, enc, dec, coef, dsk, seq):
    n = u2.shape[0]
    b = n // seq
    per_slab = lambda a: pl.BlockSpec((1,) + a.shape[1:], lambda sl, bi: (sl,) + (0,) * (a.ndim - 1),
                                      pipeline_mode=pl.Buffered(1))
    io = pl.BlockSpec((seq, LANES), lambda sl, bi: (bi, sl))
    return pl.pallas_call(
        _s5_kernel,
        grid=(SLABS, b),
        in_specs=[io, per_slab(m), per_slab(enc), per_slab(dec), per_slab(coef), per_slab(dsk)],
        out_specs=io,
        out_shape=jax.ShapeDtypeStruct((n, SSM_WIDTH), F32),
        compiler_params=pltpu.CompilerParams(dimension_semantics=("parallel", "parallel"),
                                             vmem_limit_bytes=VMEM_LIMIT),
        name="s5",
    )(u2, m, enc, dec, coef, dsk)


def _cmul(ar, ai, br, bi):
    return ar * br - ai * bi, ar * bi + ai * br


def _dot_nt_f32(a, b):
    return lax.dot_general(a, b, (((1,), (1,)), ((), ())), precision=lax.Precision.HIGHEST,
                           preferred_element_type=F32)


def _s5_gen_kernel(bm_ref, cm_ref, apow_ref, m_ref, enc_ref, dec_ref, lag_ref):
    i = pl.program_id(1)

    def power(d, k):
        return apow_ref[0, d, 0, pl.ds(k, 1), :], apow_ref[0, d, 1, pl.ds(k, 1), :]

    @pl.when(i == 0)
    def _():
        for d in range(2):
            c_cat = jnp.concatenate([cm_ref[0, d, 0], -cm_ref[0, d, 1]], axis=1)
            for k in range(CHUNK):
                e_re, e_im = _cmul(bm_ref[0, d, 0], bm_ref[0, d, 1], *power(d, k))
                lag_ref[d, k] = _dot_nt_f32(jnp.concatenate([e_re, e_im], axis=1), c_cat)

    for d in range(2):
        col = d * 2 * SLAB_STATE
        k_enc = CHUNK - 1 - i if d == 0 else i
        e_re, e_im = _cmul(bm_ref[0, d, 0], bm_ref[0, d, 1], *power(d, k_enc))
        enc_ref[0, :, col:col + SLAB_STATE] = e_re.astype(BF16)
        enc_ref[0, :, col + SLAB_STATE:col + 2 * SLAB_STATE] = e_im.astype(BF16)
        k_dec = i + 1 if d == 0 else CHUNK - i
        f_re, f_im = _cmul(cm_ref[0, d, 0], cm_ref[0, d, 1], *power(d, k_dec))
        dec_ref[0, col:col + SLAB_STATE, :] = f_re.T.astype(BF16)
        dec_ref[0, col + SLAB_STATE:col + 2 * SLAB_STATE, :] = (-f_im).T.astype(BF16)

    both = lag_ref[0, 0] + lag_ref[1, 0]
    for t in range(CHUNK):
        fwd = lag_ref[0, jnp.clip(t - i, 0, CHUNK - 1)]
        bwd = lag_ref[1, jnp.clip(i - t, 0, CHUNK - 1)]
        blk = jnp.where(t > i, fwd, jnp.where(t < i, bwd, both))
        m_ref[0, :, t * LANES:(t + 1) * LANES] = blk.astype(BF16)


def _s5_gen(bm, cm, apow):
    small = lambda a: pl.BlockSpec((1,) + a.shape[1:], lambda sl, i: (sl,) + (0,) * (a.ndim - 1))
    rows = pl.BlockSpec((1, LANES, CHUNK_COLS), lambda sl, i: (sl, i, 0))
    wide = 4 * SLAB_STATE
    return pl.pallas_call(
        _s5_gen_kernel,
        grid=(SLABS, CHUNK),
        in_specs=[small(bm), small(cm), small(apow)],
        out_specs=[rows, pl.BlockSpec((1, LANES, wide), lambda sl, i: (sl, i, 0)),
                   pl.BlockSpec((1, wide, LANES), lambda sl, i: (sl, 0, i))],
        out_shape=[jax.ShapeDtypeStruct((SLABS, CHUNK_COLS, CHUNK_COLS), BF16),
                   jax.ShapeDtypeStruct((SLABS, CHUNK_COLS, wide), BF16),
                   jax.ShapeDtypeStruct((SLABS, wide, CHUNK_COLS), BF16)],
        scratch_shapes=[pltpu.VMEM((2, CHUNK, LANES, LANES), F32)],
        compiler_params=pltpu.CompilerParams(dimension_semantics=("parallel", "arbitrary"),
                                             vmem_limit_bytes=VMEM_LIMIT),
        name="s5_gen",
    )(bm, cm, apow)


def _slab_tiles(t_re, t_im):
    t = jnp.stack([t_re, t_im], 1).astype(F32)
    t = t.reshape(2, 2, SLABS, GROUPS_PER_SLAB, SSM_GROUP, SSM_STATE)
    eye = jnp.eye(GROUPS_PER_SLAB, dtype=F32)
    t = t[:, :, :, :, :, None, :] * eye[:, None, :, None]
    t = t.reshape(2, 2, SLABS, LANES, SLAB_STATE)
    return jnp.transpose(t, (2, 0, 1, 3, 4))


def _s5_weights(lam_re, lam_im, log_dt, b_re, b_im, c_re, c_im, levels):
    dt = jnp.exp(log_dt.astype(F32))[..., None]
    lr, li = lam_re.astype(F32), lam_im.astype(F32)
    mag = jnp.exp(lr * dt)
    ang = li * dt
    a_re, a_im = mag * jnp.cos(ang), mag * jnp.sin(ang)
    den = lr * lr + li * li
    nr, ni = a_re - 1.0, a_im
    z_re = (nr * lr + ni * li) / den
    z_im = (ni * lr - nr * li) / den
    br, bi = b_re.astype(F32), b_im.astype(F32)
    bb_re = z_re[..., None] * br - z_im[..., None] * bi
    bb_im = z_re[..., None] * bi + z_im[..., None] * br

    pr, pi = [jnp.ones_like(a_re)], [jnp.zeros_like(a_im)]
    for _ in range(CHUNK):
        nr_, ni_ = _cmul(pr[-1], pi[-1], a_re, a_im)
        pr.append(nr_)
        pi.append(ni_)
    apow = jnp.stack([jnp.stack(pr, 1), jnp.stack(pi, 1)], 1)
    apow = apow.reshape(2, 2, CHUNK + 1, SLABS, SLAB_STATE)
    apow = jnp.transpose(apow, (3, 0, 1, 2, 4))

    bm = _slab_tiles(jnp.swapaxes(bb_re, -1, -2), jnp.swapaxes(bb_im, -1, -2))
    cm = _slab_tiles(c_re, c_im)
    m, enc, dec = _s5_gen(bm, cm, apow)

    sr_, si_ = pr[CHUNK], pi[CHUNK]
    coefs = []
    for _ in range(levels):
        coefs.append(jnp.stack([sr_, si_], 1))
        sr_, si_ = _cmul(sr_, si_, sr_, si_)
    coef = jnp.stack(coefs, 1)
    coef = coef.reshape(2, levels, 2, SLABS, SLAB_STATE)
    coef = jnp.transpose(coef, (3, 0, 1, 2, 4)).reshape(SLABS, 2 * levels * 2, SLAB_STATE)
    return m, enc, dec, coef


def _post_kernel(x_ref, ya_ref, ys_ref, g_ref, wglu_ref, bglu_ref, wout_ref, ln2_ref,
                 wup_ref, wdown_ref, o_ref):
    y = jax.nn.gelu(ys_ref[...])
    z = jnp.dot(y.astype(BF16), wglu_ref[...], preferred_element_type=F32) + bglu_ref[...]
    y = y * jax.nn.sigmoid(z)
    g = g_ref[...].astype(F32)
    mixed = (jax.nn.sigmoid(g[:, :D_MODEL]) * ya_ref[...].astype(F32)
             + jax.nn.sigmoid(g[:, D_MODEL:]) * y)
    x = x_ref[...] + jnp.dot(mixed.astype(BF16), wout_ref[...], preferred_element_type=F32)
    ms = jnp.mean(x * x, axis=-1, keepdims=True)
    h = (x * lax.rsqrt(ms + EPS) * ln2_ref[...]).astype(BF16)
    up = jnp.dot(h, wup_ref[...], preferred_element_type=F32)
    act = jnp.square(jnp.maximum(up, 0.0)).astype(BF16)
    o_ref[...] = x + jnp.dot(act, wdown_ref[...], preferred_element_type=F32)


def _post(x2, ya, ys, g, w_glu, b_glu, w_out, ln2, w_up, w_down, tm):
    n = x2.shape[0]
    row = lambda c: pl.BlockSpec((tm, c), lambda i: (i, 0))
    return pl.pallas_call(
        _post_kernel,
        grid=(n // tm,),
        in_specs=[row(D_MODEL), row(D_MODEL), row(D_MODEL), row(2 * D_MODEL),
                  _const_spec((D_MODEL, D_MODEL)), _const_spec((1, D_MODEL)),
                  _const_spec((D_MODEL, D_MODEL)), _const_spec((1, D_MODEL)),
                  _const_spec((D_MODEL, D_FF)), _const_spec((D_FF, D_MODEL))],
        out_specs=row(D_MODEL),
        out_shape=jax.ShapeDtypeStruct((n, D_MODEL), F32),
        compiler_params=pltpu.CompilerParams(dimension_semantics=("parallel",),
                                             vmem_limit_bytes=VMEM_LIMIT),
        name="post",
    )(x2, ya, ys, g, w_glu, b_glu, w_out, ln2, w_up, w_down)


def _t5_bucket(rel):
    nb = N_BUCKETS // 2
    ret = (rel > 0).astype(np.int32) * nb
    n = np.abs(rel)
    max_exact = nb // 2
    n_safe = np.maximum(n, 1).astype(np.float32)
    large = max_exact + (np.log(n_safe / max_exact) / math.log(MAX_DISTANCE / max_exact)
                         * (nb - max_exact)).astype(np.int32)
    large = np.minimum(large, nb - 1)
    return (ret + np.where(n < max_exact, n, large)).astype(np.int32)


def _band_bias(rel_table):
    qi = np.arange(BLOCK)[:, None]
    kj = np.arange(3 * BLOCK)[None, :]
    rel = kj - BLOCK - qi
    onehot = (np.arange(N_BUCKETS)[:, None] == _t5_bucket(rel).reshape(1, -1)).astype(np.float32)
    bias = jnp.dot(rel_table.astype(F32).T * LOG2E, jnp.asarray(onehot), precision='highest')
    bias = bias.reshape(N_HEADS, BLOCK, 3 * BLOCK)
    in_band = np.abs(rel) <= WINDOW
    keep = np.stack([in_band & (kj >= BLOCK), in_band, in_band & (kj < 2 * BLOCK)], 0)
    bias = jnp.where(jnp.asarray(keep)[:, None], bias[None], NEG_INF)
    bias = bias.reshape(3, N_KV_HEADS, 2, 2, BLOCK, 3 * BLOCK)
    return jnp.transpose(bias, (0, 1, 2, 4, 3, 5)).reshape(3, N_KV_HEADS, 2 * BLOCK, 6 * BLOCK)


def _sum_mat():
    first = np.arange(6 * BLOCK)[:, None] < 3 * BLOCK
    low = np.arange(LANES)[None, :] < HEAD_DIM
    return jnp.asarray(first == low, BF16)


def _head_mats(width):
    heads = width // HEAD_DIM
    sel = (np.arange(width)[:, None] // HEAD_DIM == np.arange(LANES)[None, :]).astype(np.float32)
    assert heads <= LANES
    return jnp.asarray(sel / HEAD_DIM, BF16), jnp.asarray(sel.T, BF16)


def _layer(x, bias, sink, ln1, w_in, qg, kg, s5w, dsk, w_glu, b_glu, w_out, ln2, w_up, w_down,
           tm_in, tm_post):
    b, s, _ = x.shape
    x2 = x.reshape(b * s, D_MODEL)
    sq, eq = _head_mats(ATTN_Q)
    sk, ek = _head_mats(ATTN_KV)
    q, kvx, u, g = _in_proj(x2, ln1, w_in, qg, kg, sq, eq, sk, ek, tm_in)
    ya = _attn(q.reshape(b, s, ATTN_Q), kvx.reshape(b, s, 4 * ATTN_KV), bias, _sum_mat(), sink)
    ys = _s5(u, *s5w, dsk, s)
    out = _post(x2, ya.reshape(b * s, ATTN_Q), ys, g, w_glu, b_glu, w_out, ln2, w_up, w_down,
                tm_post)
    return out.reshape(b, s, D_MODEL)


def kernel(x_prompt, x_sample, rel_table, ln1, w_in, q_gain, k_gain, sink, lam_re, lam_im, log_dt,
           b_re, b_im, c_re, c_im, d_skip, w_glu, b_glu, w_out, ln2, w_up, w_down):
    assert ln1.shape[0] == 1, "single layer"
    bias = _band_bias(rel_table)
    outs = []
    s5w_by_levels = {}
    for x in (x_prompt, x_sample):
        seq = x.shape[1]
        levels = int(math.log2(seq // CHUNK))
        assert CHUNK << levels == seq and seq % BLOCK == 0
        if levels not in s5w_by_levels:
            s5w_by_levels[levels] = _s5_weights(lam_re[0], lam_im[0], log_dt[0], b_re[0], b_im[0],
                                                c_re[0], c_im[0], levels)
        s5w = s5w_by_levels[levels]
        outs.append(_layer(
            x, bias, sink[0].astype(F32) * LOG2E,
            ln1[0].astype(F32)[None], w_in[0].astype(BF16),
            jnp.tile(q_gain[0].astype(F32), N_HEADS)[None],
            jnp.tile(k_gain[0].astype(F32), N_KV_HEADS)[None],
            s5w, d_skip[0].astype(F32).reshape(SLABS, 1, LANES),
            w_glu[0].astype(BF16), b_glu[0].astype(F32)[None], w_out[0].astype(BF16),
            ln2[0].astype(F32)[None], w_up[0].astype(BF16), w_down[0].astype(BF16),
            tm_in=min(512, x.shape[0] * seq), tm_post=min(256, x.shape[0] * seq)))
    return tuple(outs)
```

```python
import math

import numpy as np
import jax
import jax.numpy as jnp
from jax import lax
from jax.experimental import pallas as pl
from jax.experimental.pallas import tpu as pltpu

D_MODEL = 1024
N_HEADS = 16
N_KV_HEADS = 4
HEAD_DIM = 64
Q_PER_KV = N_HEADS // N_KV_HEADS
WINDOW = 128
BLOCK = 128
N_BUCKETS = 32
MAX_DISTANCE = 128
ATTN_Q = N_HEADS * HEAD_DIM
ATTN_KV = N_KV_HEADS * HEAD_DIM
NEG_INF = -1e30
SSM_WIDTH = D_MODEL
SSM_GROUP = 16
SSM_GROUPS = SSM_WIDTH // SSM_GROUP
SSM_STATE = 64
D_FF = 4 * D_MODEL
EPS = 1e-6
IN_COLS = ATTN_Q + 2 * ATTN_KV + SSM_WIDTH + 2 * D_MODEL

LANES = 128
SUBLANES = 8
SEG_LEVELS = 3
CHUNK = 16
SLABS = SSM_WIDTH // LANES
GROUPS_PER_SLAB = LANES // SSM_GROUP
SLAB_STATE = GROUPS_PER_SLAB * SSM_STATE
VMEM_LIMIT = 56 * 1024 * 1024
TQ_IN = 64
TQ_POST = 32

LOG2E = math.log2(math.e)
Q_SCALE = HEAD_DIM ** -0.5 * LOG2E

BF16 = jnp.bfloat16
F32 = jnp.float32


def _const_spec(shape):
    nd = len(shape)
    return pl.BlockSpec(shape, lambda *_: (0,) * nd, pipeline_mode=pl.Buffered(1))


def _split_dot(r, expand):
    hi = r.astype(BF16)
    lo = (r - hi.astype(F32)).astype(BF16)
    return (jnp.dot(hi, expand, preferred_element_type=F32)
            + jnp.dot(lo, expand, preferred_element_type=F32))


def _in_proj_kernel(x_ref, ln1_ref, w_ref, qg_ref, kg_ref, sq_ref, eq_ref, sk_ref, ek_ref,
                    q_ref, kv_ref, u_ref, g_ref):
    nseg, tq, _ = x_ref.shape
    tm = nseg * tq
    x = x_ref[...].reshape(tm, D_MODEL)
    ms = jnp.mean(x * x, axis=-1, keepdims=True)
    h = (x * lax.rsqrt(ms + EPS) * ln1_ref[...]).astype(BF16)

    def head_norm(t, s_ref, e_ref, gain):
        msq = jnp.dot((t * t).astype(BF16), s_ref[...], preferred_element_type=F32)
        r = lax.rsqrt(msq + EPS)
        return t * _split_dot(r, e_ref[...]) * gain

    c0 = 0
    q = jnp.dot(h, w_ref[:, c0:c0 + ATTN_Q], preferred_element_type=F32)
    q = (head_norm(q, sq_ref, eq_ref, qg_ref[...]) * Q_SCALE).astype(BF16)
    q_ref[...] = q.reshape(nseg, tq, ATTN_Q)
    c0 += ATTN_Q
    k = jnp.dot(h, w_ref[:, c0:c0 + ATTN_KV], preferred_element_type=F32)
    k = head_norm(k, sk_ref, ek_ref, kg_ref[...])
    c0 += ATTN_KV
    v = jnp.dot(h, w_ref[:, c0:c0 + ATTN_KV], preferred_element_type=F32)
    c0 += ATTN_KV
    kvx = jnp.concatenate([k, pltpu.roll(k, HEAD_DIM, axis=1), v, pltpu.roll(v, HEAD_DIM, axis=1)],
                          axis=1).astype(BF16)
    kv_ref[...] = kvx.reshape(nseg, tq, 4 * ATTN_KV)
    u = jnp.dot(h, w_ref[:, c0:c0 + SSM_WIDTH], preferred_element_type=F32)
    for sl in range(SLABS):
        for r in range(nseg):
            u_ref[sl, pl.ds(r, tq, stride=nseg), :] = u[r * tq:(r + 1) * tq, sl * LANES:(sl + 1) * LANES]
    c0 += SSM_WIDTH
    g = jnp.dot(h, w_ref[:, c0:c0 + 2 * D_MODEL], preferred_element_type=F32).astype(BF16)
    g_ref[...] = g.reshape(nseg, tq, 2 * D_MODEL)


def _seg_view(a, b, seq):
    return a.reshape(b, SUBLANES, seq // SUBLANES, a.shape[-1])


def _in_proj(x2, ln1, w_in, qg, kg, sq, eq, sk, ek, b, seq, tq):
    n = x2.shape[0]
    per_seq = seq // SUBLANES // tq
    seg = lambda c: pl.BlockSpec((None, SUBLANES, tq, c), lambda i: (i // per_seq, 0, i % per_seq, 0))
    interleaved = pl.BlockSpec((None, SLABS, tq * SUBLANES, LANES),
                               lambda i: (i // per_seq, 0, i % per_seq, 0))
    seg_shape = lambda c, dt: jax.ShapeDtypeStruct((b, SUBLANES, seq // SUBLANES, c), dt)
    q, kvx, u, g = pl.pallas_call(
        _in_proj_kernel,
        grid=(b * per_seq,),
        in_specs=[seg(D_MODEL), _const_spec((1, D_MODEL)), _const_spec((D_MODEL, IN_COLS)),
                  _const_spec((1, ATTN_Q)), _const_spec((1, ATTN_KV)),
                  _const_spec((ATTN_Q, LANES)), _const_spec((LANES, ATTN_Q)),
                  _const_spec((ATTN_KV, LANES)), _const_spec((LANES, ATTN_KV))],
        out_specs=[seg(ATTN_Q), seg(4 * ATTN_KV), interleaved, seg(2 * D_MODEL)],
        out_shape=[seg_shape(ATTN_Q, BF16), seg_shape(4 * ATTN_KV, BF16),
                   jax.ShapeDtypeStruct((b, SLABS, seq, LANES), F32), seg_shape(2 * D_MODEL, BF16)],
        compiler_params=pltpu.CompilerParams(dimension_semantics=("parallel",),
                                             vmem_limit_bytes=VMEM_LIMIT),
        name="in_proj",
    )(_seg_view(x2, b, seq), ln1, w_in, qg, kg, sq, eq, sk, ek)
    return q.reshape(n, ATTN_Q), kvx.reshape(n, 4 * ATTN_KV), u, g.reshape(n, 2 * D_MODEL)


_ROLLED_TILE = (0, 1, 1, 0)


def _attn_kernel(sink_ref, q_ref, kvp_ref, kvc_ref, kvn_ref, bias_ref, ones_ref, o_ref):
    kvx = jnp.concatenate([kvp_ref[0], kvc_ref[0], kvn_ref[0]], axis=0)
    low = lax.broadcasted_iota(jnp.int32, (1, LANES), 1) < HEAD_DIM
    zero = jnp.zeros((), BF16)

    def halves(base, kv):
        plain = kvx[:, base + (kv // 2) * LANES:base + (kv // 2 + 1) * LANES]
        r0 = base + ATTN_KV + _ROLLED_TILE[kv] * LANES
        rolled = kvx[:, r0:r0 + LANES]
        lo_src, hi_src = (plain, rolled) if kv % 2 == 0 else (rolled, plain)
        return jnp.where(low, lo_src, zero), jnp.where(low, zero, hi_src)

    top = lax.broadcasted_iota(jnp.int32, (2 * BLOCK, 1), 0) < BLOCK
    outs = []
    for kv in range(N_KV_HEADS):
        ke, ko = halves(0, kv)
        ve, vo = halves(2 * ATTN_KV, kv)
        keys = jnp.concatenate([ke, ko], axis=0)
        vals = jnp.concatenate([jnp.concatenate([ve, vo], axis=0), ones_ref[...]], axis=1)
        q2 = jnp.concatenate([q_ref[0, :, (2 * kv) * LANES:(2 * kv + 1) * LANES],
                              q_ref[0, :, (2 * kv + 1) * LANES:(2 * kv + 2) * LANES]], axis=0)
        s = lax.dot_general(q2, keys, (((1,), (1,)), ((), ())), preferred_element_type=F32)
        s = s + bias_ref[0, kv]
        se, so = s[:, :3 * BLOCK], s[:, 3 * BLOCK:]
        sink_e = jnp.where(top, sink_ref[4 * kv], sink_ref[4 * kv + 2])
        sink_o = jnp.where(top, sink_ref[4 * kv + 1], sink_ref[4 * kv + 3])
        me = jnp.maximum(jnp.max(se, axis=-1, keepdims=True), sink_e)
        mo = jnp.maximum(jnp.max(so, axis=-1, keepdims=True), sink_o)
        p = jnp.concatenate([jnp.exp2(se - me), jnp.exp2(so - mo)], axis=1).astype(BF16)
        od = jnp.dot(p, vals, preferred_element_type=F32)
        den = od[:, LANES:] + jnp.where(low, jnp.exp2(sink_e - me), jnp.exp2(sink_o - mo))
        o = od[:, :LANES] / den
        outs += [o[:BLOCK], o[BLOCK:]]
    o_ref[0] = jnp.concatenate(outs, axis=-1).astype(BF16)


def _attn(q, kvx, bias, ones, sink):
    b, s, _ = q.shape
    nb = s // BLOCK
    assert nb >= 2
    kv_spec = lambda f: pl.BlockSpec((1, BLOCK, 4 * ATTN_KV), f)
    prev = lambda bi, n, *_: (bi, jnp.maximum(n - 1, 0), 0)
    cur = lambda bi, n, *_: (bi, n, 0)
    nxt = lambda bi, n, *_: (bi, jnp.minimum(n + 1, nb - 1), 0)
    edge = lambda bi, n, *_: (jnp.where(n == 0, 0, jnp.where(n == nb - 1, 2, 1)), 0, 0, 0)
    grid_spec = pltpu.PrefetchScalarGridSpec(
        num_scalar_prefetch=1,
        grid=(b, nb),
        in_specs=[pl.BlockSpec((1, BLOCK, ATTN_Q), cur),
                  kv_spec(prev), kv_spec(cur), kv_spec(nxt),
                  pl.BlockSpec((1, N_KV_HEADS, 2 * BLOCK, 6 * BLOCK), edge),
                  pl.BlockSpec((6 * BLOCK, LANES), lambda *_: (0, 0), pipeline_mode=pl.Buffered(1))],
        out_specs=pl.BlockSpec((1, BLOCK, ATTN_Q), cur),
    )
    return pl.pallas_call(
        _attn_kernel,
        grid_spec=grid_spec,
        out_shape=jax.ShapeDtypeStruct((b, s, ATTN_Q), BF16),
        compiler_params=pltpu.CompilerParams(dimension_semantics=("parallel", "arbitrary"),
                                             vmem_limit_bytes=VMEM_LIMIT),
        name="attn",
    )(sink, q, kvx, kvx, kvx, bias, ones)


def _s5_kernel(u_ref, mt_ref, enc_ref, dect_ref, coef_ref, dsk_ref, y_ref):
    nq = u_ref.shape[0] // (CHUNK * SUBLANES)
    tile_rows = lambda x, q: x[q * SUBLANES:(q + 1) * SUBLANES]
    tile = lambda q, s: pl.ds((q * CHUNK + s) * SUBLANES, SUBLANES)

    xs = [jnp.concatenate([u_ref[tile(q, s), :] for q in range(nq)], axis=0) for s in range(CHUNK)]
    xst = [x.T for x in xs]

    yts, pair_states = [], []
    for g in range(GROUPS_PER_SLAB):
        ugt = jnp.concatenate([xt[g * SSM_GROUP:(g + 1) * SSM_GROUP] for xt in xst], axis=0)
        ug = ugt.T.astype(BF16)
        yts.append(jnp.dot(mt_ref[0, g], ugt.astype(BF16), preferred_element_type=F32))
        part = jnp.dot(ug, enc_ref[0, g], preferred_element_type=F32)
        if g % 2 == 0:
            pair_states.append(part)
        else:
            pair_states[-1] = pair_states[-1] + part

    def state(d, ri):
        c0 = (2 * d + ri) * LANES
        return jnp.concatenate([p[:, c0:c0 + LANES] for p in pair_states], axis=1)

    def coef(d, row):
        return coef_ref[0, d, 0, row:row + 1, :], coef_ref[0, d, 1, row:row + 1, :]

    seg = lax.broadcasted_iota(jnp.int32, (SUBLANES, 1), 0)

    def seg_shift(x, sh, forward):
        if forward:
            return jnp.where(seg >= sh, pltpu.roll(x, sh, axis=0), 0.0)
        return jnp.where(seg < SUBLANES - sh, pltpu.roll(x, SUBLANES - sh, axis=0), 0.0)

    carried = []
    for d in range(2):
        forward = d == 0
        s_re, s_im = state(d, 0), state(d, 1)
        a_re, a_im = coef(d, 0)
        order = list(range(nq)) if forward else list(range(nq - 1, -1, -1))
        t_re, t_im = {}, {}
        for n, q in enumerate(order):
            cur_re, cur_im = tile_rows(s_re, q), tile_rows(s_im, q)
            if n:
                p_re, p_im = t_re[order[n - 1]], t_im[order[n - 1]]
                cur_re = cur_re + (a_re * p_re - a_im * p_im)
                cur_im = cur_im + (a_re * p_im + a_im * p_re)
            t_re[q], t_im[q] = cur_re, cur_im
        e_re, e_im = t_re[order[-1]], t_im[order[-1]]
        for lv in range(SEG_LEVELS):
            b_re, b_im = coef(d, 1 + lv)
            h_re, h_im = seg_shift(e_re, 1 << lv, forward), seg_shift(e_im, 1 << lv, forward)
            e_re, e_im = e_re + (b_re * h_re - b_im * h_im), e_im + (b_re * h_im + b_im * h_re)
        c_re, c_im = seg_shift(e_re, 1, forward), seg_shift(e_im, 1, forward)
        x_re, x_im = [], []
        for q in range(nq):
            n = order.index(q)
            w_re, w_im = coef(d, 1 + SEG_LEVELS + n)
            r_re, r_im = w_re * c_re - w_im * c_im, w_re * c_im + w_im * c_re
            if n:
                r_re, r_im = r_re + t_re[order[n - 1]], r_im + t_im[order[n - 1]]
            x_re.append(r_re)
            x_im.append(r_im)
        carried += [jnp.concatenate(x_re, axis=0), jnp.concatenate(x_im, axis=0)]

    for g in range(GROUPS_PER_SLAB):
        c0 = (g // 2) * LANES
        xpair = jnp.concatenate([c[:, c0:c0 + LANES] for c in carried], axis=1).astype(BF16)
        yts[g] = yts[g] + lax.dot_general(dect_ref[0, g], xpair, (((1,), (1,)), ((), ())),
                                          preferred_element_type=F32)
    dsk = dsk_ref[0]
    for t in range(CHUNK):
        yt = jnp.concatenate([y[t * SSM_GROUP:(t + 1) * SSM_GROUP] for y in yts], axis=0)
        y = yt.T + dsk * xs[t]
        for q in range(nq):
            y_ref[tile(q, t), :] = tile_rows(y, q)


def _s5(u4, mt, enc, dect, coef, dsk):
    b, _, seq, _ = u4.shape
    per_slab = lambda a: pl.BlockSpec((1,) + a.shape[1:], lambda sl, bi: (sl,) + (0,) * (a.ndim - 1))
    io = pl.BlockSpec((None, None, seq, LANES), lambda sl, bi: (bi, sl, 0, 0))
    return pl.pallas_call(
        _s5_kernel,
        grid=(SLABS, b),
        in_specs=[io, per_slab(mt), per_slab(enc), per_slab(dect), per_slab(coef), per_slab(dsk)],
        out_specs=io,
        out_shape=jax.ShapeDtypeStruct(u4.shape, F32),
        compiler_params=pltpu.CompilerParams(dimension_semantics=("parallel", "parallel"),
                                             vmem_limit_bytes=VMEM_LIMIT),
        name="s5",
    )(u4, mt, enc, dect, coef, dsk)


def _cmul(ar, ai, br, bi):
    return ar * br - ai * bi, ar * bi + ai * br


def _cpowers(a_re, a_im, n):
    p_re, p_im = jnp.ones_like(a_re)[None], jnp.zeros_like(a_im)[None]
    s_re, s_im = a_re, a_im
    while p_re.shape[0] < n:
        h_re, h_im = _cmul(p_re, p_im, s_re[None], s_im[None])
        p_re, p_im = jnp.concatenate([p_re, h_re], 0), jnp.concatenate([p_im, h_im], 0)
        s_re, s_im = _cmul(s_re, s_im, s_re, s_im)
    return p_re[:n], p_im[:n]


def _s5_weights(lam_re, lam_im, log_dt, b_re, b_im, c_re, c_im, nq):
    G, C = SSM_GROUPS, SSM_GROUP
    dt = jnp.exp(log_dt.astype(F32))[..., None]
    lr, li = lam_re.astype(F32), lam_im.astype(F32)
    mag = jnp.exp(lr * dt)
    ang = li * dt
    a_re, a_im = mag * jnp.cos(ang), mag * jnp.sin(ang)
    den = lr * lr + li * li
    nr, ni = a_re - 1.0, a_im
    z_re = (nr * lr + ni * li) / den
    z_im = (ni * lr - nr * li) / den
    br, bi = b_re.astype(F32), b_im.astype(F32)
    bb_re = z_re[..., None] * br - z_im[..., None] * bi
    bb_im = z_re[..., None] * bi + z_im[..., None] * br
    cr = jnp.swapaxes(c_re.astype(F32), -1, -2)
    ci = jnp.swapaxes(c_im.astype(F32), -1, -2)

    pr, pi = [jnp.ones_like(a_re)], [jnp.zeros_like(a_im)]
    for _ in range(CHUNK):
        nr_, ni_ = _cmul(pr[-1], pi[-1], a_re, a_im)
        pr.append(nr_)
        pi.append(ni_)
    pw_re, pw_im = jnp.stack(pr, 0), jnp.stack(pi, 0)

    ab_re, ab_im = _cmul(pw_re[:CHUNK, ..., None], pw_im[:CHUNK, ..., None], bb_re[None], bb_im[None])
    kern = (jnp.einsum('dgpc,kdgpe->kdgce', cr, ab_re, precision='highest')
            - jnp.einsum('dgpc,kdgpe->kdgce', ci, ab_im, precision='highest'))
    lag_tab = jnp.concatenate([kern[::-1, 0][:-1], (kern[0, 0] + kern[0, 1])[None], kern[1:, 1]], 0)
    lag_tab = jnp.transpose(lag_tab, (1, 2, 0, 3)).reshape(G, C, (2 * CHUNK - 1) * C)
    mt = jnp.stack([lag_tab[:, :, (CHUNK - 1 - t) * C:(CHUNK - 1 - t) * C + CHUNK * C]
                    for t in range(CHUNK)], 1).reshape(SLABS, GROUPS_PER_SLAB, CHUNK * C, CHUNK * C)

    half = jnp.asarray(np.arange(G)[:, None] % 2 == np.arange(2)[None, :], F32)
    d_ix = np.arange(2)[:, None]
    k_enc = np.stack([CHUNK - 1 - np.arange(CHUNK), np.arange(CHUNK)], 0)
    e_re, e_im = _cmul(pw_re[k_enc, d_ix][..., None], pw_im[k_enc, d_ix][..., None],
                       bb_re[:, None], bb_im[:, None])
    e = jnp.stack([e_re, e_im], 2)
    e = jnp.transpose(e, (3, 1, 5, 0, 2, 4))
    enc = (e[..., None, :] * half[:, None, None, None, None, :, None])
    enc = enc.reshape(SLABS, GROUPS_PER_SLAB, CHUNK * C, 4 * LANES)
    k_dec = np.stack([np.arange(CHUNK) + 1, CHUNK - np.arange(CHUNK)], 0)
    f_re, f_im = _cmul(pw_re[k_dec, d_ix][..., None], pw_im[k_dec, d_ix][..., None],
                       cr[:, None], ci[:, None])
    f = jnp.stack([f_re, -f_im], 2)
    f = jnp.transpose(f, (3, 1, 5, 0, 2, 4))
    dect = (f[..., None, :] * half[:, None, None, None, None, :, None])
    dect = dect.reshape(SLABS, GROUPS_PER_SLAB, CHUNK * C, 4 * LANES)

    s_re, s_im = pr[CHUNK], pi[CHUNK]
    w_re, w_im = _cpowers(s_re, s_im, nq)
    q_re, q_im = _cmul(w_re[nq - 1], w_im[nq - 1], s_re, s_im)
    rows_re, rows_im = [s_re], [s_im]
    for _ in range(SEG_LEVELS):
        rows_re.append(q_re)
        rows_im.append(q_im)
        q_re, q_im = _cmul(q_re, q_im, q_re, q_im)
    c_all = jnp.stack([jnp.concatenate([jnp.stack(rows_re, 0), w_re], 0),
                       jnp.concatenate([jnp.stack(rows_im, 0), w_im], 0)], 0)
    n_rows = c_all.shape[1]
    c_all = c_all.reshape(2, n_rows, 2, SLABS, SLAB_STATE)
    coef = jnp.transpose(c_all, (3, 2, 0, 1, 4))
    return mt.astype(BF16), enc.astype(BF16), dect.astype(BF16), coef


def _post_kernel(x_ref, ya_ref, ys_ref, g_ref, wglu_ref, bglu_ref, wout_ref, ln2_ref,
                 wup_ref, wdown_ref, o_ref):
    nseg, tq, _ = x_ref.shape
    tm = nseg * tq
    ys = jnp.concatenate(
        [jnp.concatenate([ys_ref[sl, pl.ds(r, tq, stride=nseg), :] for sl in range(SLABS)], axis=1)
         for r in range(nseg)], axis=0)
    y = jax.nn.gelu(ys)
    z = jnp.dot(y.astype(BF16), wglu_ref[...], preferred_element_type=F32) + bglu_ref[...]
    y = y * jax.nn.sigmoid(z)
    g = g_ref[...].reshape(tm, 2 * D_MODEL).astype(F32)
    ya = ya_ref[...].reshape(tm, D_MODEL).astype(F32)
    mixed = jax.nn.sigmoid(g[:, :D_MODEL]) * ya + jax.nn.sigmoid(g[:, D_MODEL:]) * y
    x = x_ref[...].reshape(tm, D_MODEL)
    x = x + jnp.dot(mixed.astype(BF16), wout_ref[...], preferred_element_type=F32)
    ms = jnp.mean(x * x, axis=-1, keepdims=True)
    h = (x * lax.rsqrt(ms + EPS) * ln2_ref[...]).astype(BF16)
    up = jnp.dot(h, wup_ref[...], preferred_element_type=F32)
    act = jnp.square(jnp.maximum(up, 0.0)).astype(BF16)
    out = x + jnp.dot(act, wdown_ref[...], preferred_element_type=F32)
    o_ref[...] = out.reshape(nseg, tq, D_MODEL)


def _post(x2, ya, ys4, g, w_glu, b_glu, w_out, ln2, w_up, w_down, tq):
    b, _, seq, _ = ys4.shape
    per_seq = seq // SUBLANES // tq
    seg = lambda c: pl.BlockSpec((None, SUBLANES, tq, c), lambda i: (i // per_seq, 0, i % per_seq, 0))
    interleaved = pl.BlockSpec((None, SLABS, tq * SUBLANES, LANES),
                               lambda i: (i // per_seq, 0, i % per_seq, 0))
    out = pl.pallas_call(
        _post_kernel,
        grid=(b * per_seq,),
        in_specs=[seg(D_MODEL), seg(D_MODEL), interleaved, seg(2 * D_MODEL),
                  _const_spec((D_MODEL, D_MODEL)), _const_spec((1, D_MODEL)),
                  _const_spec((D_MODEL, D_MODEL)), _const_spec((1, D_MODEL)),
                  _const_spec((D_MODEL, D_FF)), _const_spec((D_FF, D_MODEL))],
        out_specs=seg(D_MODEL),
        out_shape=jax.ShapeDtypeStruct((b, SUBLANES, seq // SUBLANES, D_MODEL), F32),
        compiler_params=pltpu.CompilerParams(dimension_semantics=("parallel",),
                                             vmem_limit_bytes=VMEM_LIMIT),
        name="post",
    )(_seg_view(x2, b, seq), _seg_view(ya, b, seq), ys4, _seg_view(g, b, seq),
      w_glu, b_glu, w_out, ln2, w_up, w_down)
    return out.reshape(b, seq, D_MODEL)


def _t5_bucket(rel):
    nb = N_BUCKETS // 2
    ret = (rel > 0).astype(np.int32) * nb
    n = np.abs(rel)
    max_exact = nb // 2
    n_safe = np.maximum(n, 1).astype(np.float32)
    large = max_exact + (np.log(n_safe / max_exact) / math.log(MAX_DISTANCE / max_exact)
                         * (nb - max_exact)).astype(np.int32)
    large = np.minimum(large, nb - 1)
    return (ret + np.where(n < max_exact, n, large)).astype(np.int32)


def _band_bias(rel_table):
    qi = np.arange(BLOCK)[:, None]
    kj = np.arange(3 * BLOCK)[None, :]
    rel = kj - BLOCK - qi
    onehot = (np.arange(N_BUCKETS)[:, None] == _t5_bucket(rel).reshape(1, -1)).astype(np.float32)
    bias = jnp.dot(rel_table.astype(F32).T * LOG2E, jnp.asarray(onehot), precision='highest')
    bias = bias.reshape(N_HEADS, BLOCK, 3 * BLOCK)
    in_band = np.abs(rel) <= WINDOW
    keep = np.stack([in_band & (kj >= BLOCK), in_band, in_band & (kj < 2 * BLOCK)], 0)
    bias = jnp.where(jnp.asarray(keep)[:, None], bias[None], NEG_INF)
    bias = bias.reshape(3, N_KV_HEADS, 2, 2, BLOCK, 3 * BLOCK)
    return jnp.transpose(bias, (0, 1, 2, 4, 3, 5)).reshape(3, N_KV_HEADS, 2 * BLOCK, 6 * BLOCK)


def _sum_mat():
    first = np.arange(6 * BLOCK)[:, None] < 3 * BLOCK
    low = np.arange(LANES)[None, :] < HEAD_DIM
    return jnp.asarray(first == low, BF16)


def _head_mats(width):
    heads = width // HEAD_DIM
    sel = (np.arange(width)[:, None] // HEAD_DIM == np.arange(LANES)[None, :]).astype(np.float32)
    assert heads <= LANES
    return jnp.asarray(sel / HEAD_DIM, BF16), jnp.asarray(sel.T, BF16)


def _layer(x, bias, sink, ln1, w_in, qg, kg, s5w, dsk, w_glu, b_glu, w_out, ln2, w_up, w_down):
    b, s, _ = x.shape
    x2 = x.reshape(b * s, D_MODEL)
    sq, eq = _head_mats(ATTN_Q)
    sk, ek = _head_mats(ATTN_KV)
    rows_per_seg = s // SUBLANES
    q, kvx, u4, g = _in_proj(x2, ln1, w_in, qg, kg, sq, eq, sk, ek, b, s, min(TQ_IN, rows_per_seg))
    ya = _attn(q.reshape(b, s, ATTN_Q), kvx.reshape(b, s, 4 * ATTN_KV), bias, _sum_mat(), sink)
    ys4 = _s5(u4, *s5w, dsk)
    return _post(x2, ya.reshape(b * s, ATTN_Q), ys4, g, w_glu, b_glu, w_out, ln2, w_up, w_down,
                 min(TQ_POST, rows_per_seg))


def kernel(x_prompt, x_sample, rel_table, ln1, w_in, q_gain, k_gain, sink, lam_re, lam_im, log_dt,
           b_re, b_im, c_re, c_im, d_skip, w_glu, b_glu, w_out, ln2, w_up, w_down):
    assert ln1.shape[0] == 1, "single layer"
    bias = _band_bias(rel_table)
    outs = []
    s5w_by_nq = {}
    for x in (x_prompt, x_sample):
        seq = x.shape[1]
        nq = seq // (CHUNK * SUBLANES)
        assert nq * CHUNK * SUBLANES == seq and seq % BLOCK == 0
        if nq not in s5w_by_nq:
            s5w_by_nq[nq] = _s5_weights(lam_re[0], lam_im[0], log_dt[0], b_re[0], b_im[0],
                                        c_re[0], c_im[0], nq)
        s5w = s5w_by_nq[nq]
        outs.append(_layer(
            x, bias, sink[0].astype(F32) * LOG2E,
            ln1[0].astype(F32)[None], w_in[0].astype(BF16),
            jnp.tile(q_gain[0].astype(F32), N_HEADS)[None],
            jnp.tile(k_gain[0].astype(F32), N_KV_HEADS)[None],
            s5w, d_skip[0].astype(F32).reshape(SLABS, 1, LANES),
            w_glu[0].astype(BF16), b_glu[0].astype(F32)[None], w_out[0].astype(BF16),
            ln2[0].astype(F32)[None], w_up[0].astype(BF16), w_down[0].astype(BF16)))
    return tuple(outs)
```

```python
import math

import numpy as np
import jax
import jax.numpy as jnp
from jax import lax
from jax.experimental import pallas as pl
from jax.experimental.pallas import tpu as pltpu

D_MODEL = 1024
N_HEADS = 16
N_KV_HEADS = 4
HEAD_DIM = 64
Q_PER_KV = N_HEADS // N_KV_HEADS
WINDOW = 128
BLOCK = 128
N_BUCKETS = 32
MAX_DISTANCE = 128
ATTN_Q = N_HEADS * HEAD_DIM
ATTN_KV = N_KV_HEADS * HEAD_DIM
NEG_INF = -1e30
SSM_WIDTH = D_MODEL
SSM_GROUP = 16
SSM_GROUPS = SSM_WIDTH // SSM_GROUP
SSM_STATE = 64
D_FF = 4 * D_MODEL
EPS = 1e-6
IN_COLS = ATTN_Q + 2 * ATTN_KV + SSM_WIDTH + 2 * D_MODEL

LANES = 128
SUBLANES = 8
SEG_LEVELS = 3
CHUNK = 16
SLABS = SSM_WIDTH // LANES
GROUPS_PER_SLAB = LANES // SSM_GROUP
SLAB_STATE = GROUPS_PER_SLAB * SSM_STATE
VMEM_LIMIT = 56 * 1024 * 1024
TQ_IN = 64
TQ_POST = 64

LOG2E = math.log2(math.e)
Q_SCALE = HEAD_DIM ** -0.5 * LOG2E

BF16 = jnp.bfloat16
F32 = jnp.float32


def _const_spec(shape):
    nd = len(shape)
    return pl.BlockSpec(shape, lambda *_: (0,) * nd, pipeline_mode=pl.Buffered(1))


def _split_dot(r, expand):
    hi = r.astype(BF16)
    lo = (r - hi.astype(F32)).astype(BF16)
    return (jnp.dot(hi, expand, preferred_element_type=F32)
            + jnp.dot(lo, expand, preferred_element_type=F32))


def _in_proj_kernel(x_ref, ln1_ref, w_ref, qg_ref, kg_ref, sq_ref, eq_ref, sk_ref, ek_ref,
                    q_ref, kv_ref, u_ref, g_ref):
    nseg, tq, _ = x_ref.shape
    tm = nseg * tq
    x = x_ref[...].reshape(tm, D_MODEL)
    ms = jnp.mean(x * x, axis=-1, keepdims=True)
    h = (x * lax.rsqrt(ms + EPS) * ln1_ref[...]).astype(BF16)

    def head_norm(t, s_ref, e_ref, gain):
        msq = jnp.dot((t * t).astype(BF16), s_ref[...], preferred_element_type=F32)
        r = lax.rsqrt(msq + EPS)
        return t * _split_dot(r, e_ref[...]) * gain

    c0 = 0
    q = jnp.dot(h, w_ref[:, c0:c0 + ATTN_Q], preferred_element_type=F32)
    q = (head_norm(q, sq_ref, eq_ref, qg_ref[...]) * Q_SCALE).astype(BF16)
    q_ref[...] = q.reshape(nseg, tq, ATTN_Q)
    c0 += ATTN_Q
    k = jnp.dot(h, w_ref[:, c0:c0 + ATTN_KV], preferred_element_type=F32)
    k = head_norm(k, sk_ref, ek_ref, kg_ref[...])
    c0 += ATTN_KV
    v = jnp.dot(h, w_ref[:, c0:c0 + ATTN_KV], preferred_element_type=F32)
    c0 += ATTN_KV
    kvx = jnp.concatenate([k, pltpu.roll(k, HEAD_DIM, axis=1), v, pltpu.roll(v, HEAD_DIM, axis=1)],
                          axis=1).astype(BF16)
    kv_ref[...] = kvx.reshape(nseg, tq, 4 * ATTN_KV)
    u = jnp.dot(h, w_ref[:, c0:c0 + SSM_WIDTH], preferred_element_type=F32)
    for sl in range(SLABS):
        for r in range(nseg):
            u_ref[sl, pl.ds(r, tq, stride=nseg), :] = u[r * tq:(r + 1) * tq, sl * LANES:(sl + 1) * LANES]
    c0 += SSM_WIDTH
    g = jnp.dot(h, w_ref[:, c0:c0 + 2 * D_MODEL], preferred_element_type=F32).astype(BF16)
    g_ref[...] = g.reshape(nseg, tq, 2 * D_MODEL)


def _seg_view(a, b, seq):
    return a.reshape(b, SUBLANES, seq // SUBLANES, a.shape[-1])


def _in_proj(x2, ln1, w_in, qg, kg, sq, eq, sk, ek, b, seq, tq):
    n = x2.shape[0]
    per_seq = seq // SUBLANES // tq
    seg = lambda c: pl.BlockSpec((None, SUBLANES, tq, c), lambda i: (i // per_seq, 0, i % per_seq, 0))
    interleaved = pl.BlockSpec((None, SLABS, tq * SUBLANES, LANES),
                               lambda i: (i // per_seq, 0, i % per_seq, 0))
    seg_shape = lambda c, dt: jax.ShapeDtypeStruct((b, SUBLANES, seq // SUBLANES, c), dt)
    q, kvx, u, g = pl.pallas_call(
        _in_proj_kernel,
        grid=(b * per_seq,),
        in_specs=[seg(D_MODEL), _const_spec((1, D_MODEL)), _const_spec((D_MODEL, IN_COLS)),
                  _const_spec((1, ATTN_Q)), _const_spec((1, ATTN_KV)),
                  _const_spec((ATTN_Q, LANES)), _const_spec((LANES, ATTN_Q)),
                  _const_spec((ATTN_KV, LANES)), _const_spec((LANES, ATTN_KV))],
        out_specs=[seg(ATTN_Q), seg(4 * ATTN_KV), interleaved, seg(2 * D_MODEL)],
        out_shape=[seg_shape(ATTN_Q, BF16), seg_shape(4 * ATTN_KV, BF16),
                   jax.ShapeDtypeStruct((b, SLABS, seq, LANES), F32), seg_shape(2 * D_MODEL, BF16)],
        compiler_params=pltpu.CompilerParams(dimension_semantics=("parallel",),
                                             vmem_limit_bytes=VMEM_LIMIT),
        name="in_proj",
    )(_seg_view(x2, b, seq), ln1, w_in, qg, kg, sq, eq, sk, ek)
    return q.reshape(n, ATTN_Q), kvx.reshape(n, 4 * ATTN_KV), u, g.reshape(n, 2 * D_MODEL)


_ROLLED_TILE = (0, 1, 1, 0)


def _attn_kernel(sink_ref, q_ref, kvp_ref, kvc_ref, kvn_ref, bias_ref, ones_ref, o_ref):
    kvx = jnp.concatenate([kvp_ref[0], kvc_ref[0], kvn_ref[0]], axis=0)
    low = lax.broadcasted_iota(jnp.int32, (1, LANES), 1) < HEAD_DIM
    zero = jnp.zeros((), BF16)

    def halves(base, kv):
        plain = kvx[:, base + (kv // 2) * LANES:base + (kv // 2 + 1) * LANES]
        r0 = base + ATTN_KV + _ROLLED_TILE[kv] * LANES
        rolled = kvx[:, r0:r0 + LANES]
        lo_src, hi_src = (plain, rolled) if kv % 2 == 0 else (rolled, plain)
        return jnp.where(low, lo_src, zero), jnp.where(low, zero, hi_src)

    top = lax.broadcasted_iota(jnp.int32, (2 * BLOCK, 1), 0) < BLOCK
    outs = []
    for kv in range(N_KV_HEADS):
        ke, ko = halves(0, kv)
        ve, vo = halves(2 * ATTN_KV, kv)
        keys = jnp.concatenate([ke, ko], axis=0)
        vals = jnp.concatenate([jnp.concatenate([ve, vo], axis=0), ones_ref[...]], axis=1)
        q2 = jnp.concatenate([q_ref[0, :, (2 * kv) * LANES:(2 * kv + 1) * LANES],
                              q_ref[0, :, (2 * kv + 1) * LANES:(2 * kv + 2) * LANES]], axis=0)
        s = lax.dot_general(q2, keys, (((1,), (1,)), ((), ())), preferred_element_type=F32)
        s = s + bias_ref[0, kv]
        se, so = s[:, :3 * BLOCK], s[:, 3 * BLOCK:]
        sink_e = jnp.where(top, sink_ref[4 * kv], sink_ref[4 * kv + 2])
        sink_o = jnp.where(top, sink_ref[4 * kv + 1], sink_ref[4 * kv + 3])
        me = jnp.maximum(jnp.max(se, axis=-1, keepdims=True), sink_e)
        mo = jnp.maximum(jnp.max(so, axis=-1, keepdims=True), sink_o)
        p = jnp.concatenate([jnp.exp2(se - me), jnp.exp2(so - mo)], axis=1).astype(BF16)
        od = jnp.dot(p, vals, preferred_element_type=F32)
        den = od[:, LANES:] + jnp.where(low, jnp.exp2(sink_e - me), jnp.exp2(sink_o - mo))
        o = od[:, :LANES] / den
        outs += [o[:BLOCK], o[BLOCK:]]
    o_ref[0] = jnp.concatenate(outs, axis=-1).astype(BF16)


def _attn(q, kvx, bias, ones, sink):
    b, s, _ = q.shape
    nb = s // BLOCK
    assert nb >= 2
    kv_spec = lambda f: pl.BlockSpec((1, BLOCK, 4 * ATTN_KV), f)
    prev = lambda bi, n, *_: (bi, jnp.maximum(n - 1, 0), 0)
    cur = lambda bi, n, *_: (bi, n, 0)
    nxt = lambda bi, n, *_: (bi, jnp.minimum(n + 1, nb - 1), 0)
    edge = lambda bi, n, *_: (jnp.where(n == 0, 0, jnp.where(n == nb - 1, 2, 1)), 0, 0, 0)
    grid_spec = pltpu.PrefetchScalarGridSpec(
        num_scalar_prefetch=1,
        grid=(b, nb),
        in_specs=[pl.BlockSpec((1, BLOCK, ATTN_Q), cur),
                  kv_spec(prev), kv_spec(cur), kv_spec(nxt),
                  pl.BlockSpec((1, N_KV_HEADS, 2 * BLOCK, 6 * BLOCK), edge),
                  pl.BlockSpec((6 * BLOCK, LANES), lambda *_: (0, 0), pipeline_mode=pl.Buffered(1))],
        out_specs=pl.BlockSpec((1, BLOCK, ATTN_Q), cur),
    )
    return pl.pallas_call(
        _attn_kernel,
        grid_spec=grid_spec,
        out_shape=jax.ShapeDtypeStruct((b, s, ATTN_Q), BF16),
        compiler_params=pltpu.CompilerParams(dimension_semantics=("parallel", "arbitrary"),
                                             vmem_limit_bytes=VMEM_LIMIT),
        name="attn",
    )(sink, q, kvx, kvx, kvx, bias, ones)


def _s5_kernel(u_ref, mt_ref, enc_ref, dect_ref, coef_ref, dsk_ref, y_ref):
    nq = u_ref.shape[0] // (CHUNK * SUBLANES)
    tile_rows = lambda x, q: x[q * SUBLANES:(q + 1) * SUBLANES]
    tile = lambda q, s: pl.ds((q * CHUNK + s) * SUBLANES, SUBLANES)

    xs = [jnp.concatenate([u_ref[tile(q, s), :] for q in range(nq)], axis=0) for s in range(CHUNK)]
    xst = [x.T for x in xs]

    yts, pair_states = [], []
    for g in range(GROUPS_PER_SLAB):
        ugt = jnp.concatenate([xt[g * SSM_GROUP:(g + 1) * SSM_GROUP] for xt in xst], axis=0)
        ug = ugt.T.astype(BF16)
        yts.append(jnp.dot(mt_ref[0, g], ugt.astype(BF16), preferred_element_type=F32))
        part = jnp.dot(ug, enc_ref[0, g], preferred_element_type=F32)
        if g % 2 == 0:
            pair_states.append(part)
        else:
            pair_states[-1] = pair_states[-1] + part

    def state(d, ri):
        c0 = (2 * d + ri) * LANES
        return jnp.concatenate([p[:, c0:c0 + LANES] for p in pair_states], axis=1)

    def coef(d, row):
        return coef_ref[0, d, 0, row:row + 1, :], coef_ref[0, d, 1, row:row + 1, :]

    seg = lax.broadcasted_iota(jnp.int32, (SUBLANES, 1), 0)

    def seg_shift(x, sh, forward):
        if forward:
            return jnp.where(seg >= sh, pltpu.roll(x, sh, axis=0), 0.0)
        return jnp.where(seg < SUBLANES - sh, pltpu.roll(x, SUBLANES - sh, axis=0), 0.0)

    carried = []
    for d in range(2):
        forward = d == 0
        s_re, s_im = state(d, 0), state(d, 1)
        a_re, a_im = coef(d, 0)
        order = list(range(nq)) if forward else list(range(nq - 1, -1, -1))
        t_re, t_im = {}, {}
        for n, q in enumerate(order):
            cur_re, cur_im = tile_rows(s_re, q), tile_rows(s_im, q)
            if n:
                p_re, p_im = t_re[order[n - 1]], t_im[order[n - 1]]
                cur_re = cur_re + (a_re * p_re - a_im * p_im)
                cur_im = cur_im + (a_re * p_im + a_im * p_re)
            t_re[q], t_im[q] = cur_re, cur_im
        e_re, e_im = t_re[order[-1]], t_im[order[-1]]
        for lv in range(SEG_LEVELS):
            b_re, b_im = coef(d, 1 + lv)
            h_re, h_im = seg_shift(e_re, 1 << lv, forward), seg_shift(e_im, 1 << lv, forward)
            e_re, e_im = e_re + (b_re * h_re - b_im * h_im), e_im + (b_re * h_im + b_im * h_re)
        c_re, c_im = seg_shift(e_re, 1, forward), seg_shift(e_im, 1, forward)
        x_re, x_im = [], []
        for q in range(nq):
            n = order.index(q)
            w_re, w_im = coef(d, 1 + SEG_LEVELS + n)
            r_re, r_im = w_re * c_re - w_im * c_im, w_re * c_im + w_im * c_re
            if n:
                r_re, r_im = r_re + t_re[order[n - 1]], r_im + t_im[order[n - 1]]
            x_re.append(r_re)
            x_im.append(r_im)
        carried += [jnp.concatenate(x_re, axis=0), jnp.concatenate(x_im, axis=0)]

    for g in range(GROUPS_PER_SLAB):
        c0 = (g // 2) * LANES
        xpair = jnp.concatenate([c[:, c0:c0 + LANES] for c in carried], axis=1).astype(BF16)
        yts[g] = yts[g] + lax.dot_general(dect_ref[0, g], xpair, (((1,), (1,)), ((), ())),
                                          preferred_element_type=F32)
    dsk = dsk_ref[0]
    for t in range(CHUNK):
        yt = jnp.concatenate([y[t * SSM_GROUP:(t + 1) * SSM_GROUP] for y in yts], axis=0)
        y = yt.T + dsk * xs[t]
        for q in range(nq):
            y_ref[tile(q, t), :] = tile_rows(y, q)


def _s5(u4, mt, enc, dect, coef, dsk):
    b, _, seq, _ = u4.shape
    per_slab = lambda a: pl.BlockSpec((1,) + a.shape[1:], lambda sl, bi: (sl,) + (0,) * (a.ndim - 1))
    io = pl.BlockSpec((None, None, seq, LANES), lambda sl, bi: (bi, sl, 0, 0))
    return pl.pallas_call(
        _s5_kernel,
        grid=(SLABS, b),
        in_specs=[io, per_slab(mt), per_slab(enc), per_slab(dect), per_slab(coef), per_slab(dsk)],
        out_specs=io,
        out_shape=jax.ShapeDtypeStruct(u4.shape, F32),
        compiler_params=pltpu.CompilerParams(dimension_semantics=("parallel", "parallel"),
                                             vmem_limit_bytes=VMEM_LIMIT),
        name="s5",
    )(u4, mt, enc, dect, coef, dsk)


def _cmul(ar, ai, br, bi):
    return ar * br - ai * bi, ar * bi + ai * br


def _cpowers(a_re, a_im, n):
    p_re, p_im = jnp.ones_like(a_re)[None], jnp.zeros_like(a_im)[None]
    s_re, s_im = a_re, a_im
    while p_re.shape[0] < n:
        h_re, h_im = _cmul(p_re, p_im, s_re[None], s_im[None])
        p_re, p_im = jnp.concatenate([p_re, h_re], 0), jnp.concatenate([p_im, h_im], 0)
        s_re, s_im = _cmul(s_re, s_im, s_re, s_im)
    return p_re[:n], p_im[:n]


def _s5_gen_kernel(bm_ref, cm_ref, apow_ref, enc_ref, dect_ref):
    def put(ref, d, ri, step, tile):
        tile = tile.astype(BF16)
        c0 = (2 * d + ri) * LANES
        for g in range(GROUPS_PER_SLAB):
            p0 = (g // 2) * LANES
            ref[0, g, step * SSM_GROUP:(step + 1) * SSM_GROUP, c0:c0 + LANES] = (
                tile[g * SSM_GROUP:(g + 1) * SSM_GROUP, p0:p0 + LANES])

    for d in range(2):
        for k in range(CHUNK + 1):
            a_re, a_im = apow_ref[0, d, 0, k:k + 1, :], apow_ref[0, d, 1, k:k + 1, :]
            if k < CHUNK:
                s = CHUNK - 1 - k if d == 0 else k
                e_re, e_im = _cmul(bm_ref[0, d, 0], bm_ref[0, d, 1], a_re, a_im)
                put(enc_ref, d, 0, s, e_re)
                put(enc_ref, d, 1, s, e_im)
            if k >= 1:
                t = k - 1 if d == 0 else CHUNK - k
                f_re, f_im = _cmul(cm_ref[0, d, 0], cm_ref[0, d, 1], a_re, a_im)
                put(dect_ref, d, 0, t, f_re)
                put(dect_ref, d, 1, t, -f_im)


def _s5_gen(bm, cm, apow):
    small = lambda a: pl.BlockSpec((1,) + a.shape[1:], lambda sl: (sl,) + (0,) * (a.ndim - 1))
    shape = (SLABS, GROUPS_PER_SLAB, CHUNK * SSM_GROUP, 4 * LANES)
    out = pl.BlockSpec((1,) + shape[1:], lambda sl: (sl, 0, 0, 0))
    return pl.pallas_call(
        _s5_gen_kernel,
        grid=(SLABS,),
        in_specs=[small(bm), small(cm), small(apow)],
        out_specs=[out, out],
        out_shape=[jax.ShapeDtypeStruct(shape, BF16), jax.ShapeDtypeStruct(shape, BF16)],
        compiler_params=pltpu.CompilerParams(dimension_semantics=("parallel",),
                                             vmem_limit_bytes=VMEM_LIMIT),
        name="s5_gen",
    )(bm, cm, apow)


def _slab_tiles(t_re, t_im):
    t = jnp.stack([t_re, t_im], 1).astype(F32)
    t = t.reshape(2, 2, SLABS, GROUPS_PER_SLAB, SSM_GROUP, SSM_STATE)
    eye = jnp.eye(GROUPS_PER_SLAB, dtype=F32)
    t = t[:, :, :, :, :, None, :] * eye[:, None, :, None]
    t = t.reshape(2, 2, SLABS, LANES, SLAB_STATE)
    return jnp.transpose(t, (2, 0, 1, 3, 4))


def _s5_weights(lam_re, lam_im, log_dt, b_re, b_im, c_re, c_im, nq):
    G, C = SSM_GROUPS, SSM_GROUP
    dt = jnp.exp(log_dt.astype(F32))[..., None]
    lr, li = lam_re.astype(F32), lam_im.astype(F32)
    mag = jnp.exp(lr * dt)
    ang = li * dt
    a_re, a_im = mag * jnp.cos(ang), mag * jnp.sin(ang)
    den = lr * lr + li * li
    nr, ni = a_re - 1.0, a_im
    z_re = (nr * lr + ni * li) / den
    z_im = (ni * lr - nr * li) / den
    br, bi = b_re.astype(F32), b_im.astype(F32)
    bb_re = z_re[..., None] * br - z_im[..., None] * bi
    bb_im = z_re[..., None] * bi + z_im[..., None] * br
    cr = jnp.swapaxes(c_re.astype(F32), -1, -2)
    ci = jnp.swapaxes(c_im.astype(F32), -1, -2)

    pr, pi = [jnp.ones_like(a_re)], [jnp.zeros_like(a_im)]
    for _ in range(CHUNK):
        nr_, ni_ = _cmul(pr[-1], pi[-1], a_re, a_im)
        pr.append(nr_)
        pi.append(ni_)
    pw_re, pw_im = jnp.stack(pr, 0), jnp.stack(pi, 0)

    ab_re, ab_im = _cmul(pw_re[:CHUNK, ..., None], pw_im[:CHUNK, ..., None], bb_re[None], bb_im[None])
    kern = (jnp.einsum('dgpc,kdgpe->kdgce', cr, ab_re, precision='highest')
            - jnp.einsum('dgpc,kdgpe->kdgce', ci, ab_im, precision='highest'))
    lag_tab = jnp.concatenate([kern[::-1, 0][:-1], (kern[0, 0] + kern[0, 1])[None], kern[1:, 1]], 0)
    lag_tab = jnp.transpose(lag_tab, (1, 2, 0, 3)).reshape(G, C, (2 * CHUNK - 1) * C)
    mt = jnp.stack([lag_tab[:, :, (CHUNK - 1 - t) * C:(CHUNK - 1 - t) * C + CHUNK * C]
                    for t in range(CHUNK)], 1).reshape(SLABS, GROUPS_PER_SLAB, CHUNK * C, CHUNK * C)

    apow = jnp.stack([pw_re, pw_im], 0).reshape(2, CHUNK + 1, 2, SLABS, SLAB_STATE)
    apow = jnp.transpose(apow, (3, 2, 0, 1, 4))
    enc, dect = _s5_gen(_slab_tiles(jnp.swapaxes(bb_re, -1, -2), jnp.swapaxes(bb_im, -1, -2)),
                        _slab_tiles(c_re, c_im), apow)

    s_re, s_im = pr[CHUNK], pi[CHUNK]
    w_re, w_im = _cpowers(s_re, s_im, nq)
    q_re, q_im = _cmul(w_re[nq - 1], w_im[nq - 1], s_re, s_im)
    rows_re, rows_im = [s_re], [s_im]
    for _ in range(SEG_LEVELS):
        rows_re.append(q_re)
        rows_im.append(q_im)
        q_re, q_im = _cmul(q_re, q_im, q_re, q_im)
    c_all = jnp.stack([jnp.concatenate([jnp.stack(rows_re, 0), w_re], 0),
                       jnp.concatenate([jnp.stack(rows_im, 0), w_im], 0)], 0)
    n_rows = c_all.shape[1]
    c_all = c_all.reshape(2, n_rows, 2, SLABS, SLAB_STATE)
    coef = jnp.transpose(c_all, (3, 2, 0, 1, 4))
    return mt.astype(BF16), enc.astype(BF16), dect.astype(BF16), coef


def _post_kernel(x_ref, ya_ref, ys_ref, g_ref, wglu_ref, bglu_ref, wout_ref, ln2_ref,
                 wup_ref, wdown_ref, o_ref):
    nseg, tq, _ = x_ref.shape
    tm = nseg * tq
    ys = jnp.concatenate(
        [jnp.concatenate([ys_ref[sl, pl.ds(r, tq, stride=nseg), :] for sl in range(SLABS)], axis=1)
         for r in range(nseg)], axis=0)
    y = jax.nn.gelu(ys)
    z = jnp.dot(y.astype(BF16), wglu_ref[...], preferred_element_type=F32) + bglu_ref[...]
    y = y * jax.nn.sigmoid(z)
    g = g_ref[...].reshape(tm, 2 * D_MODEL).astype(F32)
    ya = ya_ref[...].reshape(tm, D_MODEL).astype(F32)
    mixed = jax.nn.sigmoid(g[:, :D_MODEL]) * ya + jax.nn.sigmoid(g[:, D_MODEL:]) * y
    x = x_ref[...].reshape(tm, D_MODEL)
    x = x + jnp.dot(mixed.astype(BF16), wout_ref[...], preferred_element_type=F32)
    ms = jnp.mean(x * x, axis=-1, keepdims=True)
    h = (x * lax.rsqrt(ms + EPS) * ln2_ref[...]).astype(BF16)
    up = jnp.dot(h, wup_ref[...], preferred_element_type=F32)
    act = jnp.square(jnp.maximum(up, 0.0)).astype(BF16)
    out = x + jnp.dot(act, wdown_ref[...], preferred_element_type=F32)
    o_ref[...] = out.reshape(nseg, tq, D_MODEL)


def _post(x2, ya, ys4, g, w_glu, b_glu, w_out, ln2, w_up, w_down, tq):
    b, _, seq, _ = ys4.shape
    per_seq = seq // SUBLANES // tq
    seg = lambda c: pl.BlockSpec((None, SUBLANES, tq, c), lambda i: (i // per_seq, 0, i % per_seq, 0))
    interleaved = pl.BlockSpec((None, SLABS, tq * SUBLANES, LANES),
                               lambda i: (i // per_seq, 0, i % per_seq, 0))
    out = pl.pallas_call(
        _post_kernel,
        grid=(b * per_seq,),
        in_specs=[seg(D_MODEL), seg(D_MODEL), interleaved, seg(2 * D_MODEL),
                  _const_spec((D_MODEL, D_MODEL)), _const_spec((1, D_MODEL)),
                  _const_spec((D_MODEL, D_MODEL)), _const_spec((1, D_MODEL)),
                  _const_spec((D_MODEL, D_FF)), _const_spec((D_FF, D_MODEL))],
        out_specs=seg(D_MODEL),
        out_shape=jax.ShapeDtypeStruct((b, SUBLANES, seq // SUBLANES, D_MODEL), F32),
        compiler_params=pltpu.CompilerParams(dimension_semantics=("parallel",),
                                             vmem_limit_bytes=VMEM_LIMIT),
        name="post",
    )(_seg_view(x2, b, seq), _seg_view(ya, b, seq), ys4, _seg_view(g, b, seq),
      w_glu, b_glu, w_out, ln2, w_up, w_down)
    return out.reshape(b, seq, D_MODEL)


def _t5_bucket(rel):
    nb = N_BUCKETS // 2
    ret = (rel > 0).astype(np.int32) * nb
    n = np.abs(rel)
    max_exact = nb // 2
    n_safe = np.maximum(n, 1).astype(np.float32)
    large = max_exact + (np.log(n_safe / max_exact) / math.log(MAX_DISTANCE / max_exact)
                         * (nb - max_exact)).astype(np.int32)
    large = np.minimum(large, nb - 1)
    return (ret + np.where(n < max_exact, n, large)).astype(np.int32)


def _band_bias(rel_table):
    qi = np.arange(BLOCK)[:, None]
    kj = np.arange(3 * BLOCK)[None, :]
    rel = kj - BLOCK - qi
    onehot = (np.arange(N_BUCKETS)[:, None] == _t5_bucket(rel).reshape(1, -1)).astype(np.float32)
    bias = jnp.dot(rel_table.astype(F32).T * LOG2E, jnp.asarray(onehot), precision='highest')
    bias = bias.reshape(N_HEADS, BLOCK, 3 * BLOCK)
    in_band = np.abs(rel) <= WINDOW
    keep = np.stack([in_band & (kj >= BLOCK), in_band, in_band & (kj < 2 * BLOCK)], 0)
    bias = jnp.where(jnp.asarray(keep)[:, None], bias[None], NEG_INF)
    bias = bias.reshape(3, N_KV_HEADS, 2, 2, BLOCK, 3 * BLOCK)
    return jnp.transpose(bias, (0, 1, 2, 4, 3, 5)).reshape(3, N_KV_HEADS, 2 * BLOCK, 6 * BLOCK)


def _sum_mat():
    first = np.arange(6 * BLOCK)[:, None] < 3 * BLOCK
    low = np.arange(LANES)[None, :] < HEAD_DIM
    return jnp.asarray(first == low, BF16)


def _head_mats(width):
    heads = width // HEAD_DIM
    sel = (np.arange(width)[:, None] // HEAD_DIM == np.arange(LANES)[None, :]).astype(np.float32)
    assert heads <= LANES
    return jnp.asarray(sel / HEAD_DIM, BF16), jnp.asarray(sel.T, BF16)


def _layer(x, bias, sink, ln1, w_in, qg, kg, s5w, dsk, w_glu, b_glu, w_out, ln2, w_up, w_down):
    b, s, _ = x.shape
    x2 = x.reshape(b * s, D_MODEL)
    sq, eq = _head_mats(ATTN_Q)
    sk, ek = _head_mats(ATTN_KV)
    rows_per_seg = s // SUBLANES
    q, kvx, u4, g = _in_proj(x2, ln1, w_in, qg, kg, sq, eq, sk, ek, b, s, min(TQ_IN, rows_per_seg))
    ya = _attn(q.reshape(b, s, ATTN_Q), kvx.reshape(b, s, 4 * ATTN_KV), bias, _sum_mat(), sink)
    ys4 = _s5(u4, *s5w, dsk)
    return _post(x2, ya.reshape(b * s, ATTN_Q), ys4, g, w_glu, b_glu, w_out, ln2, w_up, w_down,
                 min(TQ_POST, rows_per_seg))


def kernel(x_prompt, x_sample, rel_table, ln1, w_in, q_gain, k_gain, sink, lam_re, lam_im, log_dt,
           b_re, b_im, c_re, c_im, d_skip, w_glu, b_glu, w_out, ln2, w_up, w_down):
    assert ln1.shape[0] == 1, "single layer"
    bias = _band_bias(rel_table)
    outs = []
    s5w_by_nq = {}
    for x in (x_prompt, x_sample):
        seq = x.shape[1]
        nq = seq // (CHUNK * SUBLANES)
        assert nq * CHUNK * SUBLANES == seq and seq % BLOCK == 0
        if nq not in s5w_by_nq:
            s5w_by_nq[nq] = _s5_weights(lam_re[0], lam_im[0], log_dt[0], b_re[0], b_im[0],
                                        c_re[0], c_im[0], nq)
        s5w = s5w_by_nq[nq]
        outs.append(_layer(
            x, bias, sink[0].astype(F32) * LOG2E,
            ln1[0].astype(F32)[None], w_in[0].astype(BF16),
            jnp.tile(q_gain[0].astype(F32), N_HEADS)[None],
            jnp.tile(k_gain[0].astype(F32), N_KV_HEADS)[None],
            s5w, d_skip[0].astype(F32).reshape(SLABS, 1, LANES),
            w_glu[0].astype(BF16), b_glu[0].astype(F32)[None], w_out[0].astype(BF16),
            ln2[0].astype(F32)[None], w_up[0].astype(BF16), w_down[0].astype(BF16)))
    return tuple(outs)
```

```python
import math

import numpy as np
import jax
import jax.numpy as jnp
from jax import lax
from jax.experimental import pallas as pl
from jax.experimental.pallas import tpu as pltpu

D_MODEL = 1024
N_HEADS = 16
N_KV_HEADS = 4
HEAD_DIM = 64
Q_PER_KV = N_HEADS // N_KV_HEADS
WINDOW = 128
BLOCK = 128
N_BUCKETS = 32
MAX_DISTANCE = 128
ATTN_Q = N_HEADS * HEAD_DIM
ATTN_KV = N_KV_HEADS * HEAD_DIM
NEG_INF = -1e30
SSM_WIDTH = D_MODEL
SSM_GROUP = 16
SSM_GROUPS = SSM_WIDTH // SSM_GROUP
SSM_STATE = 64
D_FF = 4 * D_MODEL
EPS = 1e-6
IN_COLS = ATTN_Q + 2 * ATTN_KV + SSM_WIDTH + 2 * D_MODEL

LANES = 128
SUBLANES = 8
SEG_LEVELS = 3
CHUNK = 16
SLABS = SSM_WIDTH // LANES
GROUPS_PER_SLAB = LANES // SSM_GROUP
SLAB_STATE = GROUPS_PER_SLAB * SSM_STATE
VMEM_LIMIT = 56 * 1024 * 1024
TQ_IN = 64
TQ_POST = 64
QBLOCKS = 4

LOG2E = math.log2(math.e)
Q_SCALE = HEAD_DIM ** -0.5 * LOG2E

BF16 = jnp.bfloat16
F32 = jnp.float32


def _const_spec(shape):
    nd = len(shape)
    return pl.BlockSpec(shape, lambda *_: (0,) * nd, pipeline_mode=pl.Buffered(1))


def _in_proj_kernel(x_ref, ln1_ref, w_ref, qg_ref, kg_ref, sq_ref, sk_ref,
                    q_ref, kv_ref, u_ref, g_ref):
    nseg, tq, _ = x_ref.shape
    tm = nseg * tq
    x = x_ref[...].reshape(tm, D_MODEL)
    ms = jnp.mean(x * x, axis=-1, keepdims=True)
    h = (x * lax.rsqrt(ms + EPS) * ln1_ref[...]).astype(BF16)

    lane = lax.broadcasted_iota(jnp.int32, (tm, LANES), 1)

    def head_norm(t, s_ref, gain):
        msq = jnp.dot((t * t).astype(BF16), s_ref[...], preferred_element_type=F32)
        r = lax.rsqrt(msq + EPS)
        spread = [jnp.take_along_axis(r, 2 * j + (lane >= HEAD_DIM).astype(jnp.int32), axis=1)
                  for j in range(t.shape[1] // LANES)]
        return t * jnp.concatenate(spread, axis=1) * gain

    c0 = 0
    q = jnp.dot(h, w_ref[:, c0:c0 + ATTN_Q], preferred_element_type=F32)
    q = (head_norm(q, sq_ref, qg_ref[...]) * Q_SCALE).astype(BF16)
    q_ref[...] = q.reshape(nseg, tq, ATTN_Q)
    c0 += ATTN_Q
    k = jnp.dot(h, w_ref[:, c0:c0 + ATTN_KV], preferred_element_type=F32)
    k = head_norm(k, sk_ref, kg_ref[...])
    c0 += ATTN_KV
    v = jnp.dot(h, w_ref[:, c0:c0 + ATTN_KV], preferred_element_type=F32)
    c0 += ATTN_KV
    kvx = jnp.concatenate([k, pltpu.roll(k, HEAD_DIM, axis=1), v, pltpu.roll(v, HEAD_DIM, axis=1)],
                          axis=1).astype(BF16)
    kv_ref[...] = kvx.reshape(nseg, tq, 4 * ATTN_KV)
    u = jnp.dot(h, w_ref[:, c0:c0 + SSM_WIDTH], preferred_element_type=F32)
    for sl in range(SLABS):
        for r in range(nseg):
            u_ref[sl, pl.ds(r, tq, stride=nseg), :] = u[r * tq:(r + 1) * tq, sl * LANES:(sl + 1) * LANES]
    c0 += SSM_WIDTH
    g = jnp.dot(h, w_ref[:, c0:c0 + 2 * D_MODEL], preferred_element_type=F32).astype(BF16)
    g_ref[...] = g.reshape(nseg, tq, 2 * D_MODEL)


def _seg_view(a, b, seq):
    return a.reshape(b, SUBLANES, seq // SUBLANES, a.shape[-1])


def _in_proj(x2, ln1, w_in, qg, kg, sq, sk, b, seq, tq):
    n = x2.shape[0]
    per_seq = seq // SUBLANES // tq
    seg = lambda c: pl.BlockSpec((None, SUBLANES, tq, c), lambda i: (i // per_seq, 0, i % per_seq, 0))
    interleaved = pl.BlockSpec((None, SLABS, tq * SUBLANES, LANES),
                               lambda i: (i // per_seq, 0, i % per_seq, 0))
    seg_shape = lambda c, dt: jax.ShapeDtypeStruct((b, SUBLANES, seq // SUBLANES, c), dt)
    q, kvx, u, g = pl.pallas_call(
        _in_proj_kernel,
        grid=(b * per_seq,),
        in_specs=[seg(D_MODEL), _const_spec((1, D_MODEL)), _const_spec((D_MODEL, IN_COLS)),
                  _const_spec((1, ATTN_Q)), _const_spec((1, ATTN_KV)),
                  _const_spec((ATTN_Q, LANES)), _const_spec((ATTN_KV, LANES))],
        out_specs=[seg(ATTN_Q), seg(4 * ATTN_KV), interleaved, seg(2 * D_MODEL)],
        out_shape=[seg_shape(ATTN_Q, BF16), seg_shape(4 * ATTN_KV, BF16),
                   jax.ShapeDtypeStruct((b, SLABS, seq, LANES), F32), seg_shape(2 * D_MODEL, BF16)],
        compiler_params=pltpu.CompilerParams(dimension_semantics=("parallel",),
                                             vmem_limit_bytes=VMEM_LIMIT),
        name="in_proj",
    )(_seg_view(x2, b, seq), ln1, w_in, qg, kg, sq, sk)
    return q.reshape(n, ATTN_Q), kvx.reshape(n, 4 * ATTN_KV), u, g.reshape(n, 2 * D_MODEL)


_ROLLED_TILE = (0, 1, 1, 0)


def _attn_kernel(sink_ref, q_ref, kvp_ref, kvc_ref, kvn_ref, bias_ref, ones_ref, o_ref):
    n = pl.program_id(1)
    last = pl.num_programs(1) - 1
    kvx = jnp.concatenate([kvp_ref[0], kvc_ref[0], kvn_ref[0]], axis=0)
    low = lax.broadcasted_iota(jnp.int32, (1, LANES), 1) < HEAD_DIM
    zero = jnp.zeros((), BF16)

    def halves(base, kv):
        plain = kvx[:, base + (kv // 2) * LANES:base + (kv // 2 + 1) * LANES]
        r0 = base + ATTN_KV + _ROLLED_TILE[kv] * LANES
        rolled = kvx[:, r0:r0 + LANES]
        lo_src, hi_src = (plain, rolled) if kv % 2 == 0 else (rolled, plain)
        return jnp.where(low, lo_src, zero), jnp.where(low, zero, hi_src)

    top = lax.broadcasted_iota(jnp.int32, (2 * BLOCK, 1), 0) < BLOCK
    outs = [[] for _ in range(QBLOCKS)]
    for kv in range(N_KV_HEADS):
        ke, ko = halves(0, kv)
        ve, vo = halves(2 * ATTN_KV, kv)
        sink_e = jnp.where(top, sink_ref[4 * kv], sink_ref[4 * kv + 2])
        sink_o = jnp.where(top, sink_ref[4 * kv + 1], sink_ref[4 * kv + 3])
        for qb in range(QBLOCKS):
            rows = slice(qb * BLOCK, (qb + 3) * BLOCK)
            keys = jnp.concatenate([ke[rows], ko[rows]], axis=0)
            vals = jnp.concatenate([jnp.concatenate([ve[rows], vo[rows]], axis=0), ones_ref[...]],
                                   axis=1)
            if qb == 0:
                variant = jnp.where(n == 0, 0, 1)
            elif qb == QBLOCKS - 1:
                variant = jnp.where(n == last, 2, 1)
            else:
                variant = 1
            qrows = slice(qb * BLOCK, (qb + 1) * BLOCK)
            q2 = jnp.concatenate([q_ref[0, qrows, (2 * kv) * LANES:(2 * kv + 1) * LANES],
                                  q_ref[0, qrows, (2 * kv + 1) * LANES:(2 * kv + 2) * LANES]], axis=0)
            s = lax.dot_general(q2, keys, (((1,), (1,)), ((), ())), preferred_element_type=F32)
            s = s + bias_ref[variant, kv]
            se, so = s[:, :3 * BLOCK], s[:, 3 * BLOCK:]
            me = jnp.maximum(jnp.max(se, axis=-1, keepdims=True), sink_e)
            mo = jnp.maximum(jnp.max(so, axis=-1, keepdims=True), sink_o)
            p = jnp.concatenate([jnp.exp2(se - me), jnp.exp2(so - mo)], axis=1).astype(BF16)
            od = jnp.dot(p, vals, preferred_element_type=F32)
            den = od[:, LANES:] + jnp.where(low, jnp.exp2(sink_e - me), jnp.exp2(sink_o - mo))
            o = od[:, :LANES] / den
            outs[qb] += [o[:BLOCK], o[BLOCK:]]
    for qb in range(QBLOCKS):
        o_ref[0, qb * BLOCK:(qb + 1) * BLOCK, :] = jnp.concatenate(outs[qb], axis=-1).astype(BF16)


def _attn(q, kvx, bias, ones, sink):
    b, s, _ = q.shape
    nb = s // BLOCK
    assert nb % QBLOCKS == 0 and QBLOCKS >= 2
    steps = nb // QBLOCKS
    edge_spec = lambda f: pl.BlockSpec((1, BLOCK, 4 * ATTN_KV), f)
    wide = lambda c: pl.BlockSpec((1, QBLOCKS * BLOCK, c), lambda bi, n, *_: (bi, n, 0))
    prev = lambda bi, n, *_: (bi, jnp.maximum(n * QBLOCKS - 1, 0), 0)
    nxt = lambda bi, n, *_: (bi, jnp.minimum((n + 1) * QBLOCKS, nb - 1), 0)
    grid_spec = pltpu.PrefetchScalarGridSpec(
        num_scalar_prefetch=1,
        grid=(b, steps),
        in_specs=[wide(ATTN_Q), edge_spec(prev), wide(4 * ATTN_KV), edge_spec(nxt),
                  _const_spec(bias.shape), _const_spec(ones.shape)],
        out_specs=wide(ATTN_Q),
    )
    return pl.pallas_call(
        _attn_kernel,
        grid_spec=grid_spec,
        out_shape=jax.ShapeDtypeStruct((b, s, ATTN_Q), BF16),
        compiler_params=pltpu.CompilerParams(dimension_semantics=("parallel", "arbitrary"),
                                             vmem_limit_bytes=VMEM_LIMIT),
        name="attn",
    )(sink, q, kvx, kvx, kvx, bias, ones)


def _s5_kernel(u_ref, mt_ref, enc_ref, dect_ref, coef_ref, dsk_ref, y_ref):
    nq = u_ref.shape[0] // (CHUNK * SUBLANES)
    tile_rows = lambda x, q: x[q * SUBLANES:(q + 1) * SUBLANES]
    tile = lambda q, s: pl.ds((q * CHUNK + s) * SUBLANES, SUBLANES)

    xs = [jnp.concatenate([u_ref[tile(q, s), :] for q in range(nq)], axis=0) for s in range(CHUNK)]
    xst = [x.T for x in xs]

    yts, pair_states = [], []
    for g in range(GROUPS_PER_SLAB):
        ugt = jnp.concatenate([xt[g * SSM_GROUP:(g + 1) * SSM_GROUP] for xt in xst], axis=0)
        ug = ugt.T.astype(BF16)
        yts.append(jnp.dot(mt_ref[0, g], ugt.astype(BF16), preferred_element_type=F32))
        part = jnp.dot(ug, enc_ref[0, g], preferred_element_type=F32)
        if g % 2 == 0:
            pair_states.append(part)
        else:
            pair_states[-1] = pair_states[-1] + part

    def state(d, ri):
        c0 = (2 * d + ri) * LANES
        return jnp.concatenate([p[:, c0:c0 + LANES] for p in pair_states], axis=1)

    def coef(d, row):
        return coef_ref[0, d, 0, row:row + 1, :], coef_ref[0, d, 1, row:row + 1, :]

    seg = lax.broadcasted_iota(jnp.int32, (SUBLANES, 1), 0)

    def seg_shift(x, sh, forward):
        if forward:
            return jnp.where(seg >= sh, pltpu.roll(x, sh, axis=0), 0.0)
        return jnp.where(seg < SUBLANES - sh, pltpu.roll(x, SUBLANES - sh, axis=0), 0.0)

    carried = []
    for d in range(2):
        forward = d == 0
        s_re, s_im = state(d, 0), state(d, 1)
        a_re, a_im = coef(d, 0)
        order = list(range(nq)) if forward else list(range(nq - 1, -1, -1))
        t_re, t_im = {}, {}
        for n, q in enumerate(order):
            cur_re, cur_im = tile_rows(s_re, q), tile_rows(s_im, q)
            if n:
                p_re, p_im = t_re[order[n - 1]], t_im[order[n - 1]]
                cur_re = cur_re + (a_re * p_re - a_im * p_im)
                cur_im = cur_im + (a_re * p_im + a_im * p_re)
            t_re[q], t_im[q] = cur_re, cur_im
        e_re, e_im = t_re[order[-1]], t_im[order[-1]]
        for lv in range(SEG_LEVELS):
            b_re, b_im = coef(d, 1 + lv)
            h_re, h_im = seg_shift(e_re, 1 << lv, forward), seg_shift(e_im, 1 << lv, forward)
            e_re, e_im = e_re + (b_re * h_re - b_im * h_im), e_im + (b_re * h_im + b_im * h_re)
        c_re, c_im = seg_shift(e_re, 1, forward), seg_shift(e_im, 1, forward)
        x_re, x_im = [], []
        for q in range(nq):
            n = order.index(q)
            w_re, w_im = coef(d, 1 + SEG_LEVELS + n)
            r_re, r_im = w_re * c_re - w_im * c_im, w_re * c_im + w_im * c_re
            if n:
                r_re, r_im = r_re + t_re[order[n - 1]], r_im + t_im[order[n - 1]]
            x_re.append(r_re)
            x_im.append(r_im)
        carried += [jnp.concatenate(x_re, axis=0), jnp.concatenate(x_im, axis=0)]

    for g in range(GROUPS_PER_SLAB):
        c0 = (g // 2) * LANES
        xpair = jnp.concatenate([c[:, c0:c0 + LANES] for c in carried], axis=1).astype(BF16)
        yts[g] = yts[g] + lax.dot_general(dect_ref[0, g], xpair, (((1,), (1,)), ((), ())),
                                          preferred_element_type=F32)
    dsk = dsk_ref[0]
    for t in range(CHUNK):
        yt = jnp.concatenate([y[t * SSM_GROUP:(t + 1) * SSM_GROUP] for y in yts], axis=0)
        y = yt.T + dsk * xs[t]
        for q in range(nq):
            y_ref[tile(q, t), :] = tile_rows(y, q)


def _s5(u4, mt, enc, dect, coef, dsk):
    b, _, seq, _ = u4.shape
    per_slab = lambda a: pl.BlockSpec((1,) + a.shape[1:], lambda sl, bi: (sl,) + (0,) * (a.ndim - 1))
    io = pl.BlockSpec((None, None, seq, LANES), lambda sl, bi: (bi, sl, 0, 0))
    return pl.pallas_call(
        _s5_kernel,
        grid=(SLABS, b),
        in_specs=[io, per_slab(mt), per_slab(enc), per_slab(dect), per_slab(coef), per_slab(dsk)],
        out_specs=io,
        out_shape=jax.ShapeDtypeStruct(u4.shape, F32),
        compiler_params=pltpu.CompilerParams(dimension_semantics=("parallel", "parallel"),
                                             vmem_limit_bytes=VMEM_LIMIT),
        name="s5",
    )(u4, mt, enc, dect, coef, dsk)


def _cmul(ar, ai, br, bi):
    return ar * br - ai * bi, ar * bi + ai * br


def _cpowers(a_re, a_im, n):
    p_re, p_im = jnp.ones_like(a_re)[None], jnp.zeros_like(a_im)[None]
    s_re, s_im = a_re, a_im
    while p_re.shape[0] < n:
        h_re, h_im = _cmul(p_re, p_im, s_re[None], s_im[None])
        p_re, p_im = jnp.concatenate([p_re, h_re], 0), jnp.concatenate([p_im, h_im], 0)
        s_re, s_im = _cmul(s_re, s_im, s_re, s_im)
    return p_re[:n], p_im[:n]


def _s5_gen_kernel(bm_ref, cm_ref, apow_ref, enc_ref, dect_ref):
    def put(ref, d, ri, step, tile):
        tile = tile.astype(BF16)
        c0 = (2 * d + ri) * LANES
        for g in range(GROUPS_PER_SLAB):
            p0 = (g // 2) * LANES
            ref[0, g, step * SSM_GROUP:(step + 1) * SSM_GROUP, c0:c0 + LANES] = (
                tile[g * SSM_GROUP:(g + 1) * SSM_GROUP, p0:p0 + LANES])

    for d in range(2):
        for k in range(CHUNK + 1):
            a_re, a_im = apow_ref[0, d, 0, k:k + 1, :], apow_ref[0, d, 1, k:k + 1, :]
            if k < CHUNK:
                s = CHUNK - 1 - k if d == 0 else k
                e_re, e_im = _cmul(bm_ref[0, d, 0], bm_ref[0, d, 1], a_re, a_im)
                put(enc_ref, d, 0, s, e_re)
                put(enc_ref, d, 1, s, e_im)
            if k >= 1:
                t = k - 1 if d == 0 else CHUNK - k
                f_re, f_im = _cmul(cm_ref[0, d, 0], cm_ref[0, d, 1], a_re, a_im)
                put(dect_ref, d, 0, t, f_re)
                put(dect_ref, d, 1, t, -f_im)


def _s5_gen(bm, cm, apow):
    small = lambda a: pl.BlockSpec((1,) + a.shape[1:], lambda sl: (sl,) + (0,) * (a.ndim - 1))
    shape = (SLABS, GROUPS_PER_SLAB, CHUNK * SSM_GROUP, 4 * LANES)
    out = pl.BlockSpec((1,) + shape[1:], lambda sl: (sl, 0, 0, 0))
    return pl.pallas_call(
        _s5_gen_kernel,
        grid=(SLABS,),
        in_specs=[small(bm), small(cm), small(apow)],
        out_specs=[out, out],
        out_shape=[jax.ShapeDtypeStruct(shape, BF16), jax.ShapeDtypeStruct(shape, BF16)],
        compiler_params=pltpu.CompilerParams(dimension_semantics=("parallel",),
                                             vmem_limit_bytes=VMEM_LIMIT),
        name="s5_gen",
    )(bm, cm, apow)


def _slab_tiles(t_re, t_im):
    t = jnp.stack([t_re, t_im], 1).astype(F32)
    t = t.reshape(2, 2, SLABS, GROUPS_PER_SLAB, SSM_GROUP, SSM_STATE)
    eye = jnp.eye(GROUPS_PER_SLAB, dtype=F32)
    t = t[:, :, :, :, :, None, :] * eye[:, None, :, None]
    t = t.reshape(2, 2, SLABS, LANES, SLAB_STATE)
    return jnp.transpose(t, (2, 0, 1, 3, 4))


def _s5_weights(lam_re, lam_im, log_dt, b_re, b_im, c_re, c_im, nq):
    G, C = SSM_GROUPS, SSM_GROUP
    dt = jnp.exp(log_dt.astype(F32))[..., None]
    lr, li = lam_re.astype(F32), lam_im.astype(F32)
    mag = jnp.exp(lr * dt)
    ang = li * dt
    a_re, a_im = mag * jnp.cos(ang), mag * jnp.sin(ang)
    den = lr * lr + li * li
    nr, ni = a_re - 1.0, a_im
    z_re = (nr * lr + ni * li) / den
    z_im = (ni * lr - nr * li) / den
    br, bi = b_re.astype(F32), b_im.astype(F32)
    bb_re = z_re[..., None] * br - z_im[..., None] * bi
    bb_im = z_re[..., None] * bi + z_im[..., None] * br
    cr = jnp.swapaxes(c_re.astype(F32), -1, -2)
    ci = jnp.swapaxes(c_im.astype(F32), -1, -2)

    pr, pi = [jnp.ones_like(a_re)], [jnp.zeros_like(a_im)]
    for _ in range(CHUNK):
        nr_, ni_ = _cmul(pr[-1], pi[-1], a_re, a_im)
        pr.append(nr_)
        pi.append(ni_)
    pw_re, pw_im = jnp.stack(pr, 0), jnp.stack(pi, 0)

    ab_re, ab_im = _cmul(pw_re[:CHUNK, ..., None], pw_im[:CHUNK, ..., None], bb_re[None], bb_im[None])
    kern = (jnp.einsum('dgpc,kdgpe->kdgce', cr, ab_re, precision='highest')
            - jnp.einsum('dgpc,kdgpe->kdgce', ci, ab_im, precision='highest'))
    lag_tab = jnp.concatenate([kern[::-1, 0][:-1], (kern[0, 0] + kern[0, 1])[None], kern[1:, 1]], 0)
    lag_tab = jnp.transpose(lag_tab, (1, 2, 0, 3)).reshape(G, C, (2 * CHUNK - 1) * C)
    mt = jnp.stack([lag_tab[:, :, (CHUNK - 1 - t) * C:(CHUNK - 1 - t) * C + CHUNK * C]
                    for t in range(CHUNK)], 1).reshape(SLABS, GROUPS_PER_SLAB, CHUNK * C, CHUNK * C)

    apow = jnp.stack([pw_re, pw_im], 0).reshape(2, CHUNK + 1, 2, SLABS, SLAB_STATE)
    apow = jnp.transpose(apow, (3, 2, 0, 1, 4))
    enc, dect = _s5_gen(_slab_tiles(jnp.swapaxes(bb_re, -1, -2), jnp.swapaxes(bb_im, -1, -2)),
                        _slab_tiles(c_re, c_im), apow)

    s_re, s_im = pr[CHUNK], pi[CHUNK]
    w_re, w_im = _cpowers(s_re, s_im, nq)
    q_re, q_im = _cmul(w_re[nq - 1], w_im[nq - 1], s_re, s_im)
    rows_re, rows_im = [s_re], [s_im]
    for _ in range(SEG_LEVELS):
        rows_re.append(q_re)
        rows_im.append(q_im)
        q_re, q_im = _cmul(q_re, q_im, q_re, q_im)
    c_all = jnp.stack([jnp.concatenate([jnp.stack(rows_re, 0), w_re], 0),
                       jnp.concatenate([jnp.stack(rows_im, 0), w_im], 0)], 0)
    n_rows = c_all.shape[1]
    c_all = c_all.reshape(2, n_rows, 2, SLABS, SLAB_STATE)
    coef = jnp.transpose(c_all, (3, 2, 0, 1, 4))
    return mt.astype(BF16), enc, dect, coef


def _post_kernel(x_ref, ya_ref, ys_ref, g_ref, wglu_ref, bglu_ref, wout_ref, ln2_ref,
                 wup_ref, wdown_ref, o_ref):
    nseg, tq, _ = x_ref.shape
    tm = nseg * tq
    ys = jnp.concatenate(
        [jnp.concatenate([ys_ref[sl, pl.ds(r, tq, stride=nseg), :] for sl in range(SLABS)], axis=1)
         for r in range(nseg)], axis=0)
    y = jax.nn.gelu(ys)
    z = jnp.dot(y.astype(BF16), wglu_ref[...], preferred_element_type=F32) + bglu_ref[...]
    y = y * jax.nn.sigmoid(z)
    g = g_ref[...].reshape(tm, 2 * D_MODEL).astype(F32)
    ya = ya_ref[...].reshape(tm, D_MODEL).astype(F32)
    mixed = jax.nn.sigmoid(g[:, :D_MODEL]) * ya + jax.nn.sigmoid(g[:, D_MODEL:]) * y
    x = x_ref[...].reshape(tm, D_MODEL)
    x = x + jnp.dot(mixed.astype(BF16), wout_ref[...], preferred_element_type=F32)
    ms = jnp.mean(x * x, axis=-1, keepdims=True)
    h = (x * lax.rsqrt(ms + EPS) * ln2_ref[...]).astype(BF16)
    up = jnp.dot(h, wup_ref[...], preferred_element_type=F32)
    act = jnp.square(jnp.maximum(up, 0.0)).astype(BF16)
    out = x + jnp.dot(act, wdown_ref[...], preferred_element_type=F32)
    o_ref[...] = out.reshape(nseg, tq, D_MODEL)


def _post(x2, ya, ys4, g, w_glu, b_glu, w_out, ln2, w_up, w_down, tq):
    b, _, seq, _ = ys4.shape
    per_seq = seq // SUBLANES // tq
    seg = lambda c: pl.BlockSpec((None, SUBLANES, tq, c), lambda i: (i // per_seq, 0, i % per_seq, 0))
    interleaved = pl.BlockSpec((None, SLABS, tq * SUBLANES, LANES),
                               lambda i: (i // per_seq, 0, i % per_seq, 0))
    out = pl.pallas_call(
        _post_kernel,
        grid=(b * per_seq,),
        in_specs=[seg(D_MODEL), seg(D_MODEL), interleaved, seg(2 * D_MODEL),
                  _const_spec((D_MODEL, D_MODEL)), _const_spec((1, D_MODEL)),
                  _const_spec((D_MODEL, D_MODEL)), _const_spec((1, D_MODEL)),
                  _const_spec((D_MODEL, D_FF)), _const_spec((D_FF, D_MODEL))],
        out_specs=seg(D_MODEL),
        out_shape=jax.ShapeDtypeStruct((b, SUBLANES, seq // SUBLANES, D_MODEL), F32),
        compiler_params=pltpu.CompilerParams(dimension_semantics=("parallel",),
                                             vmem_limit_bytes=VMEM_LIMIT),
        name="post",
    )(_seg_view(x2, b, seq), _seg_view(ya, b, seq), ys4, _seg_view(g, b, seq),
      w_glu, b_glu, w_out, ln2, w_up, w_down)
    return out.reshape(b, seq, D_MODEL)


def _t5_bucket(rel):
    nb = N_BUCKETS // 2
    ret = (rel > 0).astype(np.int32) * nb
    n = np.abs(rel)
    max_exact = nb // 2
    n_safe = np.maximum(n, 1).astype(np.float32)
    large = max_exact + (np.log(n_safe / max_exact) / math.log(MAX_DISTANCE / max_exact)
                         * (nb - max_exact)).astype(np.int32)
    large = np.minimum(large, nb - 1)
    return (ret + np.where(n < max_exact, n, large)).astype(np.int32)


def _band_bias(rel_table):
    qi = np.arange(BLOCK)[:, None]
    kj = np.arange(3 * BLOCK)[None, :]
    rel = kj - BLOCK - qi
    onehot = (np.arange(N_BUCKETS)[:, None] == _t5_bucket(rel).reshape(1, -1)).astype(np.float32)
    bias = jnp.dot(rel_table.astype(F32).T * LOG2E, jnp.asarray(onehot), precision='highest')
    bias = bias.reshape(N_HEADS, BLOCK, 3 * BLOCK)
    in_band = np.abs(rel) <= WINDOW
    keep = np.stack([in_band & (kj >= BLOCK), in_band, in_band & (kj < 2 * BLOCK)], 0)
    bias = jnp.where(jnp.asarray(keep)[:, None], bias[None], NEG_INF)
    bias = bias.reshape(3, N_KV_HEADS, 2, 2, BLOCK, 3 * BLOCK)
    return jnp.transpose(bias, (0, 1, 2, 4, 3, 5)).reshape(3, N_KV_HEADS, 2 * BLOCK, 6 * BLOCK)


def _sum_mat():
    first = np.arange(6 * BLOCK)[:, None] < 3 * BLOCK
    low = np.arange(LANES)[None, :] < HEAD_DIM
    return jnp.asarray(first == low, BF16)


def _head_mean_mat(width):
    assert width // HEAD_DIM <= LANES
    sel = np.arange(width)[:, None] // HEAD_DIM == np.arange(LANES)[None, :]
    return jnp.asarray(sel.astype(np.float32) / HEAD_DIM, BF16)


def _layer(x, bias, sink, ln1, w_in, qg, kg, s5w, dsk, w_glu, b_glu, w_out, ln2, w_up, w_down):
    b, s, _ = x.shape
    x2 = x.reshape(b * s, D_MODEL)
    rows_per_seg = s // SUBLANES
    q, kvx, u4, g = _in_proj(x2, ln1, w_in, qg, kg, _head_mean_mat(ATTN_Q), _head_mean_mat(ATTN_KV),
                             b, s, min(TQ_IN, rows_per_seg))
    ya = _attn(q.reshape(b, s, ATTN_Q), kvx.reshape(b, s, 4 * ATTN_KV), bias, _sum_mat(), sink)
    ys4 = _s5(u4, *s5w, dsk)
    return _post(x2, ya.reshape(b * s, ATTN_Q), ys4, g, w_glu, b_glu, w_out, ln2, w_up, w_down,
                 min(TQ_POST, rows_per_seg))


def kernel(x_prompt, x_sample, rel_table, ln1, w_in, q_gain, k_gain, sink, lam_re, lam_im, log_dt,
           b_re, b_im, c_re, c_im, d_skip, w_glu, b_glu, w_out, ln2, w_up, w_down):
    assert ln1.shape[0] == 1, "single layer"
    bias = _band_bias(rel_table)
    outs = []
    s5w_by_nq = {}
    for x in (x_prompt, x_sample):
        seq = x.shape[1]
        nq = seq // (CHUNK * SUBLANES)
        assert nq * CHUNK * SUBLANES == seq and seq % BLOCK == 0
        if nq not in s5w_by_nq:
            s5w_by_nq[nq] = _s5_weights(lam_re[0], lam_im[0], log_dt[0], b_re[0], b_im[0],
                                        c_re[0], c_im[0], nq)
        s5w = s5w_by_nq[nq]
        outs.append(_layer(
            x, bias, sink[0].astype(F32) * LOG2E,
            ln1[0].astype(F32)[None], w_in[0].astype(BF16),
            jnp.tile(q_gain[0].astype(F32), N_HEADS)[None],
            jnp.tile(k_gain[0].astype(F32), N_KV_HEADS)[None],
            s5w, d_skip[0].astype(F32).reshape(SLABS, 1, LANES),
            w_glu[0].astype(BF16), b_glu[0].astype(F32)[None], w_out[0].astype(BF16),
            ln2[0].astype(F32)[None], w_up[0].astype(BF16), w_down[0].astype(BF16)))
    return tuple(outs)
```

```python
import math

import numpy as np
import jax
import jax.numpy as jnp
from jax import lax
from jax.experimental import pallas as pl
from jax.experimental.pallas import tpu as pltpu

D_MODEL = 1024
N_HEADS = 16
N_KV_HEADS = 4
HEAD_DIM = 64
Q_PER_KV = N_HEADS // N_KV_HEADS
WINDOW = 128
BLOCK = 128
N_BUCKETS = 32
MAX_DISTANCE = 128
ATTN_Q = N_HEADS * HEAD_DIM
ATTN_KV = N_KV_HEADS * HEAD_DIM
NEG_INF = -1e30
SSM_WIDTH = D_MODEL
SSM_GROUP = 16
SSM_GROUPS = SSM_WIDTH // SSM_GROUP
SSM_STATE = 64
D_FF = 4 * D_MODEL
EPS = 1e-6
IN_COLS = ATTN_Q + 2 * ATTN_KV + SSM_WIDTH + 2 * D_MODEL

LANES = 128
SUBLANES = 8
SEG_LEVELS = 3
CHUNK = 16
SLABS = SSM_WIDTH // LANES
GROUPS_PER_SLAB = LANES // SSM_GROUP
SLAB_STATE = GROUPS_PER_SLAB * SSM_STATE
VMEM_LIMIT = 56 * 1024 * 1024
TQ_IN = 64
TQ_POST = 64
QBLOCKS = 4
SLABS_PER_STEP = 2

LOG2E = math.log2(math.e)
Q_SCALE = HEAD_DIM ** -0.5 * LOG2E

BF16 = jnp.bfloat16
F32 = jnp.float32


def _const_spec(shape):
    nd = len(shape)
    return pl.BlockSpec(shape, lambda *_: (0,) * nd, pipeline_mode=pl.Buffered(1))


def _in_proj_kernel(x_ref, ln1_ref, w_ref, qg_ref, kg_ref, sq_ref, sk_ref,
                    q_ref, kv_ref, u_ref, g_ref):
    nseg, tq, _ = x_ref.shape
    tm = nseg * tq
    x = x_ref[...].reshape(tm, D_MODEL)
    ms = jnp.mean(x * x, axis=-1, keepdims=True)
    h = (x * lax.rsqrt(ms + EPS) * ln1_ref[...]).astype(BF16)

    lane = lax.broadcasted_iota(jnp.int32, (tm, LANES), 1)

    def head_norm(t, s_ref, gain):
        msq = jnp.dot((t * t).astype(BF16), s_ref[...], preferred_element_type=F32)
        r = lax.rsqrt(msq + EPS)
        spread = [jnp.take_along_axis(r, 2 * j + (lane >= HEAD_DIM).astype(jnp.int32), axis=1)
                  for j in range(t.shape[1] // LANES)]
        return t * jnp.concatenate(spread, axis=1) * gain

    c0 = 0
    q = jnp.dot(h, w_ref[:, c0:c0 + ATTN_Q], preferred_element_type=F32)
    q = (head_norm(q, sq_ref, qg_ref[...]) * Q_SCALE).astype(BF16)
    q_ref[...] = q.reshape(nseg, tq, ATTN_Q)
    c0 += ATTN_Q
    k = jnp.dot(h, w_ref[:, c0:c0 + ATTN_KV], preferred_element_type=F32)
    k = head_norm(k, sk_ref, kg_ref[...])
    c0 += ATTN_KV
    v = jnp.dot(h, w_ref[:, c0:c0 + ATTN_KV], preferred_element_type=F32)
    c0 += ATTN_KV
    kvx = jnp.concatenate([k, pltpu.roll(k, HEAD_DIM, axis=1), v, pltpu.roll(v, HEAD_DIM, axis=1)],
                          axis=1).astype(BF16)
    kv_ref[...] = kvx.reshape(nseg, tq, 4 * ATTN_KV)
    u = jnp.dot(h, w_ref[:, c0:c0 + SSM_WIDTH], preferred_element_type=F32)
    for sl in range(SLABS):
        for r in range(nseg):
            u_ref[sl, pl.ds(r, tq, stride=nseg), :] = u[r * tq:(r + 1) * tq, sl * LANES:(sl + 1) * LANES]
    c0 += SSM_WIDTH
    g = jnp.dot(h, w_ref[:, c0:c0 + 2 * D_MODEL], preferred_element_type=F32).astype(BF16)
    g_ref[...] = g.reshape(nseg, tq, 2 * D_MODEL)


def _seg_view(a, b, seq):
    return a.reshape(b, SUBLANES, seq // SUBLANES, a.shape[-1])


def _in_proj(x2, ln1, w_in, qg, kg, sq, sk, b, seq, tq):
    n = x2.shape[0]
    per_seq = seq // SUBLANES // tq
    seg = lambda c: pl.BlockSpec((None, SUBLANES, tq, c), lambda i: (i // per_seq, 0, i % per_seq, 0))
    interleaved = pl.BlockSpec((None, SLABS, tq * SUBLANES, LANES),
                               lambda i: (i // per_seq, 0, i % per_seq, 0))
    seg_shape = lambda c, dt: jax.ShapeDtypeStruct((b, SUBLANES, seq // SUBLANES, c), dt)
    q, kvx, u, g = pl.pallas_call(
        _in_proj_kernel,
        grid=(b * per_seq,),
        in_specs=[seg(D_MODEL), _const_spec((1, D_MODEL)), _const_spec((D_MODEL, IN_COLS)),
                  _const_spec((1, ATTN_Q)), _const_spec((1, ATTN_KV)),
                  _const_spec((ATTN_Q, LANES)), _const_spec((ATTN_KV, LANES))],
        out_specs=[seg(ATTN_Q), seg(4 * ATTN_KV), interleaved, seg(2 * D_MODEL)],
        out_shape=[seg_shape(ATTN_Q, BF16), seg_shape(4 * ATTN_KV, BF16),
                   jax.ShapeDtypeStruct((b, SLABS, seq, LANES), F32), seg_shape(2 * D_MODEL, BF16)],
        compiler_params=pltpu.CompilerParams(dimension_semantics=("parallel",),
                                             vmem_limit_bytes=VMEM_LIMIT),
        name="in_proj",
    )(_seg_view(x2, b, seq), ln1, w_in, qg, kg, sq, sk)
    return q.reshape(n, ATTN_Q), kvx.reshape(n, 4 * ATTN_KV), u, g.reshape(n, 2 * D_MODEL)


_ROLLED_TILE = (0, 1, 1, 0)


def _attn_kernel(sink_ref, q_ref, kvp_ref, kvc_ref, kvn_ref, bias_ref, ones_ref, o_ref):
    n = pl.program_id(1)
    last = pl.num_programs(1) - 1
    kvx = jnp.concatenate([kvp_ref[0], kvc_ref[0], kvn_ref[0]], axis=0)
    low = lax.broadcasted_iota(jnp.int32, (1, LANES), 1) < HEAD_DIM
    zero = jnp.zeros((), BF16)

    def halves(base, kv):
        plain = kvx[:, base + (kv // 2) * LANES:base + (kv // 2 + 1) * LANES]
        r0 = base + ATTN_KV + _ROLLED_TILE[kv] * LANES
        rolled = kvx[:, r0:r0 + LANES]
        lo_src, hi_src = (plain, rolled) if kv % 2 == 0 else (rolled, plain)
        return jnp.where(low, lo_src, zero), jnp.where(low, zero, hi_src)

    top = lax.broadcasted_iota(jnp.int32, (2 * BLOCK, 1), 0) < BLOCK
    outs = [[] for _ in range(QBLOCKS)]
    for kv in range(N_KV_HEADS):
        ke, ko = halves(0, kv)
        ve, vo = halves(2 * ATTN_KV, kv)
        sink_e = jnp.where(top, sink_ref[4 * kv], sink_ref[4 * kv + 2])
        sink_o = jnp.where(top, sink_ref[4 * kv + 1], sink_ref[4 * kv + 3])
        for qb in range(QBLOCKS):
            rows = slice(qb * BLOCK, (qb + 3) * BLOCK)
            keys = jnp.concatenate([ke[rows], ko[rows]], axis=0)
            vals = jnp.concatenate([jnp.concatenate([ve[rows], vo[rows]], axis=0), ones_ref[...]],
                                   axis=1)
            if qb == 0:
                variant = jnp.where(n == 0, 0, 1)
            elif qb == QBLOCKS - 1:
                variant = jnp.where(n == last, 2, 1)
            else:
                variant = 1
            qrows = slice(qb * BLOCK, (qb + 1) * BLOCK)
            q2 = jnp.concatenate([q_ref[0, qrows, (2 * kv) * LANES:(2 * kv + 1) * LANES],
                                  q_ref[0, qrows, (2 * kv + 1) * LANES:(2 * kv + 2) * LANES]], axis=0)
            s = lax.dot_general(q2, keys, (((1,), (1,)), ((), ())), preferred_element_type=F32)
            s = s + bias_ref[variant, kv]
            se, so = s[:, :3 * BLOCK], s[:, 3 * BLOCK:]
            me = jnp.maximum(jnp.max(se, axis=-1, keepdims=True), sink_e)
            mo = jnp.maximum(jnp.max(so, axis=-1, keepdims=True), sink_o)
            p = jnp.concatenate([jnp.exp2(se - me), jnp.exp2(so - mo)], axis=1).astype(BF16)
            od = jnp.dot(p, vals, preferred_element_type=F32)
            den = od[:, LANES:] + jnp.where(low, jnp.exp2(sink_e - me), jnp.exp2(sink_o - mo))
            o = od[:, :LANES] / den
            outs[qb] += [o[:BLOCK], o[BLOCK:]]
    for qb in range(QBLOCKS):
        o_ref[0, qb * BLOCK:(qb + 1) * BLOCK, :] = jnp.concatenate(outs[qb], axis=-1).astype(BF16)


def _attn(q, kvx, bias, ones, sink):
    b, s, _ = q.shape
    nb = s // BLOCK
    assert nb % QBLOCKS == 0 and QBLOCKS >= 2
    steps = nb // QBLOCKS
    edge_spec = lambda f: pl.BlockSpec((1, BLOCK, 4 * ATTN_KV), f)
    wide = lambda c: pl.BlockSpec((1, QBLOCKS * BLOCK, c), lambda bi, n, *_: (bi, n, 0))
    prev = lambda bi, n, *_: (bi, jnp.maximum(n * QBLOCKS - 1, 0), 0)
    nxt = lambda bi, n, *_: (bi, jnp.minimum((n + 1) * QBLOCKS, nb - 1), 0)
    grid_spec = pltpu.PrefetchScalarGridSpec(
        num_scalar_prefetch=1,
        grid=(b, steps),
        in_specs=[wide(ATTN_Q), edge_spec(prev), wide(4 * ATTN_KV), edge_spec(nxt),
                  _const_spec(bias.shape), _const_spec(ones.shape)],
        out_specs=wide(ATTN_Q),
    )
    return pl.pallas_call(
        _attn_kernel,
        grid_spec=grid_spec,
        out_shape=jax.ShapeDtypeStruct((b, s, ATTN_Q), BF16),
        compiler_params=pltpu.CompilerParams(dimension_semantics=("parallel", "arbitrary"),
                                             vmem_limit_bytes=VMEM_LIMIT),
        name="attn",
    )(sink, q, kvx, kvx, kvx, bias, ones)


def _s5_slab(u_ref, mt_ref, enc_ref, dect_ref, coef_ref, dsk_ref, y_ref):
    nq = u_ref.shape[0] // (CHUNK * SUBLANES)
    tile_rows = lambda x, q: x[q * SUBLANES:(q + 1) * SUBLANES]
    tile = lambda q, s: pl.ds((q * CHUNK + s) * SUBLANES, SUBLANES)

    xs = [jnp.concatenate([u_ref[tile(q, s), :] for q in range(nq)], axis=0) for s in range(CHUNK)]
    xst = [x.T for x in xs]

    yts, pair_states = [], []
    for g in range(GROUPS_PER_SLAB):
        ugt = jnp.concatenate([xt[g * SSM_GROUP:(g + 1) * SSM_GROUP] for xt in xst], axis=0)
        ug = ugt.T.astype(BF16)
        yts.append(jnp.dot(mt_ref[g], ugt.astype(BF16), preferred_element_type=F32))
        part = jnp.dot(ug, enc_ref[g], preferred_element_type=F32)
        if g % 2 == 0:
            pair_states.append(part)
        else:
            pair_states[-1] = pair_states[-1] + part

    def state(d, ri):
        c0 = (2 * d + ri) * LANES
        return jnp.concatenate([p[:, c0:c0 + LANES] for p in pair_states], axis=1)

    def coef(d, row):
        return coef_ref[d, 0, row:row + 1, :], coef_ref[d, 1, row:row + 1, :]

    seg = lax.broadcasted_iota(jnp.int32, (SUBLANES, 1), 0)

    def seg_shift(x, sh, forward):
        if forward:
            return jnp.where(seg >= sh, pltpu.roll(x, sh, axis=0), 0.0)
        return jnp.where(seg < SUBLANES - sh, pltpu.roll(x, SUBLANES - sh, axis=0), 0.0)

    carried = []
    for d in range(2):
        forward = d == 0
        s_re, s_im = state(d, 0), state(d, 1)
        a_re, a_im = coef(d, 0)
        order = list(range(nq)) if forward else list(range(nq - 1, -1, -1))
        t_re, t_im = {}, {}
        for n, q in enumerate(order):
            cur_re, cur_im = tile_rows(s_re, q), tile_rows(s_im, q)
            if n:
                p_re, p_im = t_re[order[n - 1]], t_im[order[n - 1]]
                cur_re = cur_re + (a_re * p_re - a_im * p_im)
                cur_im = cur_im + (a_re * p_im + a_im * p_re)
            t_re[q], t_im[q] = cur_re, cur_im
        e_re, e_im = t_re[order[-1]], t_im[order[-1]]
        for lv in range(SEG_LEVELS):
            b_re, b_im = coef(d, 1 + lv)
            h_re, h_im = seg_shift(e_re, 1 << lv, forward), seg_shift(e_im, 1 << lv, forward)
            e_re, e_im = e_re + (b_re * h_re - b_im * h_im), e_im + (b_re * h_im + b_im * h_re)
        c_re, c_im = seg_shift(e_re, 1, forward), seg_shift(e_im, 1, forward)
        x_re, x_im = [], []
        for q in range(nq):
            n = order.index(q)
            w_re, w_im = coef(d, 1 + SEG_LEVELS + n)
            r_re, r_im = w_re * c_re - w_im * c_im, w_re * c_im + w_im * c_re
            if n:
                r_re, r_im = r_re + t_re[order[n - 1]], r_im + t_im[order[n - 1]]
            x_re.append(r_re)
            x_im.append(r_im)
        carried += [jnp.concatenate(x_re, axis=0), jnp.concatenate(x_im, axis=0)]

    for g in range(GROUPS_PER_SLAB):
        c0 = (g // 2) * LANES
        xpair = jnp.concatenate([c[:, c0:c0 + LANES] for c in carried], axis=1).astype(BF16)
        yts[g] = yts[g] + lax.dot_general(dect_ref[g], xpair, (((1,), (1,)), ((), ())),
                                          preferred_element_type=F32)
    dsk = dsk_ref[...]
    for t in range(CHUNK):
        yt = jnp.concatenate([y[t * SSM_GROUP:(t + 1) * SSM_GROUP] for y in yts], axis=0)
        y = yt.T + dsk * xs[t]
        for q in range(nq):
            y_ref[tile(q, t), :] = tile_rows(y, q)


def _s5_kernel(u_ref, mt_ref, enc_ref, dect_ref, coef_ref, dsk_ref, y_ref):
    for i in range(SLABS_PER_STEP):
        _s5_slab(u_ref.at[i], mt_ref.at[i], enc_ref.at[i], dect_ref.at[i], coef_ref.at[i],
                 dsk_ref.at[i], y_ref.at[i])


def _s5(u4, mt, enc, dect, coef, dsk):
    b, _, seq, _ = u4.shape
    per_slab = lambda a: pl.BlockSpec((SLABS_PER_STEP,) + a.shape[1:],
                                      lambda sl, bi: (sl,) + (0,) * (a.ndim - 1),
                                      pipeline_mode=pl.Buffered(1))
    io = pl.BlockSpec((None, SLABS_PER_STEP, seq, LANES), lambda sl, bi: (bi, sl, 0, 0))
    return pl.pallas_call(
        _s5_kernel,
        grid=(SLABS // SLABS_PER_STEP, b),
        in_specs=[io, per_slab(mt), per_slab(enc), per_slab(dect), per_slab(coef), per_slab(dsk)],
        out_specs=io,
        out_shape=jax.ShapeDtypeStruct(u4.shape, F32),
        compiler_params=pltpu.CompilerParams(dimension_semantics=("parallel", "parallel"),
                                             vmem_limit_bytes=VMEM_LIMIT),
        name="s5",
    )(u4, mt, enc, dect, coef, dsk)


def _cmul(ar, ai, br, bi):
    return ar * br - ai * bi, ar * bi + ai * br


def _cpowers(a_re, a_im, n):
    p_re, p_im = jnp.ones_like(a_re)[None], jnp.zeros_like(a_im)[None]
    s_re, s_im = a_re, a_im
    while p_re.shape[0] < n:
        h_re, h_im = _cmul(p_re, p_im, s_re[None], s_im[None])
        p_re, p_im = jnp.concatenate([p_re, h_re], 0), jnp.concatenate([p_im, h_im], 0)
        s_re, s_im = _cmul(s_re, s_im, s_re, s_im)
    return p_re[:n], p_im[:n]


def _s5_gen_kernel(bm_ref, cm_ref, apow_ref, enc_ref, dect_ref):
    def put(ref, d, ri, step, tile):
        tile = tile.astype(BF16)
        c0 = (2 * d + ri) * LANES
        for g in range(GROUPS_PER_SLAB):
            p0 = (g // 2) * LANES
            ref[0, g, step * SSM_GROUP:(step + 1) * SSM_GROUP, c0:c0 + LANES] = (
                tile[g * SSM_GROUP:(g + 1) * SSM_GROUP, p0:p0 + LANES])

    for d in range(2):
        for k in range(CHUNK + 1):
            a_re, a_im = apow_ref[0, d, 0, k:k + 1, :], apow_ref[0, d, 1, k:k + 1, :]
            if k < CHUNK:
                s = CHUNK - 1 - k if d == 0 else k
                e_re, e_im = _cmul(bm_ref[0, d, 0], bm_ref[0, d, 1], a_re, a_im)
                put(enc_ref, d, 0, s, e_re)
                put(enc_ref, d, 1, s, e_im)
            if k >= 1:
                t = k - 1 if d == 0 else CHUNK - k
                f_re, f_im = _cmul(cm_ref[0, d, 0], cm_ref[0, d, 1], a_re, a_im)
                put(dect_ref, d, 0, t, f_re)
                put(dect_ref, d, 1, t, -f_im)


def _s5_gen(bm, cm, apow):
    small = lambda a: pl.BlockSpec((1,) + a.shape[1:], lambda sl: (sl,) + (0,) * (a.ndim - 1))
    shape = (SLABS, GROUPS_PER_SLAB, CHUNK * SSM_GROUP, 4 * LANES)
    out = pl.BlockSpec((1,) + shape[1:], lambda sl: (sl, 0, 0, 0))
    return pl.pallas_call(
        _s5_gen_kernel,
        grid=(SLABS,),
        in_specs=[small(bm), small(cm), small(apow)],
        out_specs=[out, out],
        out_shape=[jax.ShapeDtypeStruct(shape, BF16), jax.ShapeDtypeStruct(shape, BF16)],
        compiler_params=pltpu.CompilerParams(dimension_semantics=("parallel",),
                                             vmem_limit_bytes=VMEM_LIMIT),
        name="s5_gen",
    )(bm, cm, apow)


def _slab_tiles(t_re, t_im):
    t = jnp.stack([t_re, t_im], 1).astype(F32)
    t = t.reshape(2, 2, SLABS, GROUPS_PER_SLAB, SSM_GROUP, SSM_STATE)
    eye = jnp.eye(GROUPS_PER_SLAB, dtype=F32)
    t = t[:, :, :, :, :, None, :] * eye[:, None, :, None]
    t = t.reshape(2, 2, SLABS, LANES, SLAB_STATE)
    return jnp.transpose(t, (2, 0, 1, 3, 4))


def _s5_weights(lam_re, lam_im, log_dt, b_re, b_im, c_re, c_im, nq):
    G, C = SSM_GROUPS, SSM_GROUP
    dt = jnp.exp(log_dt.astype(F32))[..., None]
    lr, li = lam_re.astype(F32), lam_im.astype(F32)
    mag = jnp.exp(lr * dt)
    ang = li * dt
    a_re, a_im = mag * jnp.cos(ang), mag * jnp.sin(ang)
    den = lr * lr + li * li
    nr, ni = a_re - 1.0, a_im
    z_re = (nr * lr + ni * li) / den
    z_im = (ni * lr - nr * li) / den
    br, bi = b_re.astype(F32), b_im.astype(F32)
    bb_re = z_re[..., None] * br - z_im[..., None] * bi
    bb_im = z_re[..., None] * bi + z_im[..., None] * br
    cr = jnp.swapaxes(c_re.astype(F32), -1, -2)
    ci = jnp.swapaxes(c_im.astype(F32), -1, -2)

    pr, pi = [jnp.ones_like(a_re)], [jnp.zeros_like(a_im)]
    for _ in range(CHUNK):
        nr_, ni_ = _cmul(pr[-1], pi[-1], a_re, a_im)
        pr.append(nr_)
        pi.append(ni_)
    pw_re, pw_im = jnp.stack(pr, 0), jnp.stack(pi, 0)

    ab_re, ab_im = _cmul(pw_re[:CHUNK, ..., None], pw_im[:CHUNK, ..., None], bb_re[None], bb_im[None])
    kern = (jnp.einsum('dgpc,kdgpe->kdgce', cr, ab_re, precision='highest')
            - jnp.einsum('dgpc,kdgpe->kdgce', ci, ab_im, precision='highest'))
    lag_tab = jnp.concatenate([kern[::-1, 0][:-1], (kern[0, 0] + kern[0, 1])[None], kern[1:, 1]], 0)
    lag_tab = jnp.transpose(lag_tab, (1, 2, 0, 3)).reshape(G, C, (2 * CHUNK - 1) * C)
    mt = jnp.stack([lag_tab[:, :, (CHUNK - 1 - t) * C:(CHUNK - 1 - t) * C + CHUNK * C]
                    for t in range(CHUNK)], 1).reshape(SLABS, GROUPS_PER_SLAB, CHUNK * C, CHUNK * C)

    apow = jnp.stack([pw_re, pw_im], 0).reshape(2, CHUNK + 1, 2, SLABS, SLAB_STATE)
    apow = jnp.transpose(apow, (3, 2, 0, 1, 4))
    enc, dect = _s5_gen(_slab_tiles(jnp.swapaxes(bb_re, -1, -2), jnp.swapaxes(bb_im, -1, -2)),
                        _slab_tiles(c_re, c_im), apow)

    s_re, s_im = pr[CHUNK], pi[CHUNK]
    w_re, w_im = _cpowers(s_re, s_im, nq)
    q_re, q_im = _cmul(w_re[nq - 1], w_im[nq - 1], s_re, s_im)
    rows_re, rows_im = [s_re], [s_im]
    for _ in range(SEG_LEVELS):
        rows_re.append(q_re)
        rows_im.append(q_im)
        q_re, q_im = _cmul(q_re, q_im, q_re, q_im)
    c_all = jnp.stack([jnp.concatenate([jnp.stack(rows_re, 0), w_re], 0),
                       jnp.concatenate([jnp.stack(rows_im, 0), w_im], 0)], 0)
    n_rows = c_all.shape[1]
    c_all = c_all.reshape(2, n_rows, 2, SLABS, SLAB_STATE)
    coef = jnp.transpose(c_all, (3, 2, 0, 1, 4))
    return mt.astype(BF16), enc, dect, coef


def _post_kernel(x_ref, ya_ref, ys_ref, g_ref, wglu_ref, bglu_ref, wout_ref, ln2_ref,
                 wup_ref, wdown_ref, o_ref):
    nseg, tq, _ = x_ref.shape
    tm = nseg * tq
    ys = jnp.concatenate(
        [jnp.concatenate([ys_ref[sl, pl.ds(r, tq, stride=nseg), :] for sl in range(SLABS)], axis=1)
         for r in range(nseg)], axis=0)
    y = jax.nn.gelu(ys)
    z = jnp.dot(y.astype(BF16), wglu_ref[...], preferred_element_type=F32) + bglu_ref[...]
    y = y * jax.nn.sigmoid(z)
    g = g_ref[...].reshape(tm, 2 * D_MODEL).astype(F32)
    ya = ya_ref[...].reshape(tm, D_MODEL).astype(F32)
    mixed = jax.nn.sigmoid(g[:, :D_MODEL]) * ya + jax.nn.sigmoid(g[:, D_MODEL:]) * y
    x = x_ref[...].reshape(tm, D_MODEL)
    x = x + jnp.dot(mixed.astype(BF16), wout_ref[...], preferred_element_type=F32)
    ms = jnp.mean(x * x, axis=-1, keepdims=True)
    h = (x * lax.rsqrt(ms + EPS) * ln2_ref[...]).astype(BF16)
    up = jnp.dot(h, wup_ref[...], preferred_element_type=F32)
    act = jnp.square(jnp.maximum(up, 0.0)).astype(BF16)
    out = x + jnp.dot(act, wdown_ref[...], preferred_element_type=F32)
    o_ref[...] = out.reshape(nseg, tq, D_MODEL)


def _post(x2, ya, ys4, g, w_glu, b_glu, w_out, ln2, w_up, w_down, tq):
    b, _, seq, _ = ys4.shape
    per_seq = seq // SUBLANES // tq
    seg = lambda c: pl.BlockSpec((None, SUBLANES, tq, c), lambda i: (i // per_seq, 0, i % per_seq, 0))
    interleaved = pl.BlockSpec((None, SLABS, tq * SUBLANES, LANES),
                               lambda i: (i // per_seq, 0, i % per_seq, 0))
    out = pl.pallas_call(
        _post_kernel,
        grid=(b * per_seq,),
        in_specs=[seg(D_MODEL), seg(D_MODEL), interleaved, seg(2 * D_MODEL),
                  _const_spec((D_MODEL, D_MODEL)), _const_spec((1, D_MODEL)),
                  _const_spec((D_MODEL, D_MODEL)), _const_spec((1, D_MODEL)),
                  _const_spec((D_MODEL, D_FF)), _const_spec((D_FF, D_MODEL))],
        out_specs=seg(D_MODEL),
        out_shape=jax.ShapeDtypeStruct((b, SUBLANES, seq // SUBLANES, D_MODEL), F32),
        compiler_params=pltpu.CompilerParams(dimension_semantics=("parallel",),
                                             vmem_limit_bytes=VMEM_LIMIT),
        name="post",
    )(_seg_view(x2, b, seq), _seg_view(ya, b, seq), ys4, _seg_view(g, b, seq),
      w_glu, b_glu, w_out, ln2, w_up, w_down)
    return out.reshape(b, seq, D_MODEL)


def _t5_bucket(rel):
    nb = N_BUCKETS // 2
    ret = (rel > 0).astype(np.int32) * nb
    n = np.abs(rel)
    max_exact = nb // 2
    n_safe = np.maximum(n, 1).astype(np.float32)
    large = max_exact + (np.log(n_safe / max_exact) / math.log(MAX_DISTANCE / max_exact)
                         * (nb - max_exact)).astype(np.int32)
    large = np.minimum(large, nb - 1)
    return (ret + np.where(n < max_exact, n, large)).astype(np.int32)


def _band_bias(rel_table):
    qi = np.arange(BLOCK)[:, None]
    kj = np.arange(3 * BLOCK)[None, :]
    rel = kj - BLOCK - qi
    onehot = (np.arange(N_BUCKETS)[:, None] == _t5_bucket(rel).reshape(1, -1)).astype(np.float32)
    bias = jnp.dot(rel_table.astype(F32).T * LOG2E, jnp.asarray(onehot), precision='highest')
    bias = bias.reshape(N_HEADS, BLOCK, 3 * BLOCK)
    in_band = np.abs(rel) <= WINDOW
    keep = np.stack([in_band & (kj >= BLOCK), in_band, in_band & (kj < 2 * BLOCK)], 0)
    bias = jnp.where(jnp.asarray(keep)[:, None], bias[None], NEG_INF)
    bias = bias.reshape(3, N_KV_HEADS, 2, 2, BLOCK, 3 * BLOCK)
    return jnp.transpose(bias, (0, 1, 2, 4, 3, 5)).reshape(3, N_KV_HEADS, 2 * BLOCK, 6 * BLOCK)


def _sum_mat():
    first = np.arange(6 * BLOCK)[:, None] < 3 * BLOCK
    low = np.arange(LANES)[None, :] < HEAD_DIM
    return jnp.asarray(first == low, BF16)


def _head_mean_mat(width):
    assert width // HEAD_DIM <= LANES
    sel = np.arange(width)[:, None] // HEAD_DIM == np.arange(LANES)[None, :]
    return jnp.asarray(sel.astype(np.float32) / HEAD_DIM, BF16)


def _layer(x, bias, sink, ln1, w_in, qg, kg, s5w, dsk, w_glu, b_glu, w_out, ln2, w_up, w_down):
    b, s, _ = x.shape
    x2 = x.reshape(b * s, D_MODEL)
    rows_per_seg = s // SUBLANES
    q, kvx, u4, g = _in_proj(x2, ln1, w_in, qg, kg, _head_mean_mat(ATTN_Q), _head_mean_mat(ATTN_KV),
                             b, s, min(TQ_IN, rows_per_seg))
    ya = _attn(q.reshape(b, s, ATTN_Q), kvx.reshape(b, s, 4 * ATTN_KV), bias, _sum_mat(), sink)
    ys4 = _s5(u4, *s5w, dsk)
    return _post(x2, ya.reshape(b * s, ATTN_Q), ys4, g, w_glu, b_glu, w_out, ln2, w_up, w_down,
                 min(TQ_POST, rows_per_seg))


def kernel(x_prompt, x_sample, rel_table, ln1, w_in, q_gain, k_gain, sink, lam_re, lam_im, log_dt,
           b_re, b_im, c_re, c_im, d_skip, w_glu, b_glu, w_out, ln2, w_up, w_down):
    assert ln1.shape[0] == 1, "single layer"
    bias = _band_bias(rel_table)
    outs = []
    s5w_by_nq = {}
    for x in (x_prompt, x_sample):
        seq = x.shape[1]
        nq = seq // (CHUNK * SUBLANES)
        assert nq * CHUNK * SUBLANES == seq and seq % BLOCK == 0
        if nq not in s5w_by_nq:
            s5w_by_nq[nq] = _s5_weights(lam_re[0], lam_im[0], log_dt[0], b_re[0], b_im[0],
                                        c_re[0], c_im[0], nq)
        s5w = s5w_by_nq[nq]
        outs.append(_layer(
            x, bias, sink[0].astype(F32) * LOG2E,
            ln1[0].astype(F32)[None], w_in[0].astype(BF16),
            jnp.tile(q_gain[0].astype(F32), N_HEADS)[None],
            jnp.tile(k_gain[0].astype(F32), N_KV_HEADS)[None],
            s5w, d_skip[0].astype(F32).reshape(SLABS, 1, LANES),
            w_glu[0].astype(BF16), b_glu[0].astype(F32)[None], w_out[0].astype(BF16),
            ln2[0].astype(F32)[None], w_up[0].astype(BF16), w_down[0].astype(BF16)))
    return tuple(outs)
```

```python
import math

import numpy as np
import jax
import jax.numpy as jnp
from jax import lax
from jax.experimental import pallas as pl
from jax.experimental.pallas import tpu as pltpu

D_MODEL = 1024
N_HEADS = 16
N_KV_HEADS = 4
HEAD_DIM = 64
Q_PER_KV = N_HEADS // N_KV_HEADS
WINDOW = 128
BLOCK = 128
N_BUCKETS = 32
MAX_DISTANCE = 128
ATTN_Q = N_HEADS * HEAD_DIM
ATTN_KV = N_KV_HEADS * HEAD_DIM
NEG_INF = -1e30
SSM_WIDTH = D_MODEL
SSM_GROUP = 16
SSM_GROUPS = SSM_WIDTH // SSM_GROUP
SSM_STATE = 64
D_FF = 4 * D_MODEL
EPS = 1e-6
IN_COLS = ATTN_Q + 2 * ATTN_KV + SSM_WIDTH + 2 * D_MODEL

LANES = 128
SUBLANES = 8
SEG_LEVELS = 3
CHUNK = 16
SLABS = SSM_WIDTH // LANES
GROUPS_PER_SLAB = LANES // SSM_GROUP
SLAB_STATE = GROUPS_PER_SLAB * SSM_STATE
VMEM_LIMIT = 56 * 1024 * 1024
TQ_IN = 64
TQ_POST = 64
QBLOCKS = 4
SLABS_PER_STEP = 2

LOG2E = math.log2(math.e)
Q_SCALE = HEAD_DIM ** -0.5 * LOG2E

BF16 = jnp.bfloat16
F32 = jnp.float32


def _const_spec(shape):
    nd = len(shape)
    return pl.BlockSpec(shape, lambda *_: (0,) * nd, pipeline_mode=pl.Buffered(1))


def _in_proj_kernel(x_ref, ln1_ref, w_ref, qg_ref, kg_ref, sq_ref, sk_ref,
                    q_ref, kv_ref, u_ref, g_ref):
    nseg, tq, _ = x_ref.shape
    tm = nseg * tq
    x = x_ref[...].reshape(tm, D_MODEL)
    ms = jnp.mean(x * x, axis=-1, keepdims=True)
    h = (x * lax.rsqrt(ms + EPS) * ln1_ref[...]).astype(BF16)

    lane = lax.broadcasted_iota(jnp.int32, (tm, LANES), 1)

    def head_norm(t, s_ref, gain):
        msq = jnp.dot((t * t).astype(BF16), s_ref[...], preferred_element_type=F32)
        r = lax.rsqrt(msq + EPS)
        spread = [jnp.take_along_axis(r, 2 * j + (lane >= HEAD_DIM).astype(jnp.int32), axis=1)
                  for j in range(t.shape[1] // LANES)]
        return t * jnp.concatenate(spread, axis=1) * gain

    c0 = 0
    q = jnp.dot(h, w_ref[:, c0:c0 + ATTN_Q], preferred_element_type=F32)
    q = (head_norm(q, sq_ref, qg_ref[...]) * Q_SCALE).astype(BF16)
    q_ref[...] = q.reshape(nseg, tq, ATTN_Q)
    c0 += ATTN_Q
    k = jnp.dot(h, w_ref[:, c0:c0 + ATTN_KV], preferred_element_type=F32)
    k = head_norm(k, sk_ref, kg_ref[...])
    c0 += ATTN_KV
    v = jnp.dot(h, w_ref[:, c0:c0 + ATTN_KV], preferred_element_type=F32)
    c0 += ATTN_KV
    kvx = jnp.concatenate([k, pltpu.roll(k, HEAD_DIM, axis=1), v, pltpu.roll(v, HEAD_DIM, axis=1)],
                          axis=1).astype(BF16)
    kv_ref[...] = kvx.reshape(nseg, tq, 4 * ATTN_KV)
    u = jnp.dot(h, w_ref[:, c0:c0 + SSM_WIDTH], preferred_element_type=F32)
    for sl in range(SLABS):
        for r in range(nseg):
            u_ref[sl, pl.ds(r, tq, stride=nseg), :] = u[r * tq:(r + 1) * tq, sl * LANES:(sl + 1) * LANES]
    c0 += SSM_WIDTH
    g = jnp.dot(h, w_ref[:, c0:c0 + 2 * D_MODEL], preferred_element_type=F32).astype(BF16)
    g_ref[...] = g.reshape(nseg, tq, 2 * D_MODEL)


def _seg_view(a, b, seq):
    return a.reshape(b, SUBLANES, seq // SUBLANES, a.shape[-1])


def _in_proj(x2, ln1, w_in, qg, kg, sq, sk, b, seq, tq):
    n = x2.shape[0]
    per_seq = seq // SUBLANES // tq
    seg = lambda c: pl.BlockSpec((None, SUBLANES, tq, c), lambda i: (i // per_seq, 0, i % per_seq, 0))
    interleaved = pl.BlockSpec((None, SLABS, tq * SUBLANES, LANES),
                               lambda i: (i // per_seq, 0, i % per_seq, 0))
    seg_shape = lambda c, dt: jax.ShapeDtypeStruct((b, SUBLANES, seq // SUBLANES, c), dt)
    q, kvx, u, g = pl.pallas_call(
        _in_proj_kernel,
        grid=(b * per_seq,),
        in_specs=[seg(D_MODEL), _const_spec((1, D_MODEL)), _const_spec((D_MODEL, IN_COLS)),
                  _const_spec((1, ATTN_Q)), _const_spec((1, ATTN_KV)),
                  _const_spec((ATTN_Q, LANES)), _const_spec((ATTN_KV, LANES))],
        out_specs=[seg(ATTN_Q), seg(4 * ATTN_KV), interleaved, seg(2 * D_MODEL)],
        out_shape=[seg_shape(ATTN_Q, BF16), seg_shape(4 * ATTN_KV, BF16),
                   jax.ShapeDtypeStruct((b, SLABS, seq, LANES), F32), seg_shape(2 * D_MODEL, BF16)],
        compiler_params=pltpu.CompilerParams(dimension_semantics=("parallel",),
                                             vmem_limit_bytes=VMEM_LIMIT),
        name="in_proj",
    )(_seg_view(x2, b, seq), ln1, w_in, qg, kg, sq, sk)
    return q.reshape(n, ATTN_Q), kvx.reshape(n, 4 * ATTN_KV), u, g.reshape(n, 2 * D_MODEL)


_ROLLED_TILE = (0, 1, 1, 0)


def _attn_kernel(sink_ref, q_ref, kvp_ref, kvc_ref, kvn_ref, bias_ref, ones_ref, o_ref):
    n = pl.program_id(1)
    last = pl.num_programs(1) - 1
    kvx = jnp.concatenate([kvp_ref[0], kvc_ref[0], kvn_ref[0]], axis=0)
    low = lax.broadcasted_iota(jnp.int32, (1, LANES), 1) < HEAD_DIM
    zero = jnp.zeros((), BF16)

    def halves(base, kv):
        plain = kvx[:, base + (kv // 2) * LANES:base + (kv // 2 + 1) * LANES]
        r0 = base + ATTN_KV + _ROLLED_TILE[kv] * LANES
        rolled = kvx[:, r0:r0 + LANES]
        lo_src, hi_src = (plain, rolled) if kv % 2 == 0 else (rolled, plain)
        return jnp.where(low, lo_src, zero), jnp.where(low, zero, hi_src)

    top = lax.broadcasted_iota(jnp.int32, (2 * BLOCK, 1), 0) < BLOCK
    outs = [[] for _ in range(QBLOCKS)]
    for kv in range(N_KV_HEADS):
        ke, ko = halves(0, kv)
        ve, vo = halves(2 * ATTN_KV, kv)
        sink_e = jnp.where(top, sink_ref[4 * kv], sink_ref[4 * kv + 2])
        sink_o = jnp.where(top, sink_ref[4 * kv + 1], sink_ref[4 * kv + 3])
        for qb in range(QBLOCKS):
            rows = slice(qb * BLOCK, (qb + 3) * BLOCK)
            keys = jnp.concatenate([ke[rows], ko[rows]], axis=0)
            vals = jnp.concatenate([jnp.concatenate([ve[rows], vo[rows]], axis=0), ones_ref[...]],
                                   axis=1)
            if qb == 0:
                variant = jnp.where(n == 0, 0, 1)
            elif qb == QBLOCKS - 1:
                variant = jnp.where(n == last, 2, 1)
            else:
                variant = 1
            qrows = slice(qb * BLOCK, (qb + 1) * BLOCK)
            q2 = jnp.concatenate([q_ref[0, qrows, (2 * kv) * LANES:(2 * kv + 1) * LANES],
                                  q_ref[0, qrows, (2 * kv + 1) * LANES:(2 * kv + 2) * LANES]], axis=0)
            s = lax.dot_general(q2, keys, (((1,), (1,)), ((), ())), preferred_element_type=F32)
            s = s + bias_ref[variant, kv]
            se, so = s[:, :3 * BLOCK], s[:, 3 * BLOCK:]
            me = jnp.maximum(jnp.max(se, axis=-1, keepdims=True), sink_e)
            mo = jnp.maximum(jnp.max(so, axis=-1, keepdims=True), sink_o)
            p = jnp.concatenate([jnp.exp2(se - me), jnp.exp2(so - mo)], axis=1).astype(BF16)
            od = jnp.dot(p, vals, preferred_element_type=F32)
            den = od[:, LANES:] + jnp.where(low, jnp.exp2(sink_e - me), jnp.exp2(sink_o - mo))
            o = od[:, :LANES] / den
            outs[qb] += [o[:BLOCK], o[BLOCK:]]
    for qb in range(QBLOCKS):
        o_ref[0, qb * BLOCK:(qb + 1) * BLOCK, :] = jnp.concatenate(outs[qb], axis=-1).astype(BF16)


def _attn(q, kvx, bias, ones, sink):
    b, s, _ = q.shape
    nb = s // BLOCK
    assert nb % QBLOCKS == 0 and QBLOCKS >= 2
    steps = nb // QBLOCKS
    edge_spec = lambda f: pl.BlockSpec((1, BLOCK, 4 * ATTN_KV), f)
    wide = lambda c: pl.BlockSpec((1, QBLOCKS * BLOCK, c), lambda bi, n, *_: (bi, n, 0))
    prev = lambda bi, n, *_: (bi, jnp.maximum(n * QBLOCKS - 1, 0), 0)
    nxt = lambda bi, n, *_: (bi, jnp.minimum((n + 1) * QBLOCKS, nb - 1), 0)
    grid_spec = pltpu.PrefetchScalarGridSpec(
        num_scalar_prefetch=1,
        grid=(b, steps),
        in_specs=[wide(ATTN_Q), edge_spec(prev), wide(4 * ATTN_KV), edge_spec(nxt),
                  _const_spec(bias.shape), _const_spec(ones.shape)],
        out_specs=wide(ATTN_Q),
    )
    return pl.pallas_call(
        _attn_kernel,
        grid_spec=grid_spec,
        out_shape=jax.ShapeDtypeStruct((b, s, ATTN_Q), BF16),
        compiler_params=pltpu.CompilerParams(dimension_semantics=("parallel", "arbitrary"),
                                             vmem_limit_bytes=VMEM_LIMIT),
        name="attn",
    )(sink, q, kvx, kvx, kvx, bias, ones)


def _s5_slab(u_ref, mt_ref, enc_ref, dect_ref, coef_ref, dsk_ref, y_ref):
    nq = u_ref.shape[0] // (CHUNK * SUBLANES)
    tile_rows = lambda x, q: x[q * SUBLANES:(q + 1) * SUBLANES]
    tile = lambda q, s: pl.ds((q * CHUNK + s) * SUBLANES, SUBLANES)

    xs = [jnp.concatenate([u_ref[tile(q, s), :] for q in range(nq)], axis=0) for s in range(CHUNK)]
    xst = [x.astype(BF16).T for x in xs]

    yts, pair_states = [], []
    for g in range(GROUPS_PER_SLAB):
        ugt = jnp.concatenate([xt[g * SSM_GROUP:(g + 1) * SSM_GROUP] for xt in xst], axis=0)
        ug = ugt.T
        yts.append(jnp.dot(mt_ref[g], ugt, preferred_element_type=F32))
        part = jnp.dot(ug, enc_ref[g], preferred_element_type=F32)
        if g % 2 == 0:
            pair_states.append(part)
        else:
            pair_states[-1] = pair_states[-1] + part

    def state(d, ri):
        c0 = (2 * d + ri) * LANES
        return jnp.concatenate([p[:, c0:c0 + LANES] for p in pair_states], axis=1)

    def coef(d, row):
        return coef_ref[d, 0, row:row + 1, :], coef_ref[d, 1, row:row + 1, :]

    seg = lax.broadcasted_iota(jnp.int32, (SUBLANES, 1), 0)

    def seg_shift(x, sh, forward):
        if forward:
            return jnp.where(seg >= sh, pltpu.roll(x, sh, axis=0), 0.0)
        return jnp.where(seg < SUBLANES - sh, pltpu.roll(x, SUBLANES - sh, axis=0), 0.0)

    carried = []
    for d in range(2):
        forward = d == 0
        s_re, s_im = state(d, 0), state(d, 1)
        a_re, a_im = coef(d, 0)
        order = list(range(nq)) if forward else list(range(nq - 1, -1, -1))
        t_re, t_im = {}, {}
        for n, q in enumerate(order):
            cur_re, cur_im = tile_rows(s_re, q), tile_rows(s_im, q)
            if n:
                p_re, p_im = t_re[order[n - 1]], t_im[order[n - 1]]
                cur_re = cur_re + (a_re * p_re - a_im * p_im)
                cur_im = cur_im + (a_re * p_im + a_im * p_re)
            t_re[q], t_im[q] = cur_re, cur_im
        e_re, e_im = t_re[order[-1]], t_im[order[-1]]
        for lv in range(SEG_LEVELS):
            b_re, b_im = coef(d, 1 + lv)
            h_re, h_im = seg_shift(e_re, 1 << lv, forward), seg_shift(e_im, 1 << lv, forward)
            e_re, e_im = e_re + (b_re * h_re - b_im * h_im), e_im + (b_re * h_im + b_im * h_re)
        c_re, c_im = seg_shift(e_re, 1, forward), seg_shift(e_im, 1, forward)
        x_re, x_im = [], []
        for q in range(nq):
            n = order.index(q)
            w_re, w_im = coef(d, 1 + SEG_LEVELS + n)
            r_re, r_im = w_re * c_re - w_im * c_im, w_re * c_im + w_im * c_re
            if n:
                r_re, r_im = r_re + t_re[order[n - 1]], r_im + t_im[order[n - 1]]
            x_re.append(r_re)
            x_im.append(r_im)
        carried += [jnp.concatenate(x_re, axis=0), jnp.concatenate(x_im, axis=0)]

    for g in range(GROUPS_PER_SLAB):
        c0 = (g // 2) * LANES
        xpair = jnp.concatenate([c[:, c0:c0 + LANES] for c in carried], axis=1).astype(BF16)
        yts[g] = yts[g] + lax.dot_general(dect_ref[g], xpair, (((1,), (1,)), ((), ())),
                                          preferred_element_type=F32)
    dsk = dsk_ref[...]
    for t in range(CHUNK):
        yt = jnp.concatenate([y[t * SSM_GROUP:(t + 1) * SSM_GROUP] for y in yts], axis=0)
        y = yt.T + dsk * xs[t]
        for q in range(nq):
            y_ref[tile(q, t), :] = tile_rows(y, q)


def _s5_kernel(u_ref, mt_ref, enc_ref, dect_ref, coef_ref, dsk_ref, y_ref):
    for i in range(SLABS_PER_STEP):
        _s5_slab(u_ref.at[i], mt_ref.at[i], enc_ref.at[i], dect_ref.at[i], coef_ref.at[i],
                 dsk_ref.at[i], y_ref.at[i])


def _s5(u4, mt, enc, dect, coef, dsk):
    b, _, seq, _ = u4.shape
    per_slab = lambda a: pl.BlockSpec((SLABS_PER_STEP,) + a.shape[1:],
                                      lambda sl, bi: (sl,) + (0,) * (a.ndim - 1),
                                      pipeline_mode=pl.Buffered(1))
    io = pl.BlockSpec((None, SLABS_PER_STEP, seq, LANES), lambda sl, bi: (bi, sl, 0, 0))
    return pl.pallas_call(
        _s5_kernel,
        grid=(SLABS // SLABS_PER_STEP, b),
        in_specs=[io, per_slab(mt), per_slab(enc), per_slab(dect), per_slab(coef), per_slab(dsk)],
        out_specs=io,
        out_shape=jax.ShapeDtypeStruct(u4.shape, F32),
        compiler_params=pltpu.CompilerParams(dimension_semantics=("parallel", "parallel"),
                                             vmem_limit_bytes=VMEM_LIMIT),
        name="s5",
    )(u4, mt, enc, dect, coef, dsk)


def _cmul(ar, ai, br, bi):
    return ar * br - ai * bi, ar * bi + ai * br


def _cpowers(a_re, a_im, n):
    p_re, p_im = jnp.ones_like(a_re)[None], jnp.zeros_like(a_im)[None]
    s_re, s_im = a_re, a_im
    while p_re.shape[0] < n:
        h_re, h_im = _cmul(p_re, p_im, s_re[None], s_im[None])
        p_re, p_im = jnp.concatenate([p_re, h_re], 0), jnp.concatenate([p_im, h_im], 0)
        s_re, s_im = _cmul(s_re, s_im, s_re, s_im)
    return p_re[:n], p_im[:n]


def _s5_gen_kernel(bm_ref, cm_ref, apow_ref, enc_ref, dect_ref):
    row_group = lax.broadcasted_iota(jnp.int32, (LANES, SLAB_STATE), 0) // SSM_GROUP
    col_group = lax.broadcasted_iota(jnp.int32, (LANES, SLAB_STATE), 1) // SSM_STATE

    def block_diag(ref, d, ri):
        wide = jnp.concatenate([ref[0, d, ri]] * (SLAB_STATE // LANES), axis=1)
        return jnp.where(row_group == col_group, wide, 0.0)

    bm = [[block_diag(bm_ref, d, ri) for ri in range(2)] for d in range(2)]
    cm = [[block_diag(cm_ref, d, ri) for ri in range(2)] for d in range(2)]

    def put(ref, d, ri, step, tile):
        tile = tile.astype(BF16)
        c0 = (2 * d + ri) * LANES
        for g in range(GROUPS_PER_SLAB):
            p0 = (g // 2) * LANES
            ref[0, g, step * SSM_GROUP:(step + 1) * SSM_GROUP, c0:c0 + LANES] = (
                tile[g * SSM_GROUP:(g + 1) * SSM_GROUP, p0:p0 + LANES])

    for d in range(2):
        for k in range(CHUNK + 1):
            a_re, a_im = apow_ref[0, d, 0, k:k + 1, :], apow_ref[0, d, 1, k:k + 1, :]
            if k < CHUNK:
                s = CHUNK - 1 - k if d == 0 else k
                e_re, e_im = _cmul(bm[d][0], bm[d][1], a_re, a_im)
                put(enc_ref, d, 0, s, e_re)
                put(enc_ref, d, 1, s, e_im)
            if k >= 1:
                t = k - 1 if d == 0 else CHUNK - k
                f_re, f_im = _cmul(cm[d][0], cm[d][1], a_re, a_im)
                put(dect_ref, d, 0, t, f_re)
                put(dect_ref, d, 1, t, -f_im)


def _s5_gen(bm, cm, apow):
    small = lambda a: pl.BlockSpec((1,) + a.shape[1:], lambda sl: (sl,) + (0,) * (a.ndim - 1))
    shape = (SLABS, GROUPS_PER_SLAB, CHUNK * SSM_GROUP, 4 * LANES)
    out = pl.BlockSpec((1,) + shape[1:], lambda sl: (sl, 0, 0, 0))
    return pl.pallas_call(
        _s5_gen_kernel,
        grid=(SLABS,),
        in_specs=[small(bm), small(cm), small(apow)],
        out_specs=[out, out],
        out_shape=[jax.ShapeDtypeStruct(shape, BF16), jax.ShapeDtypeStruct(shape, BF16)],
        compiler_params=pltpu.CompilerParams(dimension_semantics=("parallel",),
                                             vmem_limit_bytes=VMEM_LIMIT),
        name="s5_gen",
    )(bm, cm, apow)


def _slab_tiles(t_re, t_im):
    t = jnp.stack([t_re, t_im], 1).astype(F32)
    t = t.reshape(2, 2, SLABS, LANES, SSM_STATE)
    t = jnp.concatenate([t, t], axis=-1)
    return jnp.transpose(t, (2, 0, 1, 3, 4))


def _s5_weights(lam_re, lam_im, log_dt, b_re, b_im, c_re, c_im, nq):
    G, C = SSM_GROUPS, SSM_GROUP
    dt = jnp.exp(log_dt.astype(F32))[..., None]
    lr, li = lam_re.astype(F32), lam_im.astype(F32)
    mag = jnp.exp(lr * dt)
    ang = li * dt
    a_re, a_im = mag * jnp.cos(ang), mag * jnp.sin(ang)
    den = lr * lr + li * li
    nr, ni = a_re - 1.0, a_im
    z_re = (nr * lr + ni * li) / den
    z_im = (ni * lr - nr * li) / den
    br, bi = b_re.astype(F32), b_im.astype(F32)
    bb_re = z_re[..., None] * br - z_im[..., None] * bi
    bb_im = z_re[..., None] * bi + z_im[..., None] * br
    cr = jnp.swapaxes(c_re.astype(F32), -1, -2)
    ci = jnp.swapaxes(c_im.astype(F32), -1, -2)

    pw_re, pw_im = _cpowers(a_re, a_im, CHUNK + 1)

    ab_re, ab_im = _cmul(pw_re[:CHUNK, ..., None], pw_im[:CHUNK, ..., None], bb_re[None], bb_im[None])
    kern = jnp.einsum('dgpc,kdgpe->kdgce', jnp.concatenate([cr, -ci], 2),
                      jnp.concatenate([ab_re, ab_im], 3), precision='highest')
    lag_tab = jnp.concatenate([kern[::-1, 0][:-1], (kern[0, 0] + kern[0, 1])[None], kern[1:, 1]], 0)
    lag_tab = jnp.transpose(lag_tab, (1, 2, 0, 3)).reshape(G, C, (2 * CHUNK - 1) * C)
    mt = jnp.stack([lag_tab[:, :, (CHUNK - 1 - t) * C:(CHUNK - 1 - t) * C + CHUNK * C]
                    for t in range(CHUNK)], 1).reshape(SLABS, GROUPS_PER_SLAB, CHUNK * C, CHUNK * C)

    apow = jnp.stack([pw_re, pw_im], 0).reshape(2, CHUNK + 1, 2, SLABS, SLAB_STATE)
    apow = jnp.transpose(apow, (3, 2, 0, 1, 4))
    enc, dect = _s5_gen(_slab_tiles(jnp.swapaxes(bb_re, -1, -2), jnp.swapaxes(bb_im, -1, -2)),
                        _slab_tiles(c_re, c_im), apow)

    s_re, s_im = pw_re[CHUNK], pw_im[CHUNK]
    w_re, w_im = _cpowers(s_re, s_im, nq)
    q_re, q_im = _cmul(w_re[nq - 1], w_im[nq - 1], s_re, s_im)
    rows_re, rows_im = [s_re], [s_im]
    for _ in range(SEG_LEVELS):
        rows_re.append(q_re)
        rows_im.append(q_im)
        q_re, q_im = _cmul(q_re, q_im, q_re, q_im)
    c_all = jnp.stack([jnp.concatenate([jnp.stack(rows_re, 0), w_re], 0),
                       jnp.concatenate([jnp.stack(rows_im, 0), w_im], 0)], 0)
    n_rows = c_all.shape[1]
    c_all = c_all.reshape(2, n_rows, 2, SLABS, SLAB_STATE)
    coef = jnp.transpose(c_all, (3, 2, 0, 1, 4))
    return mt.astype(BF16), enc, dect, coef


def _post_kernel(x_ref, ya_ref, ys_ref, g_ref, wglu_ref, bglu_ref, wout_ref, ln2_ref,
                 wup_ref, wdown_ref, o_ref):
    nseg, tq, _ = x_ref.shape
    tm = nseg * tq
    ys = jnp.concatenate(
        [jnp.concatenate([ys_ref[sl, pl.ds(r, tq, stride=nseg), :] for sl in range(SLABS)], axis=1)
         for r in range(nseg)], axis=0)
    y = jax.nn.gelu(ys)
    z = jnp.dot(y.astype(BF16), wglu_ref[...], preferred_element_type=F32) + bglu_ref[...]
    y = y * jax.nn.sigmoid(z)
    g = g_ref[...].reshape(tm, 2 * D_MODEL).astype(F32)
    ya = ya_ref[...].reshape(tm, D_MODEL).astype(F32)
    mixed = jax.nn.sigmoid(g[:, :D_MODEL]) * ya + jax.nn.sigmoid(g[:, D_MODEL:]) * y
    x = x_ref[...].reshape(tm, D_MODEL)
    x = x + jnp.dot(mixed.astype(BF16), wout_ref[...], preferred_element_type=F32)
    ms = jnp.mean(x * x, axis=-1, keepdims=True)
    h = (x * lax.rsqrt(ms + EPS) * ln2_ref[...]).astype(BF16)
    up = jnp.dot(h, wup_ref[...], preferred_element_type=F32)
    act = jnp.square(jnp.maximum(up, 0.0)).astype(BF16)
    out = x + jnp.dot(act, wdown_ref[...], preferred_element_type=F32)
    o_ref[...] = out.reshape(nseg, tq, D_MODEL)


def _post(x2, ya, ys4, g, w_glu, b_glu, w_out, ln2, w_up, w_down, tq):
    b, _, seq, _ = ys4.shape
    per_seq = seq // SUBLANES // tq
    seg = lambda c: pl.BlockSpec((None, SUBLANES, tq, c), lambda i: (i // per_seq, 0, i % per_seq, 0))
    interleaved = pl.BlockSpec((None, SLABS, tq * SUBLANES, LANES),
                               lambda i: (i // per_seq, 0, i % per_seq, 0))
    out = pl.pallas_call(
        _post_kernel,
        grid=(b * per_seq,),
        in_specs=[seg(D_MODEL), seg(D_MODEL), interleaved, seg(2 * D_MODEL),
                  _const_spec((D_MODEL, D_MODEL)), _const_spec((1, D_MODEL)),
                  _const_spec((D_MODEL, D_MODEL)), _const_spec((1, D_MODEL)),
                  _const_spec((D_MODEL, D_FF)), _const_spec((D_FF, D_MODEL))],
        out_specs=seg(D_MODEL),
        out_shape=jax.ShapeDtypeStruct((b, SUBLANES, seq // SUBLANES, D_MODEL), F32),
        compiler_params=pltpu.CompilerParams(dimension_semantics=("parallel",),
                                             vmem_limit_bytes=VMEM_LIMIT),
        name="post",
    )(_seg_view(x2, b, seq), _seg_view(ya, b, seq), ys4, _seg_view(g, b, seq),
      w_glu, b_glu, w_out, ln2, w_up, w_down)
    return out.reshape(b, seq, D_MODEL)


def _t5_bucket(rel):
    nb = N_BUCKETS // 2
    ret = (rel > 0).astype(np.int32) * nb
    n = np.abs(rel)
    max_exact = nb // 2
    n_safe = np.maximum(n, 1).astype(np.float32)
    large = max_exact + (np.log(n_safe / max_exact) / math.log(MAX_DISTANCE / max_exact)
                         * (nb - max_exact)).astype(np.int32)
    large = np.minimum(large, nb - 1)
    return (ret + np.where(n < max_exact, n, large)).astype(np.int32)


def _band_bias(rel_table):
    qi = np.arange(BLOCK)[:, None]
    kj = np.arange(3 * BLOCK)[None, :]
    rel = kj - BLOCK - qi
    onehot = (np.arange(N_BUCKETS)[:, None] == _t5_bucket(rel).reshape(1, -1)).astype(np.float32)
    bias = jnp.dot(rel_table.astype(F32).T * LOG2E, jnp.asarray(onehot), precision='highest')
    bias = bias.reshape(N_HEADS, BLOCK, 3 * BLOCK)
    in_band = np.abs(rel) <= WINDOW
    keep = np.stack([in_band & (kj >= BLOCK), in_band, in_band & (kj < 2 * BLOCK)], 0)
    bias = jnp.where(jnp.asarray(keep)[:, None], bias[None], NEG_INF)
    bias = bias.reshape(3, N_KV_HEADS, 2, 2, BLOCK, 3 * BLOCK)
    return jnp.transpose(bias, (0, 1, 2, 4, 3, 5)).reshape(3, N_KV_HEADS, 2 * BLOCK, 6 * BLOCK)


def _sum_mat():
    first = np.arange(6 * BLOCK)[:, None] < 3 * BLOCK
    low = np.arange(LANES)[None, :] < HEAD_DIM
    return jnp.asarray(first == low, BF16)


def _head_mean_mat(width):
    assert width // HEAD_DIM <= LANES
    sel = np.arange(width)[:, None] // HEAD_DIM == np.arange(LANES)[None, :]
    return jnp.asarray(sel.astype(np.float32) / HEAD_DIM, BF16)


def _layer(x, bias, sink, ln1, w_in, qg, kg, s5w, dsk, w_glu, b_glu, w_out, ln2, w_up, w_down):
    b, s, _ = x.shape
    x2 = x.reshape(b * s, D_MODEL)
    rows_per_seg = s // SUBLANES
    q, kvx, u4, g = _in_proj(x2, ln1, w_in, qg, kg, _head_mean_mat(ATTN_Q), _head_mean_mat(ATTN_KV),
                             b, s, min(TQ_IN, rows_per_seg))
    ya = _attn(q.reshape(b, s, ATTN_Q), kvx.reshape(b, s, 4 * ATTN_KV), bias, _sum_mat(), sink)
    ys4 = _s5(u4, *s5w, dsk)
    return _post(x2, ya.reshape(b * s, ATTN_Q), ys4, g, w_glu, b_glu, w_out, ln2, w_up, w_down,
                 min(TQ_POST, rows_per_seg))


def kernel(x_prompt, x_sample, rel_table, ln1, w_in, q_gain, k_gain, sink, lam_re, lam_im, log_dt,
           b_re, b_im, c_re, c_im, d_skip, w_glu, b_glu, w_out, ln2, w_up, w_down):
    assert ln1.shape[0] == 1, "single layer"
    bias = _band_bias(rel_table)
    outs = []
    s5w_by_nq = {}
    for x in (x_prompt, x_sample):
        seq = x.shape[1]
        nq = seq // (CHUNK * SUBLANES)
        assert nq * CHUNK * SUBLANES == seq and seq % BLOCK == 0
        if nq not in s5w_by_nq:
            s5w_by_nq[nq] = _s5_weights(lam_re[0], lam_im[0], log_dt[0], b_re[0], b_im[0],
                                        c_re[0], c_im[0], nq)
        s5w = s5w_by_nq[nq]
        outs.append(_layer(
            x, bias, sink[0].astype(F32) * LOG2E,
            ln1[0].astype(F32)[None], w_in[0].astype(BF16),
            jnp.tile(q_gain[0].astype(F32), N_HEADS)[None],
            jnp.tile(k_gain[0].astype(F32), N_KV_HEADS)[None],
            s5w, d_skip[0].astype(F32).reshape(SLABS, 1, LANES),
            w_glu[0].astype(BF16), b_glu[0].astype(F32)[None], w_out[0].astype(BF16),
            ln2[0].astype(F32)[None], w_up[0].astype(BF16), w_down[0].astype(BF16)))
    return tuple(outs)
```

```python
import math

import numpy as np
import jax
import jax.numpy as jnp
from jax import lax
from jax.experimental import pallas as pl
from jax.experimental.pallas import tpu as pltpu

D_MODEL = 1024
N_HEADS = 16
N_KV_HEADS = 4
HEAD_DIM = 64
Q_PER_KV = N_HEADS // N_KV_HEADS
WINDOW = 128
BLOCK = 128
N_BUCKETS = 32
MAX_DISTANCE = 128
ATTN_Q = N_HEADS * HEAD_DIM
ATTN_KV = N_KV_HEADS * HEAD_DIM
NEG_INF = -1e30
SSM_WIDTH = D_MODEL
SSM_GROUP = 16
SSM_GROUPS = SSM_WIDTH // SSM_GROUP
SSM_STATE = 64
D_FF = 4 * D_MODEL
EPS = 1e-6
IN_COLS = ATTN_Q + 2 * ATTN_KV + SSM_WIDTH + 2 * D_MODEL

LANES = 128
SUBLANES = 8
SEG_LEVELS = 3
CHUNK = 16
SLABS = SSM_WIDTH // LANES
GROUPS_PER_SLAB = LANES // SSM_GROUP
SLAB_STATE = GROUPS_PER_SLAB * SSM_STATE
VMEM_LIMIT = 56 * 1024 * 1024
TQ_IN = 64
TQ_POST = 64
QBLOCKS = 4
SLABS_PER_STEP = 2

LOG2E = math.log2(math.e)
Q_SCALE = HEAD_DIM ** -0.5 * LOG2E

BF16 = jnp.bfloat16
F32 = jnp.float32


def _const_spec(shape):
    nd = len(shape)
    return pl.BlockSpec(shape, lambda *_: (0,) * nd, pipeline_mode=pl.Buffered(1))


def _in_proj_kernel(x_ref, ln1_ref, w_ref, qg_ref, kg_ref, q_ref, kv_ref, u_ref, g_ref):
    nseg, tq, _ = x_ref.shape
    tm = nseg * tq
    x = x_ref[...].reshape(tm, D_MODEL)
    ms = jnp.mean(x * x, axis=-1, keepdims=True)
    rinv = lax.rsqrt(ms + EPS)
    h = (x * ln1_ref[...]).astype(BF16)
    proj = lambda c0, width: jnp.dot(h, w_ref[:, c0:c0 + width], preferred_element_type=F32) * rinv

    low = lax.broadcasted_iota(jnp.int32, (1, LANES), 1) < HEAD_DIM

    def head_norm(t, gain):
        cols = []
        for j in range(t.shape[1] // LANES):
            blk = t[:, j * LANES:(j + 1) * LANES]
            sq = blk * blk
            m_lo = jnp.sum(jnp.where(low, sq, 0.0), axis=-1, keepdims=True) * (1.0 / HEAD_DIM)
            m_hi = jnp.sum(jnp.where(low, 0.0, sq), axis=-1, keepdims=True) * (1.0 / HEAD_DIM)
            cols.append(blk * jnp.where(low, lax.rsqrt(m_lo + EPS), lax.rsqrt(m_hi + EPS)))
        return jnp.concatenate(cols, axis=1) * gain

    c0 = 0
    q = (head_norm(proj(c0, ATTN_Q), qg_ref[...]) * Q_SCALE).astype(BF16)
    q_ref[...] = q.reshape(nseg, tq, ATTN_Q)
    c0 += ATTN_Q
    k = head_norm(proj(c0, ATTN_KV), kg_ref[...])
    c0 += ATTN_KV
    v = proj(c0, ATTN_KV)
    c0 += ATTN_KV
    kvx = jnp.concatenate([k, pltpu.roll(k, HEAD_DIM, axis=1), v, pltpu.roll(v, HEAD_DIM, axis=1)],
                          axis=1).astype(BF16)
    kv_ref[...] = kvx.reshape(nseg, tq, 4 * ATTN_KV)
    u = proj(c0, SSM_WIDTH)
    for sl in range(SLABS):
        for r in range(nseg):
            u_ref[sl, pl.ds(r, tq, stride=nseg), :] = u[r * tq:(r + 1) * tq, sl * LANES:(sl + 1) * LANES]
    c0 += SSM_WIDTH
    g = proj(c0, 2 * D_MODEL).astype(BF16)
    g_ref[...] = g.reshape(nseg, tq, 2 * D_MODEL)


def _seg_view(a, b, seq):
    return a.reshape(b, SUBLANES, seq // SUBLANES, a.shape[-1])


def _in_proj(x2, ln1, w_in, qg, kg, b, seq, tq):
    n = x2.shape[0]
    per_seq = seq // SUBLANES // tq
    seg = lambda c: pl.BlockSpec((None, SUBLANES, tq, c), lambda i: (i // per_seq, 0, i % per_seq, 0))
    interleaved = pl.BlockSpec((None, SLABS, tq * SUBLANES, LANES),
                               lambda i: (i // per_seq, 0, i % per_seq, 0))
    seg_shape = lambda c, dt: jax.ShapeDtypeStruct((b, SUBLANES, seq // SUBLANES, c), dt)
    q, kvx, u, g = pl.pallas_call(
        _in_proj_kernel,
        grid=(b * per_seq,),
        in_specs=[seg(D_MODEL), _const_spec((1, D_MODEL)), _const_spec((D_MODEL, IN_COLS)),
                  _const_spec((1, ATTN_Q)), _const_spec((1, ATTN_KV))],
        out_specs=[seg(ATTN_Q), seg(4 * ATTN_KV), interleaved, seg(2 * D_MODEL)],
        out_shape=[seg_shape(ATTN_Q, BF16), seg_shape(4 * ATTN_KV, BF16),
                   jax.ShapeDtypeStruct((b, SLABS, seq, LANES), F32), seg_shape(2 * D_MODEL, BF16)],
        compiler_params=pltpu.CompilerParams(dimension_semantics=("parallel",),
                                             vmem_limit_bytes=VMEM_LIMIT),
        name="in_proj",
    )(_seg_view(x2, b, seq), ln1, w_in, qg, kg)
    return q.reshape(n, ATTN_Q), kvx.reshape(n, 4 * ATTN_KV), u, g.reshape(n, 2 * D_MODEL)


_ROLLED_TILE = (0, 1, 1, 0)


def _attn_kernel(sink_ref, q_ref, kvp_ref, kvc_ref, kvn_ref, bias_ref, ones_ref, o_ref):
    n = pl.program_id(1)
    last = pl.num_programs(1) - 1
    kvx = jnp.concatenate([kvp_ref[0], kvc_ref[0], kvn_ref[0]], axis=0)
    low = lax.broadcasted_iota(jnp.int32, (1, LANES), 1) < HEAD_DIM
    zero = jnp.zeros((), BF16)

    def halves(base, kv):
        plain = kvx[:, base + (kv // 2) * LANES:base + (kv // 2 + 1) * LANES]
        r0 = base + ATTN_KV + _ROLLED_TILE[kv] * LANES
        rolled = kvx[:, r0:r0 + LANES]
        lo_src, hi_src = (plain, rolled) if kv % 2 == 0 else (rolled, plain)
        return jnp.where(low, lo_src, zero), jnp.where(low, zero, hi_src)

    top = lax.broadcasted_iota(jnp.int32, (2 * BLOCK, 1), 0) < BLOCK
    outs = [[] for _ in range(QBLOCKS)]
    for kv in range(N_KV_HEADS):
        ke, ko = halves(0, kv)
        ve, vo = halves(2 * ATTN_KV, kv)
        sink_e = jnp.where(top, sink_ref[4 * kv], sink_ref[4 * kv + 2])
        sink_o = jnp.where(top, sink_ref[4 * kv + 1], sink_ref[4 * kv + 3])
        for qb in range(QBLOCKS):
            rows = slice(qb * BLOCK, (qb + 3) * BLOCK)
            keys = jnp.concatenate([ke[rows], ko[rows]], axis=0)
            vals = jnp.concatenate([jnp.concatenate([ve[rows], vo[rows]], axis=0), ones_ref[...]],
                                   axis=1)
            if qb == 0:
                variant = jnp.where(n == 0, 0, 1)
            elif qb == QBLOCKS - 1:
                variant = jnp.where(n == last, 2, 1)
            else:
                variant = 1
            qrows = slice(qb * BLOCK, (qb + 1) * BLOCK)
            q2 = jnp.concatenate([q_ref[0, qrows, (2 * kv) * LANES:(2 * kv + 1) * LANES],
                                  q_ref[0, qrows, (2 * kv + 1) * LANES:(2 * kv + 2) * LANES]], axis=0)
            s = lax.dot_general(q2, keys, (((1,), (1,)), ((), ())), preferred_element_type=F32)
            s = s + bias_ref[variant, kv]
            se, so = s[:, :3 * BLOCK], s[:, 3 * BLOCK:]
            me = jnp.maximum(jnp.max(se, axis=-1, keepdims=True), sink_e)
            mo = jnp.maximum(jnp.max(so, axis=-1, keepdims=True), sink_o)
            p = jnp.concatenate([jnp.exp2(se - me), jnp.exp2(so - mo)], axis=1).astype(BF16)
            od = jnp.dot(p, vals, preferred_element_type=F32)
            den = od[:, LANES:] + jnp.where(low, jnp.exp2(sink_e - me), jnp.exp2(sink_o - mo))
            o = od[:, :LANES] / den
            outs[qb] += [o[:BLOCK], o[BLOCK:]]
    for qb in range(QBLOCKS):
        o_ref[0, qb * BLOCK:(qb + 1) * BLOCK, :] = jnp.concatenate(outs[qb], axis=-1).astype(BF16)


def _attn(q, kvx, bias, ones, sink):
    b, s, _ = q.shape
    nb = s // BLOCK
    assert nb % QBLOCKS == 0 and QBLOCKS >= 2
    steps = nb // QBLOCKS
    edge_spec = lambda f: pl.BlockSpec((1, BLOCK, 4 * ATTN_KV), f)
    wide = lambda c: pl.BlockSpec((1, QBLOCKS * BLOCK, c), lambda bi, n, *_: (bi, n, 0))
    prev = lambda bi, n, *_: (bi, jnp.maximum(n * QBLOCKS - 1, 0), 0)
    nxt = lambda bi, n, *_: (bi, jnp.minimum((n + 1) * QBLOCKS, nb - 1), 0)
    grid_spec = pltpu.PrefetchScalarGridSpec(
        num_scalar_prefetch=1,
        grid=(b, steps),
        in_specs=[wide(ATTN_Q), edge_spec(prev), wide(4 * ATTN_KV), edge_spec(nxt),
                  _const_spec(bias.shape), _const_spec(ones.shape)],
        out_specs=wide(ATTN_Q),
    )
    return pl.pallas_call(
        _attn_kernel,
        grid_spec=grid_spec,
        out_shape=jax.ShapeDtypeStruct((b, s, ATTN_Q), BF16),
        compiler_params=pltpu.CompilerParams(dimension_semantics=("parallel", "arbitrary"),
                                             vmem_limit_bytes=VMEM_LIMIT),
        name="attn",
    )(sink, q, kvx, kvx, kvx, bias, ones)


def _s5_slab(u_ref, mt_ref, enc_ref, dect_ref, coef_ref, dsk_ref, y_ref):
    nq = u_ref.shape[0] // (CHUNK * SUBLANES)
    tile_rows = lambda x, q: x[q * SUBLANES:(q + 1) * SUBLANES]
    tile = lambda q, s: pl.ds((q * CHUNK + s) * SUBLANES, SUBLANES)

    xs = [jnp.concatenate([u_ref[tile(q, s), :] for q in range(nq)], axis=0) for s in range(CHUNK)]
    xst = [x.astype(BF16).T for x in xs]

    yts, pair_states = [], []
    for g in range(GROUPS_PER_SLAB):
        ugt = jnp.concatenate([xt[g * SSM_GROUP:(g + 1) * SSM_GROUP] for xt in xst], axis=0)
        ug = ugt.T
        yts.append(jnp.dot(mt_ref[g], ugt, preferred_element_type=F32))
        part = jnp.dot(ug, enc_ref[g], preferred_element_type=F32)
        if g % 2 == 0:
            pair_states.append(part)
        else:
            pair_states[-1] = pair_states[-1] + part

    def state(d, ri):
        c0 = (2 * d + ri) * LANES
        return jnp.concatenate([p[:, c0:c0 + LANES] for p in pair_states], axis=1)

    def coef(d, row):
        return coef_ref[d, 0, row:row + 1, :], coef_ref[d, 1, row:row + 1, :]

    seg = lax.broadcasted_iota(jnp.int32, (SUBLANES, 1), 0)

    def seg_shift(x, sh, forward):
        if forward:
            return jnp.where(seg >= sh, pltpu.roll(x, sh, axis=0), 0.0)
        return jnp.where(seg < SUBLANES - sh, pltpu.roll(x, SUBLANES - sh, axis=0), 0.0)

    carried = []
    for d in range(2):
        forward = d == 0
        s_re, s_im = state(d, 0), state(d, 1)
        a_re, a_im = coef(d, 0)
        order = list(range(nq)) if forward else list(range(nq - 1, -1, -1))
        t_re, t_im = {}, {}
        for n, q in enumerate(order):
            cur_re, cur_im = tile_rows(s_re, q), tile_rows(s_im, q)
            if n:
                p_re, p_im = t_re[order[n - 1]], t_im[order[n - 1]]
                cur_re = cur_re + (a_re * p_re - a_im * p_im)
                cur_im = cur_im + (a_re * p_im + a_im * p_re)
            t_re[q], t_im[q] = cur_re, cur_im
        e_re, e_im = t_re[order[-1]], t_im[order[-1]]
        for lv in range(SEG_LEVELS):
            b_re, b_im = coef(d, 1 + lv)
            h_re, h_im = seg_shift(e_re, 1 << lv, forward), seg_shift(e_im, 1 << lv, forward)
            e_re, e_im = e_re + (b_re * h_re - b_im * h_im), e_im + (b_re * h_im + b_im * h_re)
        c_re, c_im = seg_shift(e_re, 1, forward), seg_shift(e_im, 1, forward)
        x_re, x_im = [], []
        for q in range(nq):
            n = order.index(q)
            w_re, w_im = coef(d, 1 + SEG_LEVELS + n)
            r_re, r_im = w_re * c_re - w_im * c_im, w_re * c_im + w_im * c_re
            if n:
                r_re, r_im = r_re + t_re[order[n - 1]], r_im + t_im[order[n - 1]]
            x_re.append(r_re)
            x_im.append(r_im)
        carried += [jnp.concatenate(x_re, axis=0), jnp.concatenate(x_im, axis=0)]

    for g in range(GROUPS_PER_SLAB):
        c0 = (g // 2) * LANES
        xpair = jnp.concatenate([c[:, c0:c0 + LANES] for c in carried], axis=1).astype(BF16)
        yts[g] = yts[g] + lax.dot_general(dect_ref[g], xpair, (((1,), (1,)), ((), ())),
                                          preferred_element_type=F32)
    dsk = dsk_ref[...]
    for t in range(CHUNK):
        yt = jnp.concatenate([y[t * SSM_GROUP:(t + 1) * SSM_GROUP] for y in yts], axis=0)
        y = yt.T + dsk * xs[t]
        for q in range(nq):
            y_ref[tile(q, t), :] = tile_rows(y, q)


def _s5_kernel(u_ref, mt_ref, enc_ref, dect_ref, coef_ref, dsk_ref, y_ref):
    for i in range(SLABS_PER_STEP):
        _s5_slab(u_ref.at[i], mt_ref.at[i], enc_ref.at[i], dect_ref.at[i], coef_ref.at[i],
                 dsk_ref.at[i], y_ref.at[i])


def _s5(u4, mt, enc, dect, coef, dsk):
    b, _, seq, _ = u4.shape
    per_slab = lambda a: pl.BlockSpec((SLABS_PER_STEP,) + a.shape[1:],
                                      lambda sl, bi: (sl,) + (0,) * (a.ndim - 1),
                                      pipeline_mode=pl.Buffered(1))
    io = pl.BlockSpec((None, SLABS_PER_STEP, seq, LANES), lambda sl, bi: (bi, sl, 0, 0))
    return pl.pallas_call(
        _s5_kernel,
        grid=(SLABS // SLABS_PER_STEP, b),
        in_specs=[io, per_slab(mt), per_slab(enc), per_slab(dect), per_slab(coef), per_slab(dsk)],
        out_specs=io,
        out_shape=jax.ShapeDtypeStruct(u4.shape, F32),
        compiler_params=pltpu.CompilerParams(dimension_semantics=("parallel", "parallel"),
                                             vmem_limit_bytes=VMEM_LIMIT),
        name="s5",
    )(u4, mt, enc, dect, coef, dsk)


def _cmul(ar, ai, br, bi):
    return ar * br - ai * bi, ar * bi + ai * br


def _cpowers(a_re, a_im, n):
    p_re, p_im = jnp.ones_like(a_re)[None], jnp.zeros_like(a_im)[None]
    s_re, s_im = a_re, a_im
    while p_re.shape[0] < n:
        h_re, h_im = _cmul(p_re, p_im, s_re[None], s_im[None])
        p_re, p_im = jnp.concatenate([p_re, h_re], 0), jnp.concatenate([p_im, h_im], 0)
        s_re, s_im = _cmul(s_re, s_im, s_re, s_im)
    return p_re[:n], p_im[:n]


def _s5_gen_kernel(bm_ref, cm_ref, apow_ref, enc_ref, dect_ref):
    row_group = lax.broadcasted_iota(jnp.int32, (LANES, SLAB_STATE), 0) // SSM_GROUP
    col_group = lax.broadcasted_iota(jnp.int32, (LANES, SLAB_STATE), 1) // SSM_STATE

    def block_diag(ref, d, ri):
        wide = jnp.concatenate([ref[0, d, ri]] * (SLAB_STATE // LANES), axis=1)
        return jnp.where(row_group == col_group, wide, 0.0)

    bm = [[block_diag(bm_ref, d, ri) for ri in range(2)] for d in range(2)]
    cm = [[block_diag(cm_ref, d, ri) for ri in range(2)] for d in range(2)]

    def put(ref, d, ri, step, tile):
        tile = tile.astype(BF16)
        c0 = (2 * d + ri) * LANES
        for g in range(GROUPS_PER_SLAB):
            p0 = (g // 2) * LANES
            ref[0, g, step * SSM_GROUP:(step + 1) * SSM_GROUP, c0:c0 + LANES] = (
                tile[g * SSM_GROUP:(g + 1) * SSM_GROUP, p0:p0 + LANES])

    for d in range(2):
        for k in range(CHUNK + 1):
            a_re, a_im = apow_ref[0, d, 0, k:k + 1, :], apow_ref[0, d, 1, k:k + 1, :]
            if k < CHUNK:
                s = CHUNK - 1 - k if d == 0 else k
                e_re, e_im = _cmul(bm[d][0], bm[d][1], a_re, a_im)
                put(enc_ref, d, 0, s, e_re)
                put(enc_ref, d, 1, s, e_im)
            if k >= 1:
                t = k - 1 if d == 0 else CHUNK - k
                f_re, f_im = _cmul(cm[d][0], cm[d][1], a_re, a_im)
                put(dect_ref, d, 0, t, f_re)
                put(dect_ref, d, 1, t, -f_im)


def _s5_gen(bm, cm, apow):
    small = lambda a: pl.BlockSpec((1,) + a.shape[1:], lambda sl: (sl,) + (0,) * (a.ndim - 1))
    shape = (SLABS, GROUPS_PER_SLAB, CHUNK * SSM_GROUP, 4 * LANES)
    out = pl.BlockSpec((1,) + shape[1:], lambda sl: (sl, 0, 0, 0))
    return pl.pallas_call(
        _s5_gen_kernel,
        grid=(SLABS,),
        in_specs=[small(bm), small(cm), small(apow)],
        out_specs=[out, out],
        out_shape=[jax.ShapeDtypeStruct(shape, BF16), jax.ShapeDtypeStruct(shape, BF16)],
        compiler_params=pltpu.CompilerParams(dimension_semantics=("parallel",),
                                             vmem_limit_bytes=VMEM_LIMIT),
        name="s5_gen",
    )(bm, cm, apow)


def _slab_tiles(t_re, t_im):
    t = jnp.stack([t_re, t_im], 1).astype(F32)
    t = t.reshape(2, 2, SLABS, LANES, SSM_STATE)
    t = jnp.concatenate([t, t], axis=-1)
    return jnp.transpose(t, (2, 0, 1, 3, 4))


def _s5_weights(lam_re, lam_im, log_dt, b_re, b_im, c_re, c_im, nq):
    G, C = SSM_GROUPS, SSM_GROUP
    dt = jnp.exp(log_dt.astype(F32))[..., None]
    lr, li = lam_re.astype(F32), lam_im.astype(F32)
    mag = jnp.exp(lr * dt)
    ang = li * dt
    a_re, a_im = mag * jnp.cos(ang), mag * jnp.sin(ang)
    den = lr * lr + li * li
    nr, ni = a_re - 1.0, a_im
    z_re = (nr * lr + ni * li) / den
    z_im = (ni * lr - nr * li) / den
    br, bi = b_re.astype(F32), b_im.astype(F32)
    bb_re = z_re[..., None] * br - z_im[..., None] * bi
    bb_im = z_re[..., None] * bi + z_im[..., None] * br
    cr = jnp.swapaxes(c_re.astype(F32), -1, -2)
    ci = jnp.swapaxes(c_im.astype(F32), -1, -2)

    pw_re, pw_im = _cpowers(a_re, a_im, CHUNK + 1)

    ab_re, ab_im = _cmul(pw_re[:CHUNK, ..., None], pw_im[:CHUNK, ..., None], bb_re[None], bb_im[None])
    kern = jnp.einsum('dgpc,kdgpe->kdgce', jnp.concatenate([cr, -ci], 2),
                      jnp.concatenate([ab_re, ab_im], 3), precision='highest')
    lag_tab = jnp.concatenate([kern[::-1, 0][:-1], (kern[0, 0] + kern[0, 1])[None], kern[1:, 1]], 0)
    lag_tab = jnp.transpose(lag_tab, (1, 2, 0, 3)).reshape(G, C, (2 * CHUNK - 1) * C)
    mt = jnp.stack([lag_tab[:, :, (CHUNK - 1 - t) * C:(CHUNK - 1 - t) * C + CHUNK * C]
                    for t in range(CHUNK)], 1).reshape(SLABS, GROUPS_PER_SLAB, CHUNK * C, CHUNK * C)

    apow = jnp.stack([pw_re, pw_im], 0).reshape(2, CHUNK + 1, 2, SLABS, SLAB_STATE)
    apow = jnp.transpose(apow, (3, 2, 0, 1, 4))
    enc, dect = _s5_gen(_slab_tiles(jnp.swapaxes(bb_re, -1, -2), jnp.swapaxes(bb_im, -1, -2)),
                        _slab_tiles(c_re, c_im), apow)

    s_re, s_im = pw_re[CHUNK], pw_im[CHUNK]
    w_re, w_im = _cpowers(s_re, s_im, nq)
    q_re, q_im = _cmul(w_re[nq - 1], w_im[nq - 1], s_re, s_im)
    rows_re, rows_im = [s_re], [s_im]
    for _ in range(SEG_LEVELS):
        rows_re.append(q_re)
        rows_im.append(q_im)
        q_re, q_im = _cmul(q_re, q_im, q_re, q_im)
    c_all = jnp.stack([jnp.concatenate([jnp.stack(rows_re, 0), w_re], 0),
                       jnp.concatenate([jnp.stack(rows_im, 0), w_im], 0)], 0)
    n_rows = c_all.shape[1]
    c_all = c_all.reshape(2, n_rows, 2, SLABS, SLAB_STATE)
    coef = jnp.transpose(c_all, (3, 2, 0, 1, 4))
    return mt.astype(BF16), enc, dect, coef


def _post_kernel(x_ref, ya_ref, ys_ref, g_ref, wglu_ref, bglu_ref, wout_ref, ln2_ref,
                 wup_ref, wdown_ref, o_ref):
    nseg, tq, _ = x_ref.shape
    tm = nseg * tq
    ys = jnp.concatenate(
        [jnp.concatenate([ys_ref[sl, pl.ds(r, tq, stride=nseg), :] for sl in range(SLABS)], axis=1)
         for r in range(nseg)], axis=0)
    y = jax.nn.gelu(ys)
    z = jnp.dot(y.astype(BF16), wglu_ref[...], preferred_element_type=F32) + bglu_ref[...]
    y = y * jax.nn.sigmoid(z)
    g = g_ref[...].reshape(tm, 2 * D_MODEL).astype(F32)
    ya = ya_ref[...].reshape(tm, D_MODEL).astype(F32)
    mixed = jax.nn.sigmoid(g[:, :D_MODEL]) * ya + jax.nn.sigmoid(g[:, D_MODEL:]) * y
    x = x_ref[...].reshape(tm, D_MODEL)
    x = x + jnp.dot(mixed.astype(BF16), wout_ref[...], preferred_element_type=F32)
    ms = jnp.mean(x * x, axis=-1, keepdims=True)
    h = (x * ln2_ref[...]).astype(BF16)
    up = jnp.dot(h, wup_ref[...], preferred_element_type=F32) * lax.rsqrt(ms + EPS)
    act = jnp.square(jnp.maximum(up, 0.0)).astype(BF16)
    out = x + jnp.dot(act, wdown_ref[...], preferred_element_type=F32)
    o_ref[...] = out.reshape(nseg, tq, D_MODEL)


def _post(x2, ya, ys4, g, w_glu, b_glu, w_out, ln2, w_up, w_down, tq):
    b, _, seq, _ = ys4.shape
    per_seq = seq // SUBLANES // tq
    seg = lambda c: pl.BlockSpec((None, SUBLANES, tq, c), lambda i: (i // per_seq, 0, i % per_seq, 0))
    interleaved = pl.BlockSpec((None, SLABS, tq * SUBLANES, LANES),
                               lambda i: (i // per_seq, 0, i % per_seq, 0))
    out = pl.pallas_call(
        _post_kernel,
        grid=(b * per_seq,),
        in_specs=[seg(D_MODEL), seg(D_MODEL), interleaved, seg(2 * D_MODEL),
                  _const_spec((D_MODEL, D_MODEL)), _const_spec((1, D_MODEL)),
                  _const_spec((D_MODEL, D_MODEL)), _const_spec((1, D_MODEL)),
                  _const_spec((D_MODEL, D_FF)), _const_spec((D_FF, D_MODEL))],
        out_specs=seg(D_MODEL),
        out_shape=jax.ShapeDtypeStruct((b, SUBLANES, seq // SUBLANES, D_MODEL), F32),
        compiler_params=pltpu.CompilerParams(dimension_semantics=("parallel",),
                                             vmem_limit_bytes=VMEM_LIMIT),
        name="post",
    )(_seg_view(x2, b, seq), _seg_view(ya, b, seq), ys4, _seg_view(g, b, seq),
      w_glu, b_glu, w_out, ln2, w_up, w_down)
    return out.reshape(b, seq, D_MODEL)


def _t5_bucket(rel):
    nb = N_BUCKETS // 2
    ret = (rel > 0).astype(np.int32) * nb
    n = np.abs(rel)
    max_exact = nb // 2
    n_safe = np.maximum(n, 1).astype(np.float32)
    large = max_exact + (np.log(n_safe / max_exact) / math.log(MAX_DISTANCE / max_exact)
                         * (nb - max_exact)).astype(np.int32)
    large = np.minimum(large, nb - 1)
    return (ret + np.where(n < max_exact, n, large)).astype(np.int32)


def _band_bias(rel_table):
    qi = np.arange(BLOCK)[:, None]
    kj = np.arange(3 * BLOCK)[None, :]
    rel = kj - BLOCK - qi
    onehot = (np.arange(N_BUCKETS)[:, None] == _t5_bucket(rel).reshape(1, -1)).astype(np.float32)
    bias = jnp.dot(rel_table.astype(F32).T * LOG2E, jnp.asarray(onehot), precision='highest')
    bias = bias.reshape(N_HEADS, BLOCK, 3 * BLOCK)
    in_band = np.abs(rel) <= WINDOW
    keep = np.stack([in_band & (kj >= BLOCK), in_band, in_band & (kj < 2 * BLOCK)], 0)
    bias = jnp.where(jnp.asarray(keep)[:, None], bias[None], NEG_INF)
    bias = bias.reshape(3, N_KV_HEADS, 2, 2, BLOCK, 3 * BLOCK)
    return jnp.transpose(bias, (0, 1, 2, 4, 3, 5)).reshape(3, N_KV_HEADS, 2 * BLOCK, 6 * BLOCK)


def _sum_mat():
    first = np.arange(6 * BLOCK)[:, None] < 3 * BLOCK
    low = np.arange(LANES)[None, :] < HEAD_DIM
    return jnp.asarray(first == low, BF16)


def _layer(x, bias, sink, ln1, w_in, qg, kg, s5w, dsk, w_glu, b_glu, w_out, ln2, w_up, w_down):
    b, s, _ = x.shape
    x2 = x.reshape(b * s, D_MODEL)
    rows_per_seg = s // SUBLANES
    q, kvx, u4, g = _in_proj(x2, ln1, w_in, qg, kg, b, s, min(TQ_IN, rows_per_seg))
    ya = _attn(q.reshape(b, s, ATTN_Q), kvx.reshape(b, s, 4 * ATTN_KV), bias, _sum_mat(), sink)
    ys4 = _s5(u4, *s5w, dsk)
    return _post(x2, ya.reshape(b * s, ATTN_Q), ys4, g, w_glu, b_glu, w_out, ln2, w_up, w_down,
                 min(TQ_POST, rows_per_seg))


def kernel(x_prompt, x_sample, rel_table, ln1, w_in, q_gain, k_gain, sink, lam_re, lam_im, log_dt,
           b_re, b_im, c_re, c_im, d_skip, w_glu, b_glu, w_out, ln2, w_up, w_down):
    assert ln1.shape[0] == 1, "single layer"
    bias = _band_bias(rel_table)
    outs = []
    s5w_by_nq = {}
    for x in (x_prompt, x_sample):
        seq = x.shape[1]
        nq = seq // (CHUNK * SUBLANES)
        assert nq * CHUNK * SUBLANES == seq and seq % BLOCK == 0
        if nq not in s5w_by_nq:
            s5w_by_nq[nq] = _s5_weights(lam_re[0], lam_im[0], log_dt[0], b_re[0], b_im[0],
                                        c_re[0], c_im[0], nq)
        s5w = s5w_by_nq[nq]
        outs.append(_layer(
            x, bias, sink[0].astype(F32) * LOG2E,
            ln1[0].astype(F32)[None], w_in[0].astype(BF16),
            jnp.tile(q_gain[0].astype(F32), N_HEADS)[None],
            jnp.tile(k_gain[0].astype(F32), N_KV_HEADS)[None],
            s5w, d_skip[0].astype(F32).reshape(SLABS, 1, LANES),
            w_glu[0].astype(BF16), b_glu[0].astype(F32)[None], w_out[0].astype(BF16),
            ln2[0].astype(F32)[None], w_up[0].astype(BF16), w_down[0].astype(BF16)))
    return tuple(outs)
```

```python
import math

import numpy as np
import jax
import jax.numpy as jnp
from jax import lax
from jax.experimental import pallas as pl
from jax.experimental.pallas import tpu as pltpu

D_MODEL = 1024
N_HEADS = 16
N_KV_HEADS = 4
HEAD_DIM = 64
Q_PER_KV = N_HEADS // N_KV_HEADS
WINDOW = 128
BLOCK = 128
N_BUCKETS = 32
MAX_DISTANCE = 128
ATTN_Q = N_HEADS * HEAD_DIM
ATTN_KV = N_KV_HEADS * HEAD_DIM
NEG_INF = -1e30
SSM_WIDTH = D_MODEL
SSM_GROUP = 16
SSM_GROUPS = SSM_WIDTH // SSM_GROUP
SSM_STATE = 64
D_FF = 4 * D_MODEL
EPS = 1e-6
IN_COLS = ATTN_Q + 2 * ATTN_KV + SSM_WIDTH + 2 * D_MODEL

LANES = 128
SUBLANES = 8
SEG_LEVELS = 3
CHUNK = 16
SLABS = SSM_WIDTH // LANES
GROUPS_PER_SLAB = LANES // SSM_GROUP
SLAB_STATE = GROUPS_PER_SLAB * SSM_STATE
VMEM_LIMIT = 56 * 1024 * 1024
TQ_IN = 128
TQ_POST = 64
QBLOCKS = 8
SLABS_PER_STEP = 2

LOG2E = math.log2(math.e)
Q_SCALE = HEAD_DIM ** -0.5 * LOG2E

BF16 = jnp.bfloat16
F32 = jnp.float32


def _const_spec(shape):
    nd = len(shape)
    return pl.BlockSpec(shape, lambda *_: (0,) * nd, pipeline_mode=pl.Buffered(1))


def _in_proj_kernel(x_ref, ln1_ref, w_ref, qg_ref, kg_ref, q_ref, kv_ref, u_ref, g_ref):
    nseg, tq, _ = x_ref.shape
    tm = nseg * tq
    x = x_ref[...].reshape(tm, D_MODEL)
    ms = jnp.mean(x * x, axis=-1, keepdims=True)
    rinv = lax.rsqrt(ms + EPS)
    h = (x * ln1_ref[...]).astype(BF16)
    proj = lambda c0, width: jnp.dot(h, w_ref[:, c0:c0 + width], preferred_element_type=F32) * rinv

    low = lax.broadcasted_iota(jnp.int32, (1, LANES), 1) < HEAD_DIM

    def head_norm(t, gain):
        cols = []
        for j in range(t.shape[1] // LANES):
            blk = t[:, j * LANES:(j + 1) * LANES]
            sq = blk * blk
            m_lo = jnp.sum(jnp.where(low, sq, 0.0), axis=-1, keepdims=True) * (1.0 / HEAD_DIM)
            m_hi = jnp.sum(jnp.where(low, 0.0, sq), axis=-1, keepdims=True) * (1.0 / HEAD_DIM)
            cols.append(blk * jnp.where(low, lax.rsqrt(m_lo + EPS), lax.rsqrt(m_hi + EPS)))
        return jnp.concatenate(cols, axis=1) * gain

    c0 = 0
    q = (head_norm(proj(c0, ATTN_Q), qg_ref[...]) * Q_SCALE).astype(BF16)
    q_ref[...] = q.reshape(nseg, tq, ATTN_Q)
    c0 += ATTN_Q
    k = head_norm(proj(c0, ATTN_KV), kg_ref[...])
    c0 += ATTN_KV
    v = proj(c0, ATTN_KV)
    c0 += ATTN_KV
    kvx = jnp.concatenate([k, pltpu.roll(k, HEAD_DIM, axis=1), v, pltpu.roll(v, HEAD_DIM, axis=1)],
                          axis=1).astype(BF16)
    kv_ref[...] = kvx.reshape(nseg, tq, 4 * ATTN_KV)
    u = proj(c0, SSM_WIDTH)
    for sl in range(SLABS):
        for r in range(nseg):
            u_ref[sl, pl.ds(r, tq, stride=nseg), :] = u[r * tq:(r + 1) * tq, sl * LANES:(sl + 1) * LANES]
    c0 += SSM_WIDTH
    g = proj(c0, 2 * D_MODEL).astype(BF16)
    g_ref[...] = g.reshape(nseg, tq, 2 * D_MODEL)


def _seg_view(a, b, seq):
    return a.reshape(b, SUBLANES, seq // SUBLANES, a.shape[-1])


def _in_proj(x2, ln1, w_in, qg, kg, b, seq, tq):
    n = x2.shape[0]
    per_seq = seq // SUBLANES // tq
    seg = lambda c: pl.BlockSpec((None, SUBLANES, tq, c), lambda i: (i // per_seq, 0, i % per_seq, 0))
    interleaved = pl.BlockSpec((None, SLABS, tq * SUBLANES, LANES),
                               lambda i: (i // per_seq, 0, i % per_seq, 0))
    seg_shape = lambda c, dt: jax.ShapeDtypeStruct((b, SUBLANES, seq // SUBLANES, c), dt)
    q, kvx, u, g = pl.pallas_call(
        _in_proj_kernel,
        grid=(b * per_seq,),
        in_specs=[seg(D_MODEL), _const_spec((1, D_MODEL)), _const_spec((D_MODEL, IN_COLS)),
                  _const_spec((1, ATTN_Q)), _const_spec((1, ATTN_KV))],
        out_specs=[seg(ATTN_Q), seg(4 * ATTN_KV), interleaved, seg(2 * D_MODEL)],
        out_shape=[seg_shape(ATTN_Q, BF16), seg_shape(4 * ATTN_KV, BF16),
                   jax.ShapeDtypeStruct((b, SLABS, seq, LANES), F32), seg_shape(2 * D_MODEL, BF16)],
        compiler_params=pltpu.CompilerParams(dimension_semantics=("parallel",),
                                             vmem_limit_bytes=VMEM_LIMIT),
        name="in_proj",
    )(_seg_view(x2, b, seq), ln1, w_in, qg, kg)
    return q.reshape(n, ATTN_Q), kvx.reshape(n, 4 * ATTN_KV), u, g.reshape(n, 2 * D_MODEL)


_ROLLED_TILE = (0, 1, 1, 0)


def _attn_kernel(sink_ref, q_ref, kvp_ref, kvc_ref, kvn_ref, bias_ref, ones_ref, o_ref):
    n = pl.program_id(1)
    last = pl.num_programs(1) - 1
    kvx = jnp.concatenate([kvp_ref[0], kvc_ref[0], kvn_ref[0]], axis=0)
    low = lax.broadcasted_iota(jnp.int32, (1, LANES), 1) < HEAD_DIM
    zero = jnp.zeros((), BF16)

    def halves(base, kv):
        plain = kvx[:, base + (kv // 2) * LANES:base + (kv // 2 + 1) * LANES]
        r0 = base + ATTN_KV + _ROLLED_TILE[kv] * LANES
        rolled = kvx[:, r0:r0 + LANES]
        lo_src, hi_src = (plain, rolled) if kv % 2 == 0 else (rolled, plain)
        return jnp.where(low, lo_src, zero), jnp.where(low, zero, hi_src)

    top = lax.broadcasted_iota(jnp.int32, (2 * BLOCK, 1), 0) < BLOCK
    outs = [[] for _ in range(QBLOCKS)]
    for kv in range(N_KV_HEADS):
        ke, ko = halves(0, kv)
        ve, vo = halves(2 * ATTN_KV, kv)
        sink_e = jnp.where(top, sink_ref[4 * kv], sink_ref[4 * kv + 2])
        sink_o = jnp.where(top, sink_ref[4 * kv + 1], sink_ref[4 * kv + 3])
        for qb in range(QBLOCKS):
            rows = slice(qb * BLOCK, (qb + 3) * BLOCK)
            keys = jnp.concatenate([ke[rows], ko[rows]], axis=0)
            vals = jnp.concatenate([jnp.concatenate([ve[rows], vo[rows]], axis=0), ones_ref[...]],
                                   axis=1)
            if qb == 0:
                variant = jnp.where(n == 0, 0, 1)
            elif qb == QBLOCKS - 1:
                variant = jnp.where(n == last, 2, 1)
            else:
                variant = 1
            qrows = slice(qb * BLOCK, (qb + 1) * BLOCK)
            q2 = jnp.concatenate([q_ref[0, qrows, (2 * kv) * LANES:(2 * kv + 1) * LANES],
                                  q_ref[0, qrows, (2 * kv + 1) * LANES:(2 * kv + 2) * LANES]], axis=0)
            s = lax.dot_general(q2, keys, (((1,), (1,)), ((), ())), preferred_element_type=F32)
            s = s + bias_ref[variant, kv]
            se, so = s[:, :3 * BLOCK], s[:, 3 * BLOCK:]
            me = jnp.maximum(jnp.max(se, axis=-1, keepdims=True), sink_e)
            mo = jnp.maximum(jnp.max(so, axis=-1, keepdims=True), sink_o)
            p = jnp.concatenate([jnp.exp2(se - me), jnp.exp2(so - mo)], axis=1).astype(BF16)
            od = jnp.dot(p, vals, preferred_element_type=F32)
            den = od[:, LANES:] + jnp.where(low, jnp.exp2(sink_e - me), jnp.exp2(sink_o - mo))
            o = od[:, :LANES] / den
            outs[qb] += [o[:BLOCK], o[BLOCK:]]
    for qb in range(QBLOCKS):
        o_ref[0, qb * BLOCK:(qb + 1) * BLOCK, :] = jnp.concatenate(outs[qb], axis=-1).astype(BF16)


def _attn(q, kvx, bias, ones, sink):
    b, s, _ = q.shape
    nb = s // BLOCK
    assert nb % QBLOCKS == 0 and QBLOCKS >= 2
    steps = nb // QBLOCKS
    edge_spec = lambda f: pl.BlockSpec((1, BLOCK, 4 * ATTN_KV), f)
    wide = lambda c: pl.BlockSpec((1, QBLOCKS * BLOCK, c), lambda bi, n, *_: (bi, n, 0))
    prev = lambda bi, n, *_: (bi, jnp.maximum(n * QBLOCKS - 1, 0), 0)
    nxt = lambda bi, n, *_: (bi, jnp.minimum((n + 1) * QBLOCKS, nb - 1), 0)
    grid_spec = pltpu.PrefetchScalarGridSpec(
        num_scalar_prefetch=1,
        grid=(b, steps),
        in_specs=[wide(ATTN_Q), edge_spec(prev), wide(4 * ATTN_KV), edge_spec(nxt),
                  _const_spec(bias.shape), _const_spec(ones.shape)],
        out_specs=wide(ATTN_Q),
    )
    return pl.pallas_call(
        _attn_kernel,
        grid_spec=grid_spec,
        out_shape=jax.ShapeDtypeStruct((b, s, ATTN_Q), BF16),
        compiler_params=pltpu.CompilerParams(dimension_semantics=("parallel", "arbitrary"),
                                             vmem_limit_bytes=VMEM_LIMIT),
        name="attn",
    )(sink, q, kvx, kvx, kvx, bias, ones)


def _s5_slab(u_ref, mt_ref, enc_ref, dect_ref, coef_ref, dsk_ref, y_ref):
    nq = u_ref.shape[0] // (CHUNK * SUBLANES)
    tile_rows = lambda x, q: x[q * SUBLANES:(q + 1) * SUBLANES]
    tile = lambda q, s: pl.ds((q * CHUNK + s) * SUBLANES, SUBLANES)

    xs = [jnp.concatenate([u_ref[tile(q, s), :] for q in range(nq)], axis=0) for s in range(CHUNK)]
    xst = [x.astype(BF16).T for x in xs]

    yts, pair_states = [], []
    for g in range(GROUPS_PER_SLAB):
        ugt = jnp.concatenate([xt[g * SSM_GROUP:(g + 1) * SSM_GROUP] for xt in xst], axis=0)
        ug = ugt.T
        yts.append(jnp.dot(mt_ref[g], ugt, preferred_element_type=F32))
        part = jnp.dot(ug, enc_ref[g], preferred_element_type=F32)
        if g % 2 == 0:
            pair_states.append(part)
        else:
            pair_states[-1] = pair_states[-1] + part

    def state(d, ri):
        c0 = (2 * d + ri) * LANES
        return jnp.concatenate([p[:, c0:c0 + LANES] for p in pair_states], axis=1)

    def coef(d, row):
        return coef_ref[d, 0, row:row + 1, :], coef_ref[d, 1, row:row + 1, :]

    seg = lax.broadcasted_iota(jnp.int32, (SUBLANES, 1), 0)

    def seg_shift(x, sh, forward):
        if forward:
            return jnp.where(seg >= sh, pltpu.roll(x, sh, axis=0), 0.0)
        return jnp.where(seg < SUBLANES - sh, pltpu.roll(x, SUBLANES - sh, axis=0), 0.0)

    carried = []
    for d in range(2):
        forward = d == 0
        s_re, s_im = state(d, 0), state(d, 1)
        a_re, a_im = coef(d, 0)
        order = list(range(nq)) if forward else list(range(nq - 1, -1, -1))
        t_re, t_im = {}, {}
        for n, q in enumerate(order):
            cur_re, cur_im = tile_rows(s_re, q), tile_rows(s_im, q)
            if n:
                p_re, p_im = t_re[order[n - 1]], t_im[order[n - 1]]
                cur_re = cur_re + (a_re * p_re - a_im * p_im)
                cur_im = cur_im + (a_re * p_im + a_im * p_re)
            t_re[q], t_im[q] = cur_re, cur_im
        e_re, e_im = t_re[order[-1]], t_im[order[-1]]
        for lv in range(SEG_LEVELS):
            b_re, b_im = coef(d, 1 + lv)
            h_re, h_im = seg_shift(e_re, 1 << lv, forward), seg_shift(e_im, 1 << lv, forward)
            e_re, e_im = e_re + (b_re * h_re - b_im * h_im), e_im + (b_re * h_im + b_im * h_re)
        c_re, c_im = seg_shift(e_re, 1, forward), seg_shift(e_im, 1, forward)
        x_re, x_im = [], []
        for q in range(nq):
            n = order.index(q)
            w_re, w_im = coef(d, 1 + SEG_LEVELS + n)
            r_re, r_im = w_re * c_re - w_im * c_im, w_re * c_im + w_im * c_re
            if n:
                r_re, r_im = r_re + t_re[order[n - 1]], r_im + t_im[order[n - 1]]
            x_re.append(r_re)
            x_im.append(r_im)
        carried += [jnp.concatenate(x_re, axis=0), jnp.concatenate(x_im, axis=0)]

    for g in range(GROUPS_PER_SLAB):
        c0 = (g // 2) * LANES
        xpair = jnp.concatenate([c[:, c0:c0 + LANES] for c in carried], axis=1).astype(BF16)
        yts[g] = yts[g] + lax.dot_general(dect_ref[g], xpair, (((1,), (1,)), ((), ())),
                                          preferred_element_type=F32)
    dsk = dsk_ref[...]
    for t in range(CHUNK):
        yt = jnp.concatenate([y[t * SSM_GROUP:(t + 1) * SSM_GROUP] for y in yts], axis=0)
        y = yt.T + dsk * xs[t]
        for q in range(nq):
            y_ref[tile(q, t), :] = tile_rows(y, q)


def _s5_kernel(u_ref, mt_ref, enc_ref, dect_ref, coef_ref, dsk_ref, y_ref):
    for i in range(SLABS_PER_STEP):
        _s5_slab(u_ref.at[i], mt_ref.at[i], enc_ref.at[i], dect_ref.at[i], coef_ref.at[i],
                 dsk_ref.at[i], y_ref.at[i])


def _s5(u4, mt, enc, dect, coef, dsk):
    b, _, seq, _ = u4.shape
    per_slab = lambda a: pl.BlockSpec((SLABS_PER_STEP,) + a.shape[1:],
                                      lambda sl, bi: (sl,) + (0,) * (a.ndim - 1),
                                      pipeline_mode=pl.Buffered(1))
    io = pl.BlockSpec((None, SLABS_PER_STEP, seq, LANES), lambda sl, bi: (bi, sl, 0, 0))
    return pl.pallas_call(
        _s5_kernel,
        grid=(SLABS // SLABS_PER_STEP, b),
        in_specs=[io, per_slab(mt), per_slab(enc), per_slab(dect), per_slab(coef), per_slab(dsk)],
        out_specs=io,
        out_shape=jax.ShapeDtypeStruct(u4.shape, F32),
        compiler_params=pltpu.CompilerParams(dimension_semantics=("parallel", "parallel"),
                                             vmem_limit_bytes=VMEM_LIMIT),
        name="s5",
    )(u4, mt, enc, dect, coef, dsk)


def _cmul(ar, ai, br, bi):
    return ar * br - ai * bi, ar * bi + ai * br


def _cpowers(a_re, a_im, n):
    p_re, p_im = jnp.ones_like(a_re)[None], jnp.zeros_like(a_im)[None]
    s_re, s_im = a_re, a_im
    while p_re.shape[0] < n:
        h_re, h_im = _cmul(p_re, p_im, s_re[None], s_im[None])
        p_re, p_im = jnp.concatenate([p_re, h_re], 0), jnp.concatenate([p_im, h_im], 0)
        s_re, s_im = _cmul(s_re, s_im, s_re, s_im)
    return p_re[:n], p_im[:n]


def _s5_gen_kernel(bm_ref, cm_ref, apow_ref, enc_ref, dect_ref):
    row_group = lax.broadcasted_iota(jnp.int32, (LANES, SLAB_STATE), 0) // SSM_GROUP
    col_group = lax.broadcasted_iota(jnp.int32, (LANES, SLAB_STATE), 1) // SSM_STATE

    def block_diag(ref, d, ri):
        wide = jnp.concatenate([ref[0, d, ri]] * (SLAB_STATE // LANES), axis=1)
        return jnp.where(row_group == col_group, wide, 0.0)

    bm = [[block_diag(bm_ref, d, ri) for ri in range(2)] for d in range(2)]
    cm = [[block_diag(cm_ref, d, ri) for ri in range(2)] for d in range(2)]

    def put(ref, d, ri, step, tile):
        tile = tile.astype(BF16)
        c0 = (2 * d + ri) * LANES
        for g in range(GROUPS_PER_SLAB):
            p0 = (g // 2) * LANES
            ref[0, g, step * SSM_GROUP:(step + 1) * SSM_GROUP, c0:c0 + LANES] = (
                tile[g * SSM_GROUP:(g + 1) * SSM_GROUP, p0:p0 + LANES])

    for d in range(2):
        for k in range(CHUNK + 1):
            a_re, a_im = apow_ref[0, d, 0, k:k + 1, :], apow_ref[0, d, 1, k:k + 1, :]
            if k < CHUNK:
                s = CHUNK - 1 - k if d == 0 else k
                e_re, e_im = _cmul(bm[d][0], bm[d][1], a_re, a_im)
                put(enc_ref, d, 0, s, e_re)
                put(enc_ref, d, 1, s, e_im)
            if k >= 1:
                t = k - 1 if d == 0 else CHUNK - k
                f_re, f_im = _cmul(cm[d][0], cm[d][1], a_re, a_im)
                put(dect_ref, d, 0, t, f_re)
                put(dect_ref, d, 1, t, -f_im)


def _s5_gen(bm, cm, apow):
    small = lambda a: pl.BlockSpec((1,) + a.shape[1:], lambda sl: (sl,) + (0,) * (a.ndim - 1))
    shape = (SLABS, GROUPS_PER_SLAB, CHUNK * SSM_GROUP, 4 * LANES)
    out = pl.BlockSpec((1,) + shape[1:], lambda sl: (sl, 0, 0, 0))
    return pl.pallas_call(
        _s5_gen_kernel,
        grid=(SLABS,),
        in_specs=[small(bm), small(cm), small(apow)],
        out_specs=[out, out],
        out_shape=[jax.ShapeDtypeStruct(shape, BF16), jax.ShapeDtypeStruct(shape, BF16)],
        compiler_params=pltpu.CompilerParams(dimension_semantics=("parallel",),
                                             vmem_limit_bytes=VMEM_LIMIT),
        name="s5_gen",
    )(bm, cm, apow)


def _slab_tiles(t_re, t_im):
    t = jnp.stack([t_re, t_im], 1).astype(F32)
    t = t.reshape(2, 2, SLABS, LANES, SSM_STATE)
    t = jnp.concatenate([t, t], axis=-1)
    return jnp.transpose(t, (2, 0, 1, 3, 4))


def _s5_weights(lam_re, lam_im, log_dt, b_re, b_im, c_re, c_im, nq):
    G, C = SSM_GROUPS, SSM_GROUP
    dt = jnp.exp(log_dt.astype(F32))[..., None]
    lr, li = lam_re.astype(F32), lam_im.astype(F32)
    mag = jnp.exp(lr * dt)
    ang = li * dt
    a_re, a_im = mag * jnp.cos(ang), mag * jnp.sin(ang)
    den = lr * lr + li * li
    nr, ni = a_re - 1.0, a_im
    z_re = (nr * lr + ni * li) / den
    z_im = (ni * lr - nr * li) / den
    br, bi = b_re.astype(F32), b_im.astype(F32)
    bb_re = z_re[..., None] * br - z_im[..., None] * bi
    bb_im = z_re[..., None] * bi + z_im[..., None] * br
    cr = jnp.swapaxes(c_re.astype(F32), -1, -2)
    ci = jnp.swapaxes(c_im.astype(F32), -1, -2)

    pw_re, pw_im = _cpowers(a_re, a_im, CHUNK + 1)

    ab_re, ab_im = _cmul(pw_re[:CHUNK, ..., None], pw_im[:CHUNK, ..., None], bb_re[None], bb_im[None])
    kern = jnp.einsum('dgpc,kdgpe->kdgce', jnp.concatenate([cr, -ci], 2),
                      jnp.concatenate([ab_re, ab_im], 3), precision='highest')
    lag_tab = jnp.concatenate([kern[::-1, 0][:-1], (kern[0, 0] + kern[0, 1])[None], kern[1:, 1]], 0)
    lag_tab = jnp.transpose(lag_tab, (1, 2, 0, 3)).reshape(G, C, (2 * CHUNK - 1) * C)
    mt = jnp.stack([lag_tab[:, :, (CHUNK - 1 - t) * C:(CHUNK - 1 - t) * C + CHUNK * C]
                    for t in range(CHUNK)], 1).reshape(SLABS, GROUPS_PER_SLAB, CHUNK * C, CHUNK * C)

    apow = jnp.stack([pw_re, pw_im], 0).reshape(2, CHUNK + 1, 2, SLABS, SLAB_STATE)
    apow = jnp.transpose(apow, (3, 2, 0, 1, 4))
    enc, dect = _s5_gen(_slab_tiles(jnp.swapaxes(bb_re, -1, -2), jnp.swapaxes(bb_im, -1, -2)),
                        _slab_tiles(c_re, c_im), apow)

    s_re, s_im = pw_re[CHUNK], pw_im[CHUNK]
    w_re, w_im = _cpowers(s_re, s_im, nq)
    q_re, q_im = _cmul(w_re[nq - 1], w_im[nq - 1], s_re, s_im)
    rows_re, rows_im = [s_re], [s_im]
    for _ in range(SEG_LEVELS):
        rows_re.append(q_re)
        rows_im.append(q_im)
        q_re, q_im = _cmul(q_re, q_im, q_re, q_im)
    c_all = jnp.stack([jnp.concatenate([jnp.stack(rows_re, 0), w_re], 0),
                       jnp.concatenate([jnp.stack(rows_im, 0), w_im], 0)], 0)
    n_rows = c_all.shape[1]
    c_all = c_all.reshape(2, n_rows, 2, SLABS, SLAB_STATE)
    coef = jnp.transpose(c_all, (3, 2, 0, 1, 4))
    return mt.astype(BF16), enc, dect, coef


def _post_kernel(x_ref, ya_ref, ys_ref, g_ref, wglu_ref, bglu_ref, wout_ref, ln2_ref,
                 wup_ref, wdown_ref, o_ref):
    nseg, tq, _ = x_ref.shape
    tm = nseg * tq
    ys = jnp.concatenate(
        [jnp.concatenate([ys_ref[sl, pl.ds(r, tq, stride=nseg), :] for sl in range(SLABS)], axis=1)
         for r in range(nseg)], axis=0)
    y = jax.nn.gelu(ys)
    z = jnp.dot(y.astype(BF16), wglu_ref[...], preferred_element_type=F32) + bglu_ref[...]
    y = y * jax.nn.sigmoid(z)
    g = g_ref[...].reshape(tm, 2 * D_MODEL).astype(F32)
    ya = ya_ref[...].reshape(tm, D_MODEL).astype(F32)
    mixed = jax.nn.sigmoid(g[:, :D_MODEL]) * ya + jax.nn.sigmoid(g[:, D_MODEL:]) * y
    x = x_ref[...].reshape(tm, D_MODEL)
    x = x + jnp.dot(mixed.astype(BF16), wout_ref[...], preferred_element_type=F32)
    ms = jnp.mean(x * x, axis=-1, keepdims=True)
    h = (x * ln2_ref[...]).astype(BF16)
    up = jnp.dot(h, wup_ref[...], preferred_element_type=F32) * lax.rsqrt(ms + EPS)
    act = jnp.square(jnp.maximum(up, 0.0)).astype(BF16)
    out = x + jnp.dot(act, wdown_ref[...], preferred_element_type=F32)
    o_ref[...] = out.reshape(nseg, tq, D_MODEL)


def _post(x2, ya, ys4, g, w_glu, b_glu, w_out, ln2, w_up, w_down, tq):
    b, _, seq, _ = ys4.shape
    per_seq = seq // SUBLANES // tq
    seg = lambda c: pl.BlockSpec((None, SUBLANES, tq, c), lambda i: (i // per_seq, 0, i % per_seq, 0))
    interleaved = pl.BlockSpec((None, SLABS, tq * SUBLANES, LANES),
                               lambda i: (i // per_seq, 0, i % per_seq, 0))
    out = pl.pallas_call(
        _post_kernel,
        grid=(b * per_seq,),
        in_specs=[seg(D_MODEL), seg(D_MODEL), interleaved, seg(2 * D_MODEL),
                  _const_spec((D_MODEL, D_MODEL)), _const_spec((1, D_MODEL)),
                  _const_spec((D_MODEL, D_MODEL)), _const_spec((1, D_MODEL)),
                  _const_spec((D_MODEL, D_FF)), _const_spec((D_FF, D_MODEL))],
        out_specs=seg(D_MODEL),
        out_shape=jax.ShapeDtypeStruct((b, SUBLANES, seq // SUBLANES, D_MODEL), F32),
        compiler_params=pltpu.CompilerParams(dimension_semantics=("parallel",),
                                             vmem_limit_bytes=VMEM_LIMIT),
        name="post",
    )(_seg_view(x2, b, seq), _seg_view(ya, b, seq), ys4, _seg_view(g, b, seq),
      w_glu, b_glu, w_out, ln2, w_up, w_down)
    return out.reshape(b, seq, D_MODEL)


def _t5_bucket(rel):
    nb = N_BUCKETS // 2
    ret = (rel > 0).astype(np.int32) * nb
    n = np.abs(rel)
    max_exact = nb // 2
    n_safe = np.maximum(n, 1).astype(np.float32)
    large = max_exact + (np.log(n_safe / max_exact) / math.log(MAX_DISTANCE / max_exact)
                         * (nb - max_exact)).astype(np.int32)
    large = np.minimum(large, nb - 1)
    return (ret + np.where(n < max_exact, n, large)).astype(np.int32)


def _band_bias(rel_table):
    qi = np.arange(BLOCK)[:, None]
    kj = np.arange(3 * BLOCK)[None, :]
    rel = kj - BLOCK - qi
    onehot = (np.arange(N_BUCKETS)[:, None] == _t5_bucket(rel).reshape(1, -1)).astype(np.float32)
    bias = jnp.dot(rel_table.astype(F32).T * LOG2E, jnp.asarray(onehot), precision='highest')
    bias = bias.reshape(N_HEADS, BLOCK, 3 * BLOCK)
    in_band = np.abs(rel) <= WINDOW
    keep = np.stack([in_band & (kj >= BLOCK), in_band, in_band & (kj < 2 * BLOCK)], 0)
    bias = jnp.where(jnp.asarray(keep)[:, None], bias[None], NEG_INF)
    bias = bias.reshape(3, N_KV_HEADS, 2, 2, BLOCK, 3 * BLOCK)
    return jnp.transpose(bias, (0, 1, 2, 4, 3, 5)).reshape(3, N_KV_HEADS, 2 * BLOCK, 6 * BLOCK)


def _sum_mat():
    first = np.arange(6 * BLOCK)[:, None] < 3 * BLOCK
    low = np.arange(LANES)[None, :] < HEAD_DIM
    return jnp.asarray(first == low, BF16)


def _layer(x, bias, sink, ln1, w_in, qg, kg, s5w, dsk, w_glu, b_glu, w_out, ln2, w_up, w_down):
    b, s, _ = x.shape
    x2 = x.reshape(b * s, D_MODEL)
    rows_per_seg = s // SUBLANES
    q, kvx, u4, g = _in_proj(x2, ln1, w_in, qg, kg, b, s, min(TQ_IN, rows_per_seg))
    ya = _attn(q.reshape(b, s, ATTN_Q), kvx.reshape(b, s, 4 * ATTN_KV), bias, _sum_mat(), sink)
    ys4 = _s5(u4, *s5w, dsk)
    return _post(x2, ya.reshape(b * s, ATTN_Q), ys4, g, w_glu, b_glu, w_out, ln2, w_up, w_down,
                 min(TQ_POST, rows_per_seg))


def kernel(x_prompt, x_sample, rel_table, ln1, w_in, q_gain, k_gain, sink, lam_re, lam_im, log_dt,
           b_re, b_im, c_re, c_im, d_skip, w_glu, b_glu, w_out, ln2, w_up, w_down):
    assert ln1.shape[0] == 1, "single layer"
    bias = _band_bias(rel_table)
    outs = []
    s5w_by_nq = {}
    for x in (x_prompt, x_sample):
        seq = x.shape[1]
        nq = seq // (CHUNK * SUBLANES)
        assert nq * CHUNK * SUBLANES == seq and seq % BLOCK == 0
        if nq not in s5w_by_nq:
            s5w_by_nq[nq] = _s5_weights(lam_re[0], lam_im[0], log_dt[0], b_re[0], b_im[0],
                                        c_re[0], c_im[0], nq)
        s5w = s5w_by_nq[nq]
        outs.append(_layer(
            x, bias, sink[0].astype(F32) * LOG2E,
            ln1[0].astype(F32)[None], w_in[0].astype(BF16),
            jnp.tile(q_gain[0].astype(F32), N_HEADS)[None],
            jnp.tile(k_gain[0].astype(F32), N_KV_HEADS)[None],
            s5w, d_skip[0].astype(F32).reshape(SLABS, 1, LANES),
            w_glu[0].astype(BF16), b_glu[0].astype(F32)[None], w_out[0].astype(BF16),
            ln2[0].astype(F32)[None], w_up[0].astype(BF16), w_down[0].astype(BF16)))
    return tuple(outs)
```

```python
import math

import numpy as np
import jax
import jax.numpy as jnp
from jax import lax
from jax.experimental import pallas as pl
from jax.experimental.pallas import tpu as pltpu

D_MODEL = 1024
N_HEADS = 16
N_KV_HEADS = 4
HEAD_DIM = 64
Q_PER_KV = N_HEADS // N_KV_HEADS
WINDOW = 128
BLOCK = 128
N_BUCKETS = 32
MAX_DISTANCE = 128
ATTN_Q = N_HEADS * HEAD_DIM
ATTN_KV = N_KV_HEADS * HEAD_DIM
NEG_INF = -1e30
SSM_WIDTH = D_MODEL
SSM_GROUP = 16
SSM_GROUPS = SSM_WIDTH // SSM_GROUP
SSM_STATE = 64
D_FF = 4 * D_MODEL
EPS = 1e-6
IN_COLS = ATTN_Q + 2 * ATTN_KV + SSM_WIDTH + 2 * D_MODEL

LANES = 128
SUBLANES = 8
SEG_LEVELS = 3
CHUNK = 16
SLABS = SSM_WIDTH // LANES
GROUPS_PER_SLAB = LANES // SSM_GROUP
SLAB_STATE = GROUPS_PER_SLAB * SSM_STATE
VMEM_LIMIT = 56 * 1024 * 1024
TQ_IN = 64
TQ_POST = 64
QBLOCKS = 8

LOG2E = math.log2(math.e)
Q_SCALE = HEAD_DIM ** -0.5 * LOG2E

BF16 = jnp.bfloat16
F32 = jnp.float32


def _const_spec(shape):
    nd = len(shape)
    return pl.BlockSpec(shape, lambda *_: (0,) * nd, pipeline_mode=pl.Buffered(1))


def _pack2(lo, hi):
    lo_bits = lax.bitcast_convert_type(lo.astype(BF16).astype(F32), jnp.uint32)
    hi_bits = lax.bitcast_convert_type(hi.astype(BF16).astype(F32), jnp.uint32)
    return (lo_bits >> 16) | hi_bits


def _unpack2(words):
    lo = lax.bitcast_convert_type(words << 16, F32)
    hi = lax.bitcast_convert_type(words & jnp.uint32(0xFFFF0000), F32)
    return lo, hi


def _in_proj_kernel(x_ref, ln1_ref, w_ref, qg_ref, kg_ref, q_ref, kv_ref, u_ref, g_ref):
    nseg, tq, _ = x_ref.shape
    tm = nseg * tq
    x = x_ref[...].reshape(tm, D_MODEL)
    ms = jnp.mean(x * x, axis=-1, keepdims=True)
    rinv = lax.rsqrt(ms + EPS)
    h = (x * ln1_ref[...]).astype(BF16)
    proj = lambda c0, width: jnp.dot(h, w_ref[:, c0:c0 + width], preferred_element_type=F32) * rinv

    low = lax.broadcasted_iota(jnp.int32, (1, LANES), 1) < HEAD_DIM

    def head_norm(t, gain):
        cols = []
        for j in range(t.shape[1] // LANES):
            blk = t[:, j * LANES:(j + 1) * LANES]
            sq = blk * blk
            m_lo = jnp.sum(jnp.where(low, sq, 0.0), axis=-1, keepdims=True) * (1.0 / HEAD_DIM)
            m_hi = jnp.sum(jnp.where(low, 0.0, sq), axis=-1, keepdims=True) * (1.0 / HEAD_DIM)
            cols.append(blk * jnp.where(low, lax.rsqrt(m_lo + EPS), lax.rsqrt(m_hi + EPS)))
        return jnp.concatenate(cols, axis=1) * gain

    c0 = 0
    q = (head_norm(proj(c0, ATTN_Q), qg_ref[...]) * Q_SCALE).astype(BF16)
    q_ref[...] = q.reshape(nseg, tq, ATTN_Q)
    c0 += ATTN_Q
    k = head_norm(proj(c0, ATTN_KV), kg_ref[...])
    c0 += ATTN_KV
    v = proj(c0, ATTN_KV)
    c0 += ATTN_KV
    kvx = jnp.concatenate([k, pltpu.roll(k, HEAD_DIM, axis=1), v, pltpu.roll(v, HEAD_DIM, axis=1)],
                          axis=1).astype(BF16)
    kv_ref[...] = kvx.reshape(nseg, tq, 4 * ATTN_KV)
    u = proj(c0, SSM_WIDTH)
    for p in range(SLABS // 2):
        words = _pack2(u[:, (2 * p) * LANES:(2 * p + 1) * LANES], u[:, (2 * p + 1) * LANES:(2 * p + 2) * LANES])
        for r in range(nseg):
            u_ref[p, pl.ds(r, tq, stride=nseg), :] = words[r * tq:(r + 1) * tq]
    c0 += SSM_WIDTH
    g = proj(c0, 2 * D_MODEL).astype(BF16)
    g_ref[...] = g.reshape(nseg, tq, 2 * D_MODEL)


def _seg_view(a, b, seq):
    return a.reshape(b, SUBLANES, seq // SUBLANES, a.shape[-1])


def _in_proj(x2, ln1, w_in, qg, kg, b, seq, tq):
    n = x2.shape[0]
    per_seq = seq // SUBLANES // tq
    seg = lambda c: pl.BlockSpec((None, SUBLANES, tq, c), lambda i: (i // per_seq, 0, i % per_seq, 0))
    interleaved = pl.BlockSpec((None, SLABS // 2, tq * SUBLANES, LANES),
                               lambda i: (i // per_seq, 0, i % per_seq, 0))
    seg_shape = lambda c, dt: jax.ShapeDtypeStruct((b, SUBLANES, seq // SUBLANES, c), dt)
    q, kvx, u, g = pl.pallas_call(
        _in_proj_kernel,
        grid=(b * per_seq,),
        in_specs=[seg(D_MODEL), _const_spec((1, D_MODEL)), _const_spec((D_MODEL, IN_COLS)),
                  _const_spec((1, ATTN_Q)), _const_spec((1, ATTN_KV))],
        out_specs=[seg(ATTN_Q), seg(4 * ATTN_KV), interleaved, seg(2 * D_MODEL)],
        out_shape=[seg_shape(ATTN_Q, BF16), seg_shape(4 * ATTN_KV, BF16),
                   jax.ShapeDtypeStruct((b, SLABS // 2, seq, LANES), jnp.uint32),
                   seg_shape(2 * D_MODEL, BF16)],
        compiler_params=pltpu.CompilerParams(dimension_semantics=("parallel",),
                                             vmem_limit_bytes=VMEM_LIMIT),
        name="in_proj",
    )(_seg_view(x2, b, seq), ln1, w_in, qg, kg)
    return q.reshape(n, ATTN_Q), kvx.reshape(n, 4 * ATTN_KV), u, g.reshape(n, 2 * D_MODEL)


_ROLLED_TILE = (0, 1, 1, 0)


def _attn_kernel(sink_ref, q_ref, kvp_ref, kvc_ref, kvn_ref, bias_ref, ones_ref, o_ref):
    n = pl.program_id(1)
    last = pl.num_programs(1) - 1
    kvx = jnp.concatenate([kvp_ref[0], kvc_ref[0], kvn_ref[0]], axis=0)
    low = lax.broadcasted_iota(jnp.int32, (1, LANES), 1) < HEAD_DIM
    zero = jnp.zeros((), BF16)

    def halves(base, kv):
        plain = kvx[:, base + (kv // 2) * LANES:base + (kv // 2 + 1) * LANES]
        r0 = base + ATTN_KV + _ROLLED_TILE[kv] * LANES
        rolled = kvx[:, r0:r0 + LANES]
        lo_src, hi_src = (plain, rolled) if kv % 2 == 0 else (rolled, plain)
        return jnp.where(low, lo_src, zero), jnp.where(low, zero, hi_src)

    top = lax.broadcasted_iota(jnp.int32, (2 * BLOCK, 1), 0) < BLOCK
    outs = [[] for _ in range(QBLOCKS)]
    for kv in range(N_KV_HEADS):
        ke, ko = halves(0, kv)
        ve, vo = halves(2 * ATTN_KV, kv)
        sink_e = jnp.where(top, sink_ref[4 * kv], sink_ref[4 * kv + 2])
        sink_o = jnp.where(top, sink_ref[4 * kv + 1], sink_ref[4 * kv + 3])
        for qb in range(QBLOCKS):
            rows = slice(qb * BLOCK, (qb + 3) * BLOCK)
            keys = jnp.concatenate([ke[rows], ko[rows]], axis=0)
            vals = jnp.concatenate([jnp.concatenate([ve[rows], vo[rows]], axis=0), ones_ref[...]],
                                   axis=1)
            if qb == 0:
                variant = jnp.where(n == 0, 0, 1)
            elif qb == QBLOCKS - 1:
                variant = jnp.where(n == last, 2, 1)
            else:
                variant = 1
            qrows = slice(qb * BLOCK, (qb + 1) * BLOCK)
            q2 = jnp.concatenate([q_ref[0, qrows, (2 * kv) * LANES:(2 * kv + 1) * LANES],
                                  q_ref[0, qrows, (2 * kv + 1) * LANES:(2 * kv + 2) * LANES]], axis=0)
            s = lax.dot_general(q2, keys, (((1,), (1,)), ((), ())), preferred_element_type=F32)
            s = s + bias_ref[variant, kv]
            se, so = s[:, :3 * BLOCK], s[:, 3 * BLOCK:]
            me = jnp.maximum(jnp.max(se, axis=-1, keepdims=True), sink_e)
            mo = jnp.maximum(jnp.max(so, axis=-1, keepdims=True), sink_o)
            p = jnp.concatenate([jnp.exp2(se - me), jnp.exp2(so - mo)], axis=1).astype(BF16)
            od = jnp.dot(p, vals, preferred_element_type=F32)
            den = od[:, LANES:] + jnp.where(low, jnp.exp2(sink_e - me), jnp.exp2(sink_o - mo))
            o = od[:, :LANES] / den
            outs[qb] += [o[:BLOCK], o[BLOCK:]]
    for qb in range(QBLOCKS):
        o_ref[0, qb * BLOCK:(qb + 1) * BLOCK, :] = jnp.concatenate(outs[qb], axis=-1).astype(BF16)


def _attn(q, kvx, bias, ones, sink):
    b, s, _ = q.shape
    nb = s // BLOCK
    assert nb % QBLOCKS == 0 and QBLOCKS >= 2
    steps = nb // QBLOCKS
    edge_spec = lambda f: pl.BlockSpec((1, BLOCK, 4 * ATTN_KV), f)
    wide = lambda c: pl.BlockSpec((1, QBLOCKS * BLOCK, c), lambda bi, n, *_: (bi, n, 0))
    prev = lambda bi, n, *_: (bi, jnp.maximum(n * QBLOCKS - 1, 0), 0)
    nxt = lambda bi, n, *_: (bi, jnp.minimum((n + 1) * QBLOCKS, nb - 1), 0)
    grid_spec = pltpu.PrefetchScalarGridSpec(
        num_scalar_prefetch=1,
        grid=(b, steps),
        in_specs=[wide(ATTN_Q), edge_spec(prev), wide(4 * ATTN_KV), edge_spec(nxt),
                  _const_spec(bias.shape), _const_spec(ones.shape)],
        out_specs=wide(ATTN_Q),
    )
    return pl.pallas_call(
        _attn_kernel,
        grid_spec=grid_spec,
        out_shape=jax.ShapeDtypeStruct((b, s, ATTN_Q), BF16),
        compiler_params=pltpu.CompilerParams(dimension_semantics=("parallel", "arbitrary"),
                                             vmem_limit_bytes=VMEM_LIMIT),
        name="attn",
    )(sink, q, kvx, kvx, kvx, bias, ones)


def _s5_slab(nq, load_u, mt_ref, enc_ref, dect_ref, coef_ref, dsk_ref, store_y):
    tile_rows = lambda x, q: x[q * SUBLANES:(q + 1) * SUBLANES]
    tile = lambda q, s: pl.ds((q * CHUNK + s) * SUBLANES, SUBLANES)

    xs = [jnp.concatenate([load_u(tile(q, s)) for q in range(nq)], axis=0) for s in range(CHUNK)]
    xst = [x.astype(BF16).T for x in xs]

    yts, pair_states = [], []
    for g in range(GROUPS_PER_SLAB):
        ugt = jnp.concatenate([xt[g * SSM_GROUP:(g + 1) * SSM_GROUP] for xt in xst], axis=0)
        ug = ugt.T
        yts.append(jnp.dot(mt_ref[g], ugt, preferred_element_type=F32))
        part = jnp.dot(ug, enc_ref[g], preferred_element_type=F32)
        if g % 2 == 0:
            pair_states.append(part)
        else:
            pair_states[-1] = pair_states[-1] + part

    def state(d, ri):
        c0 = (2 * d + ri) * LANES
        return jnp.concatenate([p[:, c0:c0 + LANES] for p in pair_states], axis=1)

    def coef(d, row):
        return coef_ref[d, 0, row:row + 1, :], coef_ref[d, 1, row:row + 1, :]

    seg = lax.broadcasted_iota(jnp.int32, (SUBLANES, 1), 0)

    def seg_shift(x, sh, forward):
        if forward:
            return jnp.where(seg >= sh, pltpu.roll(x, sh, axis=0), 0.0)
        return jnp.where(seg < SUBLANES - sh, pltpu.roll(x, SUBLANES - sh, axis=0), 0.0)

    carried = []
    for d in range(2):
        forward = d == 0
        s_re, s_im = state(d, 0), state(d, 1)
        a_re, a_im = coef(d, 0)
        order = list(range(nq)) if forward else list(range(nq - 1, -1, -1))
        t_re, t_im = {}, {}
        for n, q in enumerate(order):
            cur_re, cur_im = tile_rows(s_re, q), tile_rows(s_im, q)
            if n:
                p_re, p_im = t_re[order[n - 1]], t_im[order[n - 1]]
                cur_re = cur_re + (a_re * p_re - a_im * p_im)
                cur_im = cur_im + (a_re * p_im + a_im * p_re)
            t_re[q], t_im[q] = cur_re, cur_im
        e_re, e_im = t_re[order[-1]], t_im[order[-1]]
        for lv in range(SEG_LEVELS):
            b_re, b_im = coef(d, 1 + lv)
            h_re, h_im = seg_shift(e_re, 1 << lv, forward), seg_shift(e_im, 1 << lv, forward)
            e_re, e_im = e_re + (b_re * h_re - b_im * h_im), e_im + (b_re * h_im + b_im * h_re)
        c_re, c_im = seg_shift(e_re, 1, forward), seg_shift(e_im, 1, forward)
        x_re, x_im = [], []
        for q in range(nq):
            n = order.index(q)
            w_re, w_im = coef(d, 1 + SEG_LEVELS + n)
            r_re, r_im = w_re * c_re - w_im * c_im, w_re * c_im + w_im * c_re
            if n:
                r_re, r_im = r_re + t_re[order[n - 1]], r_im + t_im[order[n - 1]]
            x_re.append(r_re)
            x_im.append(r_im)
        carried += [jnp.concatenate(x_re, axis=0), jnp.concatenate(x_im, axis=0)]

    for g in range(GROUPS_PER_SLAB):
        c0 = (g // 2) * LANES
        xpair = jnp.concatenate([c[:, c0:c0 + LANES] for c in carried], axis=1).astype(BF16)
        yts[g] = yts[g] + lax.dot_general(dect_ref[g], xpair, (((1,), (1,)), ((), ())),
                                          preferred_element_type=F32)
    dsk = dsk_ref[...]
    for t in range(CHUNK):
        yt = jnp.concatenate([y[t * SSM_GROUP:(t + 1) * SSM_GROUP] for y in yts], axis=0)
        y = yt.T + dsk * xs[t]
        for q in range(nq):
            store_y(tile(q, t), tile_rows(y, q))


def _s5_kernel(u_ref, mt_ref, enc_ref, dect_ref, coef_ref, dsk_ref, y_ref, y_lo_ref):
    nq = u_ref.shape[0] // (CHUNK * SUBLANES)

    def store_lo(rows, y):
        y_lo_ref[rows, :] = y

    def store_both(rows, y):
        y_ref[rows, :] = _pack2(y_lo_ref[rows, :], y)

    for i, store in enumerate((store_lo, store_both)):
        _s5_slab(nq, lambda rows, i=i: _unpack2(u_ref[rows, :])[i], mt_ref.at[i], enc_ref.at[i],
                 dect_ref.at[i], coef_ref.at[i], dsk_ref.at[i], store)


def _s5(u4, mt, enc, dect, coef, dsk):
    b, planes, seq, _ = u4.shape
    per_plane = lambda a: pl.BlockSpec((2,) + a.shape[1:], lambda p, bi: (p,) + (0,) * (a.ndim - 1),
                                       pipeline_mode=pl.Buffered(1))
    io = pl.BlockSpec((None, None, seq, LANES), lambda p, bi: (bi, p, 0, 0))
    return pl.pallas_call(
        _s5_kernel,
        grid=(planes, b),
        in_specs=[io, per_plane(mt), per_plane(enc), per_plane(dect), per_plane(coef), per_plane(dsk)],
        out_specs=io,
        out_shape=jax.ShapeDtypeStruct(u4.shape, jnp.uint32),
        scratch_shapes=[pltpu.VMEM((seq, LANES), F32)],
        compiler_params=pltpu.CompilerParams(dimension_semantics=("parallel", "arbitrary"),
                                             vmem_limit_bytes=VMEM_LIMIT),
        name="s5",
    )(u4, mt, enc, dect, coef, dsk)


def _cmul(ar, ai, br, bi):
    return ar * br - ai * bi, ar * bi + ai * br


def _cpowers(a_re, a_im, n):
    p_re, p_im = jnp.ones_like(a_re)[None], jnp.zeros_like(a_im)[None]
    s_re, s_im = a_re, a_im
    while p_re.shape[0] < n:
        h_re, h_im = _cmul(p_re, p_im, s_re[None], s_im[None])
        p_re, p_im = jnp.concatenate([p_re, h_re], 0), jnp.concatenate([p_im, h_im], 0)
        s_re, s_im = _cmul(s_re, s_im, s_re, s_im)
    return p_re[:n], p_im[:n]


def _s5_gen_kernel(bm_ref, cm_ref, apow_ref, enc_ref, dect_ref):
    row_group = lax.broadcasted_iota(jnp.int32, (LANES, SLAB_STATE), 0) // SSM_GROUP
    col_group = lax.broadcasted_iota(jnp.int32, (LANES, SLAB_STATE), 1) // SSM_STATE

    def block_diag(ref, d, ri):
        wide = jnp.concatenate([ref[0, d, ri]] * (SLAB_STATE // LANES), axis=1)
        return jnp.where(row_group == col_group, wide, 0.0)

    bm = [[block_diag(bm_ref, d, ri) for ri in range(2)] for d in range(2)]
    cm = [[block_diag(cm_ref, d, ri) for ri in range(2)] for d in range(2)]

    def put(ref, d, ri, step, tile):
        tile = tile.astype(BF16)
        c0 = (2 * d + ri) * LANES
        for g in range(GROUPS_PER_SLAB):
            p0 = (g // 2) * LANES
            ref[0, g, step * SSM_GROUP:(step + 1) * SSM_GROUP, c0:c0 + LANES] = (
                tile[g * SSM_GROUP:(g + 1) * SSM_GROUP, p0:p0 + LANES])

    for d in range(2):
        for k in range(CHUNK + 1):
            a_re, a_im = apow_ref[0, d, 0, k:k + 1, :], apow_ref[0, d, 1, k:k + 1, :]
            if k < CHUNK:
                s = CHUNK - 1 - k if d == 0 else k
                e_re, e_im = _cmul(bm[d][0], bm[d][1], a_re, a_im)
                put(enc_ref, d, 0, s, e_re)
                put(enc_ref, d, 1, s, e_im)
            if k >= 1:
                t = k - 1 if d == 0 else CHUNK - k
                f_re, f_im = _cmul(cm[d][0], cm[d][1], a_re, a_im)
                put(dect_ref, d, 0, t, f_re)
                put(dect_ref, d, 1, t, -f_im)


def _s5_gen(bm, cm, apow):
    small = lambda a: pl.BlockSpec((1,) + a.shape[1:], lambda sl: (sl,) + (0,) * (a.ndim - 1))
    shape = (SLABS, GROUPS_PER_SLAB, CHUNK * SSM_GROUP, 4 * LANES)
    out = pl.BlockSpec((1,) + shape[1:], lambda sl: (sl, 0, 0, 0))
    return pl.pallas_call(
        _s5_gen_kernel,
        grid=(SLABS,),
        in_specs=[small(bm), small(cm), small(apow)],
        out_specs=[out, out],
        out_shape=[jax.ShapeDtypeStruct(shape, BF16), jax.ShapeDtypeStruct(shape, BF16)],
        compiler_params=pltpu.CompilerParams(dimension_semantics=("parallel",),
                                             vmem_limit_bytes=VMEM_LIMIT),
        name="s5_gen",
    )(bm, cm, apow)


def _slab_tiles(t_re, t_im):
    t = jnp.stack([t_re, t_im], 1).astype(F32)
    t = t.reshape(2, 2, SLABS, LANES, SSM_STATE)
    t = jnp.concatenate([t, t], axis=-1)
    return jnp.transpose(t, (2, 0, 1, 3, 4))


def _s5_weights(lam_re, lam_im, log_dt, b_re, b_im, c_re, c_im, nq):
    G, C = SSM_GROUPS, SSM_GROUP
    dt = jnp.exp(log_dt.astype(F32))[..., None]
    lr, li = lam_re.astype(F32), lam_im.astype(F32)
    mag = jnp.exp(lr * dt)
    ang = li * dt
    a_re, a_im = mag * jnp.cos(ang), mag * jnp.sin(ang)
    den = lr * lr + li * li
    nr, ni = a_re - 1.0, a_im
    z_re = (nr * lr + ni * li) / den
    z_im = (ni * lr - nr * li) / den
    br, bi = b_re.astype(F32), b_im.astype(F32)
    bb_re = z_re[..., None] * br - z_im[..., None] * bi
    bb_im = z_re[..., None] * bi + z_im[..., None] * br
    cr = jnp.swapaxes(c_re.astype(F32), -1, -2)
    ci = jnp.swapaxes(c_im.astype(F32), -1, -2)

    pw_re, pw_im = _cpowers(a_re, a_im, CHUNK + 1)

    ab_re, ab_im = _cmul(pw_re[:CHUNK, ..., None], pw_im[:CHUNK, ..., None], bb_re[None], bb_im[None])
    kern = jnp.einsum('dgpc,kdgpe->kdgce', jnp.concatenate([cr, -ci], 2),
                      jnp.concatenate([ab_re, ab_im], 3), precision='highest')
    lag_tab = jnp.concatenate([kern[::-1, 0][:-1], (kern[0, 0] + kern[0, 1])[None], kern[1:, 1]], 0)
    lag_tab = jnp.transpose(lag_tab, (1, 2, 0, 3)).reshape(G, C, (2 * CHUNK - 1) * C)
    mt = jnp.stack([lag_tab[:, :, (CHUNK - 1 - t) * C:(CHUNK - 1 - t) * C + CHUNK * C]
                    for t in range(CHUNK)], 1).reshape(SLABS, GROUPS_PER_SLAB, CHUNK * C, CHUNK * C)

    apow = jnp.stack([pw_re, pw_im], 0).reshape(2, CHUNK + 1, 2, SLABS, SLAB_STATE)
    apow = jnp.transpose(apow, (3, 2, 0, 1, 4))
    enc, dect = _s5_gen(_slab_tiles(jnp.swapaxes(bb_re, -1, -2), jnp.swapaxes(bb_im, -1, -2)),
                        _slab_tiles(c_re, c_im), apow)

    s_re, s_im = pw_re[CHUNK], pw_im[CHUNK]
    w_re, w_im = _cpowers(s_re, s_im, nq)
    q_re, q_im = _cmul(w_re[nq - 1], w_im[nq - 1], s_re, s_im)
    rows_re, rows_im = [s_re], [s_im]
    for _ in range(SEG_LEVELS):
        rows_re.append(q_re)
        rows_im.append(q_im)
        q_re, q_im = _cmul(q_re, q_im, q_re, q_im)
    c_all = jnp.stack([jnp.concatenate([jnp.stack(rows_re, 0), w_re], 0),
                       jnp.concatenate([jnp.stack(rows_im, 0), w_im], 0)], 0)
    n_rows = c_all.shape[1]
    c_all = c_all.reshape(2, n_rows, 2, SLABS, SLAB_STATE)
    coef = jnp.transpose(c_all, (3, 2, 0, 1, 4))
    return mt.astype(BF16), enc, dect, coef


def _post_kernel(x_ref, ya_ref, ys_ref, g_ref, wglu_ref, bglu_ref, wout_ref, ln2_ref,
                 wup_ref, wdown_ref, o_ref):
    nseg, tq, _ = x_ref.shape
    tm = nseg * tq
    ys = jnp.concatenate(
        [jnp.concatenate([half for p in range(SLABS // 2)
                          for half in _unpack2(ys_ref[p, pl.ds(r, tq, stride=nseg), :])], axis=1)
         for r in range(nseg)], axis=0)
    y = jax.nn.gelu(ys)
    z = jnp.dot(y.astype(BF16), wglu_ref[...], preferred_element_type=F32) + bglu_ref[...]
    y = y * jax.nn.sigmoid(z)
    g = g_ref[...].reshape(tm, 2 * D_MODEL).astype(F32)
    ya = ya_ref[...].reshape(tm, D_MODEL).astype(F32)
    mixed = jax.nn.sigmoid(g[:, :D_MODEL]) * ya + jax.nn.sigmoid(g[:, D_MODEL:]) * y
    x = x_ref[...].reshape(tm, D_MODEL)
    x = x + jnp.dot(mixed.astype(BF16), wout_ref[...], preferred_element_type=F32)
    ms = jnp.mean(x * x, axis=-1, keepdims=True)
    h = (x * ln2_ref[...]).astype(BF16)
    up = jnp.dot(h, wup_ref[...], preferred_element_type=F32) * lax.rsqrt(ms + EPS)
    act = jnp.square(jnp.maximum(up, 0.0)).astype(BF16)
    out = x + jnp.dot(act, wdown_ref[...], preferred_element_type=F32)
    o_ref[...] = out.reshape(nseg, tq, D_MODEL)


def _post(x2, ya, ys4, g, w_glu, b_glu, w_out, ln2, w_up, w_down, tq):
    b, _, seq, _ = ys4.shape
    per_seq = seq // SUBLANES // tq
    seg = lambda c: pl.BlockSpec((None, SUBLANES, tq, c), lambda i: (i // per_seq, 0, i % per_seq, 0))
    interleaved = pl.BlockSpec((None, SLABS // 2, tq * SUBLANES, LANES),
                               lambda i: (i // per_seq, 0, i % per_seq, 0))
    out = pl.pallas_call(
        _post_kernel,
        grid=(b * per_seq,),
        in_specs=[seg(D_MODEL), seg(D_MODEL), interleaved, seg(2 * D_MODEL),
                  _const_spec((D_MODEL, D_MODEL)), _const_spec((1, D_MODEL)),
                  _const_spec((D_MODEL, D_MODEL)), _const_spec((1, D_MODEL)),
                  _const_spec((D_MODEL, D_FF)), _const_spec((D_FF, D_MODEL))],
        out_specs=seg(D_MODEL),
        out_shape=jax.ShapeDtypeStruct((b, SUBLANES, seq // SUBLANES, D_MODEL), F32),
        compiler_params=pltpu.CompilerParams(dimension_semantics=("parallel",),
                                             vmem_limit_bytes=VMEM_LIMIT),
        name="post",
    )(_seg_view(x2, b, seq), _seg_view(ya, b, seq), ys4, _seg_view(g, b, seq),
      w_glu, b_glu, w_out, ln2, w_up, w_down)
    return out.reshape(b, seq, D_MODEL)


def _t5_bucket(rel):
    nb = N_BUCKETS // 2
    ret = (rel > 0).astype(np.int32) * nb
    n = np.abs(rel)
    max_exact = nb // 2
    n_safe = np.maximum(n, 1).astype(np.float32)
    large = max_exact + (np.log(n_safe / max_exact) / math.log(MAX_DISTANCE / max_exact)
                         * (nb - max_exact)).astype(np.int32)
    large = np.minimum(large, nb - 1)
    return (ret + np.where(n < max_exact, n, large)).astype(np.int32)


def _band_bias(rel_table):
    qi = np.arange(BLOCK)[:, None]
    kj = np.arange(3 * BLOCK)[None, :]
    rel = kj - BLOCK - qi
    onehot = (np.arange(N_BUCKETS)[:, None] == _t5_bucket(rel).reshape(1, -1)).astype(np.float32)
    bias = jnp.dot(rel_table.astype(F32).T * LOG2E, jnp.asarray(onehot), precision='highest')
    bias = bias.reshape(N_HEADS, BLOCK, 3 * BLOCK)
    in_band = np.abs(rel) <= WINDOW
    keep = np.stack([in_band & (kj >= BLOCK), in_band, in_band & (kj < 2 * BLOCK)], 0)
    bias = jnp.where(jnp.asarray(keep)[:, None], bias[None], NEG_INF)
    bias = bias.reshape(3, N_KV_HEADS, 2, 2, BLOCK, 3 * BLOCK)
    return jnp.transpose(bias, (0, 1, 2, 4, 3, 5)).reshape(3, N_KV_HEADS, 2 * BLOCK, 6 * BLOCK)


def _sum_mat():
    first = np.arange(6 * BLOCK)[:, None] < 3 * BLOCK
    low = np.arange(LANES)[None, :] < HEAD_DIM
    return jnp.asarray(first == low, BF16)


def _layer(x, bias, sink, ln1, w_in, qg, kg, s5w, dsk, w_glu, b_glu, w_out, ln2, w_up, w_down):
    b, s, _ = x.shape
    x2 = x.reshape(b * s, D_MODEL)
    rows_per_seg = s // SUBLANES
    q, kvx, u4, g = _in_proj(x2, ln1, w_in, qg, kg, b, s, min(TQ_IN, rows_per_seg))
    ya = _attn(q.reshape(b, s, ATTN_Q), kvx.reshape(b, s, 4 * ATTN_KV), bias, _sum_mat(), sink)
    ys4 = _s5(u4, *s5w, dsk)
    return _post(x2, ya.reshape(b * s, ATTN_Q), ys4, g, w_glu, b_glu, w_out, ln2, w_up, w_down,
                 min(TQ_POST, rows_per_seg))


def kernel(x_prompt, x_sample, rel_table, ln1, w_in, q_gain, k_gain, sink, lam_re, lam_im, log_dt,
           b_re, b_im, c_re, c_im, d_skip, w_glu, b_glu, w_out, ln2, w_up, w_down):
    assert ln1.shape[0] == 1, "single layer"
    bias = _band_bias(rel_table)
    outs = []
    s5w_by_nq = {}
    for x in (x_prompt, x_sample):
        seq = x.shape[1]
        nq = seq // (CHUNK * SUBLANES)
        assert nq * CHUNK * SUBLANES == seq and seq % BLOCK == 0
        if nq not in s5w_by_nq:
            s5w_by_nq[nq] = _s5_weights(lam_re[0], lam_im[0], log_dt[0], b_re[0], b_im[0],
                                        c_re[0], c_im[0], nq)
        s5w = s5w_by_nq[nq]
        outs.append(_layer(
            x, bias, sink[0].astype(F32) * LOG2E,
            ln1[0].astype(F32)[None], w_in[0].astype(BF16),
            jnp.tile(q_gain[0].astype(F32), N_HEADS)[None],
            jnp.tile(k_gain[0].astype(F32), N_KV_HEADS)[None],
            s5w, d_skip[0].astype(F32).reshape(SLABS, 1, LANES),
            w_glu[0].astype(BF16), b_glu[0].astype(F32)[None], w_out[0].astype(BF16),
            ln2[0].astype(F32)[None], w_up[0].astype(BF16), w_down[0].astype(BF16)))
    return tuple(outs)
```

```python
import math

import numpy as np
import jax
import jax.numpy as jnp
from jax import lax
from jax.experimental import pallas as pl
from jax.experimental.pallas import tpu as pltpu

D_MODEL = 1024
N_HEADS = 16
N_KV_HEADS = 4
HEAD_DIM = 64
Q_PER_KV = N_HEADS // N_KV_HEADS
WINDOW = 128
BLOCK = 128
N_BUCKETS = 32
MAX_DISTANCE = 128
ATTN_Q = N_HEADS * HEAD_DIM
ATTN_KV = N_KV_HEADS * HEAD_DIM
NEG_INF = -1e30
SSM_WIDTH = D_MODEL
SSM_GROUP = 16
SSM_GROUPS = SSM_WIDTH // SSM_GROUP
SSM_STATE = 64
D_FF = 4 * D_MODEL
EPS = 1e-6
IN_COLS = ATTN_Q + 2 * ATTN_KV + SSM_WIDTH + 2 * D_MODEL

LANES = 128
SUBLANES = 8
SEG_LEVELS = 3
CHUNK = 16
SLABS = SSM_WIDTH // LANES
GROUPS_PER_SLAB = LANES // SSM_GROUP
SLAB_STATE = GROUPS_PER_SLAB * SSM_STATE
VMEM_LIMIT = 56 * 1024 * 1024
TQ_IN = 64
TQ_POST = 64
QBLOCKS = 8

LOG2E = math.log2(math.e)
Q_SCALE = HEAD_DIM ** -0.5 * LOG2E

BF16 = jnp.bfloat16
F32 = jnp.float32


def _const_spec(shape):
    nd = len(shape)
    return pl.BlockSpec(shape, lambda *_: (0,) * nd, pipeline_mode=pl.Buffered(1))


def _pack2(lo, hi):
    lo_bits = lax.bitcast_convert_type(lo.astype(BF16).astype(F32), jnp.uint32)
    hi_bits = lax.bitcast_convert_type(hi.astype(BF16).astype(F32), jnp.uint32)
    return (lo_bits >> 16) | hi_bits


def _unpack2(words):
    lo = lax.bitcast_convert_type(words << 16, F32)
    hi = lax.bitcast_convert_type(words & jnp.uint32(0xFFFF0000), F32)
    return lo, hi


def _in_proj_kernel(x_ref, ln1_ref, w_ref, qg_ref, kg_ref, q_ref, kv_ref, u_ref, g_ref):
    nseg, tq, _ = x_ref.shape
    tm = nseg * tq
    x = x_ref[...].reshape(tm, D_MODEL)
    ms = jnp.mean(x * x, axis=-1, keepdims=True)
    rinv = lax.rsqrt(ms + EPS)
    h = (x * ln1_ref[...]).astype(BF16)
    proj = lambda c0, width: jnp.dot(h, w_ref[:, c0:c0 + width], preferred_element_type=F32) * rinv

    low = lax.broadcasted_iota(jnp.int32, (1, LANES), 1) < HEAD_DIM

    def head_norm(t, gain):
        cols = []
        for j in range(t.shape[1] // LANES):
            blk = t[:, j * LANES:(j + 1) * LANES]
            sq = blk * blk
            m_lo = jnp.sum(jnp.where(low, sq, 0.0), axis=-1, keepdims=True) * (1.0 / HEAD_DIM)
            m_hi = jnp.sum(jnp.where(low, 0.0, sq), axis=-1, keepdims=True) * (1.0 / HEAD_DIM)
            cols.append(blk * jnp.where(low, lax.rsqrt(m_lo + EPS), lax.rsqrt(m_hi + EPS)))
        return jnp.concatenate(cols, axis=1) * gain

    c0 = 0
    q = (head_norm(proj(c0, ATTN_Q), qg_ref[...]) * Q_SCALE).astype(BF16)
    q_ref[...] = q.reshape(nseg, tq, ATTN_Q)
    c0 += ATTN_Q
    k = head_norm(proj(c0, ATTN_KV), kg_ref[...])
    c0 += ATTN_KV
    v = proj(c0, ATTN_KV)
    c0 += ATTN_KV
    kv_ref[...] = jnp.concatenate([k, v], axis=1).astype(BF16).reshape(nseg, tq, 2 * ATTN_KV)
    u = proj(c0, SSM_WIDTH)
    for p in range(SLABS // 2):
        words = _pack2(u[:, (2 * p) * LANES:(2 * p + 1) * LANES], u[:, (2 * p + 1) * LANES:(2 * p + 2) * LANES])
        for r in range(nseg):
            u_ref[p, pl.ds(r, tq, stride=nseg), :] = words[r * tq:(r + 1) * tq]
    c0 += SSM_WIDTH
    g = proj(c0, 2 * D_MODEL).astype(BF16)
    g_ref[...] = g.reshape(nseg, tq, 2 * D_MODEL)


def _seg_view(a, b, seq):
    return a.reshape(b, SUBLANES, seq // SUBLANES, a.shape[-1])


def _in_proj(x2, ln1, w_in, qg, kg, b, seq, tq):
    n = x2.shape[0]
    per_seq = seq // SUBLANES // tq
    seg = lambda c: pl.BlockSpec((None, SUBLANES, tq, c), lambda i: (i // per_seq, 0, i % per_seq, 0))
    interleaved = pl.BlockSpec((None, SLABS // 2, tq * SUBLANES, LANES),
                               lambda i: (i // per_seq, 0, i % per_seq, 0))
    seg_shape = lambda c, dt: jax.ShapeDtypeStruct((b, SUBLANES, seq // SUBLANES, c), dt)
    q, kvx, u, g = pl.pallas_call(
        _in_proj_kernel,
        grid=(b * per_seq,),
        in_specs=[seg(D_MODEL), _const_spec((1, D_MODEL)), _const_spec((D_MODEL, IN_COLS)),
                  _const_spec((1, ATTN_Q)), _const_spec((1, ATTN_KV))],
        out_specs=[seg(ATTN_Q), seg(2 * ATTN_KV), interleaved, seg(2 * D_MODEL)],
        out_shape=[seg_shape(ATTN_Q, BF16), seg_shape(2 * ATTN_KV, BF16),
                   jax.ShapeDtypeStruct((b, SLABS // 2, seq, LANES), jnp.uint32),
                   seg_shape(2 * D_MODEL, BF16)],
        compiler_params=pltpu.CompilerParams(dimension_semantics=("parallel",),
                                             vmem_limit_bytes=VMEM_LIMIT),
        name="in_proj",
    )(_seg_view(x2, b, seq), ln1, w_in, qg, kg)
    return q.reshape(n, ATTN_Q), kvx.reshape(n, 2 * ATTN_KV), u, g.reshape(n, 2 * D_MODEL)


_PAIRED_HEADS = tuple(8 * m + 4 * half + j for m in range(2) for j in range(4) for half in range(2))
PAIRS = N_HEADS // 4


def _attn_kernel(sink_ref, q_ref, kvp_ref, kvc_ref, kvn_ref, bias_ref, ones_ref, o_ref):
    n = pl.program_id(1)
    last = pl.num_programs(1) - 1
    kv = jnp.concatenate([kvp_ref[0], kvc_ref[0], kvn_ref[0]], axis=0)
    low = lax.broadcasted_iota(jnp.int32, (1, LANES), 1) < HEAD_DIM
    zero = jnp.zeros((), BF16)
    block_of_row = lax.broadcasted_iota(jnp.int32, (PAIRS * BLOCK, 1), 0) // BLOCK

    def per_pair(vals4):
        col = jnp.full((PAIRS * BLOCK, 1), vals4[-1], F32)
        for j in range(PAIRS - 2, -1, -1):
            col = jnp.where(block_of_row == j, vals4[j], col)
        return col

    outs = [[None] * (N_HEADS // 2) for _ in range(QBLOCKS)]
    for m in range(N_KV_HEADS // 2):
        k_tile = kv[:, m * LANES:(m + 1) * LANES]
        v_tile = kv[:, ATTN_KV + m * LANES:ATTN_KV + (m + 1) * LANES]
        ke, ko = jnp.where(low, k_tile, zero), jnp.where(low, zero, k_tile)
        ve, vo = jnp.where(low, v_tile, zero), jnp.where(low, zero, v_tile)
        sink_e = per_pair([sink_ref[8 * m + j] for j in range(PAIRS)])
        sink_o = per_pair([sink_ref[8 * m + 4 + j] for j in range(PAIRS)])
        for qb in range(QBLOCKS):
            rows = slice(qb * BLOCK, (qb + 3) * BLOCK)
            keys = jnp.concatenate([ke[rows], ko[rows]], axis=0)
            vals = jnp.concatenate([jnp.concatenate([ve[rows], vo[rows]], axis=0), ones_ref[...]],
                                   axis=1)
            if qb == 0:
                variant = jnp.where(n == 0, 0, 1)
            elif qb == QBLOCKS - 1:
                variant = jnp.where(n == last, 2, 1)
            else:
                variant = 1
            qrows = slice(qb * BLOCK, (qb + 1) * BLOCK)
            q4 = jnp.concatenate([q_ref[0, qrows, (PAIRS * m + j) * LANES:(PAIRS * m + j + 1) * LANES]
                                  for j in range(PAIRS)], axis=0)
            s = lax.dot_general(q4, keys, (((1,), (1,)), ((), ())), preferred_element_type=F32)
            s = s + bias_ref[variant, m]
            se, so = s[:, :3 * BLOCK], s[:, 3 * BLOCK:]
            me = jnp.maximum(jnp.max(se, axis=-1, keepdims=True), sink_e)
            mo = jnp.maximum(jnp.max(so, axis=-1, keepdims=True), sink_o)
            p = jnp.concatenate([jnp.exp2(se - me), jnp.exp2(so - mo)], axis=1).astype(BF16)
            od = jnp.dot(p, vals, preferred_element_type=F32)
            den = od[:, LANES:] + jnp.where(low, jnp.exp2(sink_e - me), jnp.exp2(sink_o - mo))
            o = od[:, :LANES] / den
            t = [o[j * BLOCK:(j + 1) * BLOCK] for j in range(PAIRS)]
            swap = lambda x: pltpu.roll(x, HEAD_DIM, axis=1)
            outs[qb][PAIRS * m + 0] = jnp.where(low, t[0], swap(t[1]))
            outs[qb][PAIRS * m + 1] = jnp.where(low, t[2], swap(t[3]))
            outs[qb][PAIRS * m + 2] = jnp.where(low, swap(t[0]), t[1])
            outs[qb][PAIRS * m + 3] = jnp.where(low, swap(t[2]), t[3])
    for qb in range(QBLOCKS):
        o_ref[0, qb * BLOCK:(qb + 1) * BLOCK, :] = jnp.concatenate(outs[qb], axis=-1).astype(BF16)


def _attn(q, kvx, bias, ones, sink):
    b, s, _ = q.shape
    nb = s // BLOCK
    assert nb % QBLOCKS == 0 and QBLOCKS >= 2
    steps = nb // QBLOCKS
    edge_spec = lambda f: pl.BlockSpec((1, BLOCK, 2 * ATTN_KV), f)
    wide = lambda c: pl.BlockSpec((1, QBLOCKS * BLOCK, c), lambda bi, n, *_: (bi, n, 0))
    prev = lambda bi, n, *_: (bi, jnp.maximum(n * QBLOCKS - 1, 0), 0)
    nxt = lambda bi, n, *_: (bi, jnp.minimum((n + 1) * QBLOCKS, nb - 1), 0)
    grid_spec = pltpu.PrefetchScalarGridSpec(
        num_scalar_prefetch=1,
        grid=(b, steps),
        in_specs=[wide(ATTN_Q), edge_spec(prev), wide(2 * ATTN_KV), edge_spec(nxt),
                  _const_spec(bias.shape), _const_spec(ones.shape)],
        out_specs=wide(ATTN_Q),
    )
    return pl.pallas_call(
        _attn_kernel,
        grid_spec=grid_spec,
        out_shape=jax.ShapeDtypeStruct((b, s, ATTN_Q), BF16),
        compiler_params=pltpu.CompilerParams(dimension_semantics=("parallel", "arbitrary"),
                                             vmem_limit_bytes=VMEM_LIMIT),
        name="attn",
    )(sink, q, kvx, kvx, kvx, bias, ones)


def _s5_slab(nq, load_u, mt_ref, enc_ref, dect_ref, coef_ref, dsk_ref, store_y):
    tile_rows = lambda x, q: x[q * SUBLANES:(q + 1) * SUBLANES]
    tile = lambda q, s: pl.ds((q * CHUNK + s) * SUBLANES, SUBLANES)

    xs = [jnp.concatenate([load_u(tile(q, s)) for q in range(nq)], axis=0) for s in range(CHUNK)]
    xst = [x.astype(BF16).T for x in xs]

    yts, pair_states = [], []
    for g in range(GROUPS_PER_SLAB):
        ugt = jnp.concatenate([xt[g * SSM_GROUP:(g + 1) * SSM_GROUP] for xt in xst], axis=0)
        ug = ugt.T
        yts.append(jnp.dot(mt_ref[g], ugt, preferred_element_type=F32))
        part = jnp.dot(ug, enc_ref[g], preferred_element_type=F32)
        if g % 2 == 0:
            pair_states.append(part)
        else:
            pair_states[-1] = pair_states[-1] + part

    def state(d, ri):
        c0 = (2 * d + ri) * LANES
        return jnp.concatenate([p[:, c0:c0 + LANES] for p in pair_states], axis=1)

    def coef(d, row):
        return coef_ref[d, 0, row:row + 1, :], coef_ref[d, 1, row:row + 1, :]

    seg = lax.broadcasted_iota(jnp.int32, (SUBLANES, 1), 0)

    def seg_shift(x, sh, forward):
        if forward:
            return jnp.where(seg >= sh, pltpu.roll(x, sh, axis=0), 0.0)
        return jnp.where(seg < SUBLANES - sh, pltpu.roll(x, SUBLANES - sh, axis=0), 0.0)

    carried = []
    for d in range(2):
        forward = d == 0
        s_re, s_im = state(d, 0), state(d, 1)
        a_re, a_im = coef(d, 0)
        order = list(range(nq)) if forward else list(range(nq - 1, -1, -1))
        t_re, t_im = {}, {}
        for n, q in enumerate(order):
            cur_re, cur_im = tile_rows(s_re, q), tile_rows(s_im, q)
            if n:
                p_re, p_im = t_re[order[n - 1]], t_im[order[n - 1]]
                cur_re = cur_re + (a_re * p_re - a_im * p_im)
                cur_im = cur_im + (a_re * p_im + a_im * p_re)
            t_re[q], t_im[q] = cur_re, cur_im
        e_re, e_im = t_re[order[-1]], t_im[order[-1]]
        for lv in range(SEG_LEVELS):
            b_re, b_im = coef(d, 1 + lv)
            h_re, h_im = seg_shift(e_re, 1 << lv, forward), seg_shift(e_im, 1 << lv, forward)
            e_re, e_im = e_re + (b_re * h_re - b_im * h_im), e_im + (b_re * h_im + b_im * h_re)
        c_re, c_im = seg_shift(e_re, 1, forward), seg_shift(e_im, 1, forward)
        x_re, x_im = [], []
        for q in range(nq):
            n = order.index(q)
            w_re, w_im = coef(d, 1 + SEG_LEVELS + n)
            r_re, r_im = w_re * c_re - w_im * c_im, w_re * c_im + w_im * c_re
            if n:
                r_re, r_im = r_re + t_re[order[n - 1]], r_im + t_im[order[n - 1]]
            x_re.append(r_re)
            x_im.append(r_im)
        carried += [jnp.concatenate(x_re, axis=0), jnp.concatenate(x_im, axis=0)]

    for g in range(GROUPS_PER_SLAB):
        c0 = (g // 2) * LANES
        xpair = jnp.concatenate([c[:, c0:c0 + LANES] for c in carried], axis=1).astype(BF16)
        yts[g] = yts[g] + lax.dot_general(dect_ref[g], xpair, (((1,), (1,)), ((), ())),
                                          preferred_element_type=F32)
    dsk = dsk_ref[...]
    for t in range(CHUNK):
        yt = jnp.concatenate([y[t * SSM_GROUP:(t + 1) * SSM_GROUP] for y in yts], axis=0)
        y = yt.T + dsk * xs[t]
        for q in range(nq):
            store_y(tile(q, t), tile_rows(y, q))


def _s5_kernel(u_ref, mt_ref, enc_ref, dect_ref, coef_ref, dsk_ref, y_ref, y_lo_ref):
    nq = u_ref.shape[0] // (CHUNK * SUBLANES)

    def store_lo(rows, y):
        y_lo_ref[rows, :] = y

    def store_both(rows, y):
        y_ref[rows, :] = _pack2(y_lo_ref[rows, :], y)

    for i, store in enumerate((store_lo, store_both)):
        _s5_slab(nq, lambda rows, i=i: _unpack2(u_ref[rows, :])[i], mt_ref.at[i], enc_ref.at[i],
                 dect_ref.at[i], coef_ref.at[i], dsk_ref.at[i], store)


def _s5(u4, mt, enc, dect, coef, dsk):
    b, planes, seq, _ = u4.shape
    per_plane = lambda a: pl.BlockSpec((2,) + a.shape[1:], lambda p, bi: (p,) + (0,) * (a.ndim - 1),
                                       pipeline_mode=pl.Buffered(1))
    io = pl.BlockSpec((None, None, seq, LANES), lambda p, bi: (bi, p, 0, 0))
    return pl.pallas_call(
        _s5_kernel,
        grid=(planes, b),
        in_specs=[io, per_plane(mt), per_plane(enc), per_plane(dect), per_plane(coef), per_plane(dsk)],
        out_specs=io,
        out_shape=jax.ShapeDtypeStruct(u4.shape, jnp.uint32),
        scratch_shapes=[pltpu.VMEM((seq, LANES), F32)],
        compiler_params=pltpu.CompilerParams(dimension_semantics=("parallel", "arbitrary"),
                                             vmem_limit_bytes=VMEM_LIMIT),
        name="s5",
    )(u4, mt, enc, dect, coef, dsk)


def _cmul(ar, ai, br, bi):
    return ar * br - ai * bi, ar * bi + ai * br


def _cpowers(a_re, a_im, n):
    p_re, p_im = jnp.ones_like(a_re)[None], jnp.zeros_like(a_im)[None]
    s_re, s_im = a_re, a_im
    while p_re.shape[0] < n:
        h_re, h_im = _cmul(p_re, p_im, s_re[None], s_im[None])
        p_re, p_im = jnp.concatenate([p_re, h_re], 0), jnp.concatenate([p_im, h_im], 0)
        s_re, s_im = _cmul(s_re, s_im, s_re, s_im)
    return p_re[:n], p_im[:n]


def _s5_gen_kernel(bm_ref, cm_ref, apow_ref, enc_ref, dect_ref):
    row_group = lax.broadcasted_iota(jnp.int32, (LANES, SLAB_STATE), 0) // SSM_GROUP
    col_group = lax.broadcasted_iota(jnp.int32, (LANES, SLAB_STATE), 1) // SSM_STATE

    def block_diag(ref, d, ri):
        wide = jnp.concatenate([ref[0, d, ri]] * (SLAB_STATE // LANES), axis=1)
        return jnp.where(row_group == col_group, wide, 0.0)

    bm = [[block_diag(bm_ref, d, ri) for ri in range(2)] for d in range(2)]
    cm = [[block_diag(cm_ref, d, ri) for ri in range(2)] for d in range(2)]

    def put(ref, d, ri, step, tile):
        tile = tile.astype(BF16)
        c0 = (2 * d + ri) * LANES
        for g in range(GROUPS_PER_SLAB):
            p0 = (g // 2) * LANES
            ref[0, g, step * SSM_GROUP:(step + 1) * SSM_GROUP, c0:c0 + LANES] = (
                tile[g * SSM_GROUP:(g + 1) * SSM_GROUP, p0:p0 + LANES])

    for d in range(2):
        for k in range(CHUNK + 1):
            a_re, a_im = apow_ref[0, d, 0, k:k + 1, :], apow_ref[0, d, 1, k:k + 1, :]
            if k < CHUNK:
                s = CHUNK - 1 - k if d == 0 else k
                e_re, e_im = _cmul(bm[d][0], bm[d][1], a_re, a_im)
                put(enc_ref, d, 0, s, e_re)
                put(enc_ref, d, 1, s, e_im)
            if k >= 1:
                t = k - 1 if d == 0 else CHUNK - k
                f_re, f_im = _cmul(cm[d][0], cm[d][1], a_re, a_im)
                put(dect_ref, d, 0, t, f_re)
                put(dect_ref, d, 1, t, -f_im)


def _s5_gen(bm, cm, apow):
    small = lambda a: pl.BlockSpec((1,) + a.shape[1:], lambda sl: (sl,) + (0,) * (a.ndim - 1))
    shape = (SLABS, GROUPS_PER_SLAB, CHUNK * SSM_GROUP, 4 * LANES)
    out = pl.BlockSpec((1,) + shape[1:], lambda sl: (sl, 0, 0, 0))
    return pl.pallas_call(
        _s5_gen_kernel,
        grid=(SLABS,),
        in_specs=[small(bm), small(cm), small(apow)],
        out_specs=[out, out],
        out_shape=[jax.ShapeDtypeStruct(shape, BF16), jax.ShapeDtypeStruct(shape, BF16)],
        compiler_params=pltpu.CompilerParams(dimension_semantics=("parallel",),
                                             vmem_limit_bytes=VMEM_LIMIT),
        name="s5_gen",
    )(bm, cm, apow)


def _slab_tiles(t_re, t_im):
    t = jnp.stack([t_re, t_im], 1).astype(F32)
    t = t.reshape(2, 2, SLABS, LANES, SSM_STATE)
    t = jnp.concatenate([t, t], axis=-1)
    return jnp.transpose(t, (2, 0, 1, 3, 4))


def _s5_weights(lam_re, lam_im, log_dt, b_re, b_im, c_re, c_im, nq):
    G, C = SSM_GROUPS, SSM_GROUP
    dt = jnp.exp(log_dt.astype(F32))[..., None]
    lr, li = lam_re.astype(F32), lam_im.astype(F32)
    mag = jnp.exp(lr * dt)
    ang = li * dt
    a_re, a_im = mag * jnp.cos(ang), mag * jnp.sin(ang)
    den = lr * lr + li * li
    nr, ni = a_re - 1.0, a_im
    z_re = (nr * lr + ni * li) / den
    z_im = (ni * lr - nr * li) / den
    br, bi = b_re.astype(F32), b_im.astype(F32)
    bb_re = z_re[..., None] * br - z_im[..., None] * bi
    bb_im = z_re[..., None] * bi + z_im[..., None] * br
    cr = jnp.swapaxes(c_re.astype(F32), -1, -2)
    ci = jnp.swapaxes(c_im.astype(F32), -1, -2)

    pw_re, pw_im = _cpowers(a_re, a_im, CHUNK + 1)

    ab_re, ab_im = _cmul(pw_re[:CHUNK, ..., None], pw_im[:CHUNK, ..., None], bb_re[None], bb_im[None])
    kern = jnp.einsum('dgpc,kdgpe->kdgce', jnp.concatenate([cr, -ci], 2),
                      jnp.concatenate([ab_re, ab_im], 3), precision='highest')
    lag_tab = jnp.concatenate([kern[::-1, 0][:-1], (kern[0, 0] + kern[0, 1])[None], kern[1:, 1]], 0)
    lag_tab = jnp.transpose(lag_tab, (1, 2, 0, 3)).reshape(G, C, (2 * CHUNK - 1) * C)
    mt = jnp.stack([lag_tab[:, :, (CHUNK - 1 - t) * C:(CHUNK - 1 - t) * C + CHUNK * C]
                    for t in range(CHUNK)], 1).reshape(SLABS, GROUPS_PER_SLAB, CHUNK * C, CHUNK * C)

    apow = jnp.stack([pw_re, pw_im], 0).reshape(2, CHUNK + 1, 2, SLABS, SLAB_STATE)
    apow = jnp.transpose(apow, (3, 2, 0, 1, 4))
    enc, dect = _s5_gen(_slab_tiles(jnp.swapaxes(bb_re, -1, -2), jnp.swapaxes(bb_im, -1, -2)),
                        _slab_tiles(c_re, c_im), apow)

    s_re, s_im = pw_re[CHUNK], pw_im[CHUNK]
    w_re, w_im = _cpowers(s_re, s_im, nq)
    q_re, q_im = _cmul(w_re[nq - 1], w_im[nq - 1], s_re, s_im)
    rows_re, rows_im = [s_re], [s_im]
    for _ in range(SEG_LEVELS):
        rows_re.append(q_re)
        rows_im.append(q_im)
        q_re, q_im = _cmul(q_re, q_im, q_re, q_im)
    c_all = jnp.stack([jnp.concatenate([jnp.stack(rows_re, 0), w_re], 0),
                       jnp.concatenate([jnp.stack(rows_im, 0), w_im], 0)], 0)
    n_rows = c_all.shape[1]
    c_all = c_all.reshape(2, n_rows, 2, SLABS, SLAB_STATE)
    coef = jnp.transpose(c_all, (3, 2, 0, 1, 4))
    return mt.astype(BF16), enc, dect, coef


def _post_kernel(x_ref, ya_ref, ys_ref, g_ref, wglu_ref, bglu_ref, wout_ref, ln2_ref,
                 wup_ref, wdown_ref, o_ref):
    nseg, tq, _ = x_ref.shape
    tm = nseg * tq
    ys = jnp.concatenate(
        [jnp.concatenate([half for p in range(SLABS // 2)
                          for half in _unpack2(ys_ref[p, pl.ds(r, tq, stride=nseg), :])], axis=1)
         for r in range(nseg)], axis=0)
    y = jax.nn.gelu(ys)
    z = jnp.dot(y.astype(BF16), wglu_ref[...], preferred_element_type=F32) + bglu_ref[...]
    y = y * jax.nn.sigmoid(z)
    g = g_ref[...].reshape(tm, 2 * D_MODEL).astype(F32)
    ya = ya_ref[...].reshape(tm, D_MODEL).astype(F32)
    mixed = jax.nn.sigmoid(g[:, :D_MODEL]) * ya + jax.nn.sigmoid(g[:, D_MODEL:]) * y
    x = x_ref[...].reshape(tm, D_MODEL)
    x = x + jnp.dot(mixed.astype(BF16), wout_ref[...], preferred_element_type=F32)
    ms = jnp.mean(x * x, axis=-1, keepdims=True)
    h = (x * ln2_ref[...]).astype(BF16)
    up = jnp.dot(h, wup_ref[...], preferred_element_type=F32) * lax.rsqrt(ms + EPS)
    act = jnp.square(jnp.maximum(up, 0.0)).astype(BF16)
    out = x + jnp.dot(act, wdown_ref[...], preferred_element_type=F32)
    o_ref[...] = out.reshape(nseg, tq, D_MODEL)


def _post(x2, ya, ys4, g, w_glu, b_glu, w_out, ln2, w_up, w_down, tq):
    b, _, seq, _ = ys4.shape
    per_seq = seq // SUBLANES // tq
    seg = lambda c: pl.BlockSpec((None, SUBLANES, tq, c), lambda i: (i // per_seq, 0, i % per_seq, 0))
    interleaved = pl.BlockSpec((None, SLABS // 2, tq * SUBLANES, LANES),
                               lambda i: (i // per_seq, 0, i % per_seq, 0))
    out = pl.pallas_call(
        _post_kernel,
        grid=(b * per_seq,),
        in_specs=[seg(D_MODEL), seg(D_MODEL), interleaved, seg(2 * D_MODEL),
                  _const_spec((D_MODEL, D_MODEL)), _const_spec((1, D_MODEL)),
                  _const_spec((D_MODEL, D_MODEL)), _const_spec((1, D_MODEL)),
                  _const_spec((D_MODEL, D_FF)), _const_spec((D_FF, D_MODEL))],
        out_specs=seg(D_MODEL),
        out_shape=jax.ShapeDtypeStruct((b, SUBLANES, seq // SUBLANES, D_MODEL), F32),
        compiler_params=pltpu.CompilerParams(dimension_semantics=("parallel",),
                                             vmem_limit_bytes=VMEM_LIMIT),
        name="post",
    )(_seg_view(x2, b, seq), _seg_view(ya, b, seq), ys4, _seg_view(g, b, seq),
      w_glu, b_glu, w_out, ln2, w_up, w_down)
    return out.reshape(b, seq, D_MODEL)


def _t5_bucket(rel):
    nb = N_BUCKETS // 2
    ret = (rel > 0).astype(np.int32) * nb
    n = np.abs(rel)
    max_exact = nb // 2
    n_safe = np.maximum(n, 1).astype(np.float32)
    large = max_exact + (np.log(n_safe / max_exact) / math.log(MAX_DISTANCE / max_exact)
                         * (nb - max_exact)).astype(np.int32)
    large = np.minimum(large, nb - 1)
    return (ret + np.where(n < max_exact, n, large)).astype(np.int32)


def _band_bias(rel_table):
    qi = np.arange(BLOCK)[:, None]
    kj = np.arange(3 * BLOCK)[None, :]
    rel = kj - BLOCK - qi
    onehot = (np.arange(N_BUCKETS)[:, None] == _t5_bucket(rel).reshape(1, -1)).astype(np.float32)
    bias = jnp.dot(rel_table.astype(F32).T * LOG2E, jnp.asarray(onehot), precision='highest')
    bias = bias.reshape(N_HEADS, BLOCK, 3 * BLOCK)
    in_band = np.abs(rel) <= WINDOW
    keep = np.stack([in_band & (kj >= BLOCK), in_band, in_band & (kj < 2 * BLOCK)], 0)
    bias = jnp.where(jnp.asarray(keep)[:, None], bias[None], NEG_INF)
    bias = bias.reshape(3, N_KV_HEADS // 2, 2, PAIRS, BLOCK, 3 * BLOCK)
    return jnp.transpose(bias, (0, 1, 3, 4, 2, 5)).reshape(3, N_KV_HEADS // 2, PAIRS * BLOCK, 6 * BLOCK)


def _sum_mat():
    first = np.arange(6 * BLOCK)[:, None] < 3 * BLOCK
    low = np.arange(LANES)[None, :] < HEAD_DIM
    return jnp.asarray(first == low, BF16)


def _layer(x, bias, sink, ln1, w_in, qg, kg, s5w, dsk, w_glu, b_glu, w_out, ln2, w_up, w_down):
    b, s, _ = x.shape
    x2 = x.reshape(b * s, D_MODEL)
    rows_per_seg = s // SUBLANES
    q, kvx, u4, g = _in_proj(x2, ln1, w_in, qg, kg, b, s, min(TQ_IN, rows_per_seg))
    ya = _attn(q.reshape(b, s, ATTN_Q), kvx.reshape(b, s, 2 * ATTN_KV), bias, _sum_mat(), sink)
    ys4 = _s5(u4, *s5w, dsk)
    return _post(x2, ya.reshape(b * s, ATTN_Q), ys4, g, w_glu, b_glu, w_out, ln2, w_up, w_down,
                 min(TQ_POST, rows_per_seg))


def kernel(x_prompt, x_sample, rel_table, ln1, w_in, q_gain, k_gain, sink, lam_re, lam_im, log_dt,
           b_re, b_im, c_re, c_im, d_skip, w_glu, b_glu, w_out, ln2, w_up, w_down):
    assert ln1.shape[0] == 1, "single layer"
    bias = _band_bias(rel_table)
    q_cols = np.concatenate([np.arange(h * HEAD_DIM, (h + 1) * HEAD_DIM) for h in _PAIRED_HEADS])
    cols = np.concatenate([q_cols, np.arange(ATTN_Q, IN_COLS)])
    w_in_paired = w_in[0][:, cols].astype(BF16)
    outs = []
    s5w_by_nq = {}
    for x in (x_prompt, x_sample):
        seq = x.shape[1]
        nq = seq // (CHUNK * SUBLANES)
        assert nq * CHUNK * SUBLANES == seq and seq % BLOCK == 0
        if nq not in s5w_by_nq:
            s5w_by_nq[nq] = _s5_weights(lam_re[0], lam_im[0], log_dt[0], b_re[0], b_im[0],
                                        c_re[0], c_im[0], nq)
        s5w = s5w_by_nq[nq]
        outs.append(_layer(
            x, bias, sink[0].astype(F32) * LOG2E,
            ln1[0].astype(F32)[None], w_in_paired,
            jnp.tile(q_gain[0].astype(F32), N_HEADS)[None],
            jnp.tile(k_gain[0].astype(F32), N_KV_HEADS)[None],
            s5w, d_skip[0].astype(F32).reshape(SLABS, 1, LANES),
            w_glu[0].astype(BF16), b_glu[0].astype(F32)[None], w_out[0].astype(BF16),
            ln2[0].astype(F32)[None], w_up[0].astype(BF16), w_down[0].astype(BF16)))
    return tuple(outs)
```

```python
import math

import numpy as np
import jax
import jax.numpy as jnp
from jax import lax
from jax.experimental import pallas as pl
from jax.experimental.pallas import tpu as pltpu

D_MODEL = 1024
N_HEADS = 16
N_KV_HEADS = 4
HEAD_DIM = 64
Q_PER_KV = N_HEADS // N_KV_HEADS
WINDOW = 128
BLOCK = 128
N_BUCKETS = 32
MAX_DISTANCE = 128
ATTN_Q = N_HEADS * HEAD_DIM
ATTN_KV = N_KV_HEADS * HEAD_DIM
NEG_INF = -1e30
SSM_WIDTH = D_MODEL
SSM_GROUP = 16
SSM_GROUPS = SSM_WIDTH // SSM_GROUP
SSM_STATE = 64
D_FF = 4 * D_MODEL
EPS = 1e-6
IN_COLS = ATTN_Q + 2 * ATTN_KV + SSM_WIDTH + 2 * D_MODEL

LANES = 128
SUBLANES = 8
SEG_LEVELS = 3
CHUNK = 16
SLABS = SSM_WIDTH // LANES
GROUPS_PER_SLAB = LANES // SSM_GROUP
SLAB_STATE = GROUPS_PER_SLAB * SSM_STATE
VMEM_LIMIT = 56 * 1024 * 1024
TQ_IN = 64
TQ_POST = 64
QBLOCKS = 16

LOG2E = math.log2(math.e)
Q_SCALE = HEAD_DIM ** -0.5 * LOG2E

BF16 = jnp.bfloat16
F32 = jnp.float32


def _const_spec(shape):
    nd = len(shape)
    return pl.BlockSpec(shape, lambda *_: (0,) * nd, pipeline_mode=pl.Buffered(1))


def _pack2(lo, hi):
    lo_bits = lax.bitcast_convert_type(lo.astype(BF16).astype(F32), jnp.uint32)
    hi_bits = lax.bitcast_convert_type(hi.astype(BF16).astype(F32), jnp.uint32)
    return (lo_bits >> 16) | hi_bits


def _unpack2(words):
    lo = lax.bitcast_convert_type(words << 16, F32)
    hi = lax.bitcast_convert_type(words & jnp.uint32(0xFFFF0000), F32)
    return lo, hi


def _in_proj_kernel(x_ref, ln1_ref, w_ref, qg_ref, kg_ref, q_ref, kv_ref, u_ref, g_ref):
    nseg, tq, _ = x_ref.shape
    tm = nseg * tq
    x = x_ref[...].reshape(tm, D_MODEL)
    ms = jnp.mean(x * x, axis=-1, keepdims=True)
    rinv = lax.rsqrt(ms + EPS)
    h = (x * ln1_ref[...]).astype(BF16)
    proj = lambda c0, width: jnp.dot(h, w_ref[:, c0:c0 + width], preferred_element_type=F32) * rinv

    low = lax.broadcasted_iota(jnp.int32, (1, LANES), 1) < HEAD_DIM

    def head_norm(t, gain):
        cols = []
        for j in range(t.shape[1] // LANES):
            blk = t[:, j * LANES:(j + 1) * LANES]
            sq = blk * blk
            m_lo = jnp.sum(jnp.where(low, sq, 0.0), axis=-1, keepdims=True) * (1.0 / HEAD_DIM)
            m_hi = jnp.sum(jnp.where(low, 0.0, sq), axis=-1, keepdims=True) * (1.0 / HEAD_DIM)
            cols.append(blk * jnp.where(low, lax.rsqrt(m_lo + EPS), lax.rsqrt(m_hi + EPS)))
        return jnp.concatenate(cols, axis=1) * gain

    c0 = 0
    q = (head_norm(proj(c0, ATTN_Q), qg_ref[...]) * Q_SCALE).astype(BF16)
    q_ref[...] = q.reshape(nseg, tq, ATTN_Q)
    c0 += ATTN_Q
    k = head_norm(proj(c0, ATTN_KV), kg_ref[...])
    c0 += ATTN_KV
    v = proj(c0, ATTN_KV)
    c0 += ATTN_KV
    kvx = jnp.concatenate([k, pltpu.roll(k, HEAD_DIM, axis=1), v, pltpu.roll(v, HEAD_DIM, axis=1)],
                          axis=1).astype(BF16)
    kv_ref[...] = kvx.reshape(nseg, tq, 4 * ATTN_KV)
    u = proj(c0, SSM_WIDTH)
    for p in range(SLABS // 2):
        words = _pack2(u[:, (2 * p) * LANES:(2 * p + 1) * LANES], u[:, (2 * p + 1) * LANES:(2 * p + 2) * LANES])
        for r in range(nseg):
            u_ref[p, pl.ds(r, tq, stride=nseg), :] = words[r * tq:(r + 1) * tq]
    c0 += SSM_WIDTH
    g = proj(c0, 2 * D_MODEL).astype(BF16)
    g_ref[...] = g.reshape(nseg, tq, 2 * D_MODEL)


def _seg_view(a, b, seq):
    return a.reshape(b, SUBLANES, seq // SUBLANES, a.shape[-1])


def _in_proj(x2, ln1, w_in, qg, kg, b, seq, tq):
    n = x2.shape[0]
    per_seq = seq // SUBLANES // tq
    seg = lambda c: pl.BlockSpec((None, SUBLANES, tq, c), lambda i: (i // per_seq, 0, i % per_seq, 0))
    interleaved = pl.BlockSpec((None, SLABS // 2, tq * SUBLANES, LANES),
                               lambda i: (i // per_seq, 0, i % per_seq, 0))
    seg_shape = lambda c, dt: jax.ShapeDtypeStruct((b, SUBLANES, seq // SUBLANES, c), dt)
    q, kvx, u, g = pl.pallas_call(
        _in_proj_kernel,
        grid=(b * per_seq,),
        in_specs=[seg(D_MODEL), _const_spec((1, D_MODEL)), _const_spec((D_MODEL, IN_COLS)),
                  _const_spec((1, ATTN_Q)), _const_spec((1, ATTN_KV))],
        out_specs=[seg(ATTN_Q), seg(4 * ATTN_KV), interleaved, seg(2 * D_MODEL)],
        out_shape=[seg_shape(ATTN_Q, BF16), seg_shape(4 * ATTN_KV, BF16),
                   jax.ShapeDtypeStruct((b, SLABS // 2, seq, LANES), jnp.uint32),
                   seg_shape(2 * D_MODEL, BF16)],
        compiler_params=pltpu.CompilerParams(dimension_semantics=("parallel",),
                                             vmem_limit_bytes=VMEM_LIMIT),
        name="in_proj",
    )(_seg_view(x2, b, seq), ln1, w_in, qg, kg)
    return q.reshape(n, ATTN_Q), kvx.reshape(n, 4 * ATTN_KV), u, g.reshape(n, 2 * D_MODEL)


_ROLLED_TILE = (0, 1, 1, 0)


def _attn_kernel(sink_ref, q_ref, kvp_ref, kvc_ref, kvn_ref, bias_ref, ones_ref, o_ref):
    n = pl.program_id(1)
    last = pl.num_programs(1) - 1
    kvx = jnp.concatenate([kvp_ref[0], kvc_ref[0], kvn_ref[0]], axis=0)
    low = lax.broadcasted_iota(jnp.int32, (1, LANES), 1) < HEAD_DIM
    zero = jnp.zeros((), BF16)

    def halves(base, kv):
        plain = kvx[:, base + (kv // 2) * LANES:base + (kv // 2 + 1) * LANES]
        r0 = base + ATTN_KV + _ROLLED_TILE[kv] * LANES
        rolled = kvx[:, r0:r0 + LANES]
        lo_src, hi_src = (plain, rolled) if kv % 2 == 0 else (rolled, plain)
        return jnp.where(low, lo_src, zero), jnp.where(low, zero, hi_src)

    top = lax.broadcasted_iota(jnp.int32, (2 * BLOCK, 1), 0) < BLOCK
    outs = [[] for _ in range(QBLOCKS)]
    for kv in range(N_KV_HEADS):
        ke, ko = halves(0, kv)
        ve, vo = halves(2 * ATTN_KV, kv)
        sink_e = jnp.where(top, sink_ref[4 * kv], sink_ref[4 * kv + 2])
        sink_o = jnp.where(top, sink_ref[4 * kv + 1], sink_ref[4 * kv + 3])
        for qb in range(QBLOCKS):
            rows = slice(qb * BLOCK, (qb + 3) * BLOCK)
            keys = jnp.concatenate([ke[rows], ko[rows]], axis=0)
            vals = jnp.concatenate([jnp.concatenate([ve[rows], vo[rows]], axis=0), ones_ref[...]],
                                   axis=1)
            if qb == 0:
                variant = jnp.where(n == 0, 0, 1)
            elif qb == QBLOCKS - 1:
                variant = jnp.where(n == last, 2, 1)
            else:
                variant = 1
            qrows = slice(qb * BLOCK, (qb + 1) * BLOCK)
            q2 = jnp.concatenate([q_ref[0, qrows, (2 * kv) * LANES:(2 * kv + 1) * LANES],
                                  q_ref[0, qrows, (2 * kv + 1) * LANES:(2 * kv + 2) * LANES]], axis=0)
            s = lax.dot_general(q2, keys, (((1,), (1,)), ((), ())), preferred_element_type=F32)
            s = s + bias_ref[variant, kv]
            se, so = s[:, :3 * BLOCK], s[:, 3 * BLOCK:]
            me = jnp.maximum(jnp.max(se, axis=-1, keepdims=True), sink_e)
            mo = jnp.maximum(jnp.max(so, axis=-1, keepdims=True), sink_o)
            p = jnp.concatenate([jnp.exp2(se - me), jnp.exp2(so - mo)], axis=1).astype(BF16)
            od = jnp.dot(p, vals, preferred_element_type=F32)
            den = od[:, LANES:] + jnp.where(low, jnp.exp2(sink_e - me), jnp.exp2(sink_o - mo))
            o = od[:, :LANES] / den
            outs[qb] += [o[:BLOCK], o[BLOCK:]]
    for qb in range(QBLOCKS):
        o_ref[0, qb * BLOCK:(qb + 1) * BLOCK, :] = jnp.concatenate(outs[qb], axis=-1).astype(BF16)


def _attn(q, kvx, bias, ones, sink):
    b, s, _ = q.shape
    nb = s // BLOCK
    assert nb % QBLOCKS == 0 and QBLOCKS >= 2
    steps = nb // QBLOCKS
    edge_spec = lambda f: pl.BlockSpec((1, BLOCK, 4 * ATTN_KV), f)
    wide = lambda c: pl.BlockSpec((1, QBLOCKS * BLOCK, c), lambda bi, n, *_: (bi, n, 0))
    prev = lambda bi, n, *_: (bi, jnp.maximum(n * QBLOCKS - 1, 0), 0)
    nxt = lambda bi, n, *_: (bi, jnp.minimum((n + 1) * QBLOCKS, nb - 1), 0)
    grid_spec = pltpu.PrefetchScalarGridSpec(
        num_scalar_prefetch=1,
        grid=(b, steps),
        in_specs=[wide(ATTN_Q), edge_spec(prev), wide(4 * ATTN_KV), edge_spec(nxt),
                  _const_spec(bias.shape), _const_spec(ones.shape)],
        out_specs=wide(ATTN_Q),
    )
    return pl.pallas_call(
        _attn_kernel,
        grid_spec=grid_spec,
        out_shape=jax.ShapeDtypeStruct((b, s, ATTN_Q), BF16),
        compiler_params=pltpu.CompilerParams(dimension_semantics=("parallel", "arbitrary"),
                                             vmem_limit_bytes=VMEM_LIMIT),
        name="attn",
    )(sink, q, kvx, kvx, kvx, bias, ones)


def _s5_slab(nq, load_u, mt_ref, enc_ref, dect_ref, coef_ref, dsk_ref, store_y):
    tile_rows = lambda x, q: x[q * SUBLANES:(q + 1) * SUBLANES]
    tile = lambda q, s: pl.ds((q * CHUNK + s) * SUBLANES, SUBLANES)

    x_step = lambda s: jnp.concatenate([load_u(tile(q, s)) for q in range(nq)], axis=0)
    xst = [x_step(s).astype(BF16).T for s in range(CHUNK)]

    yield
    yts, pair_states = [], []
    for g in range(GROUPS_PER_SLAB):
        ugt = jnp.concatenate([xt[g * SSM_GROUP:(g + 1) * SSM_GROUP] for xt in xst], axis=0)
        ug = ugt.T
        yts.append(jnp.dot(mt_ref[g], ugt, preferred_element_type=F32))
        part = jnp.dot(ug, enc_ref[g], preferred_element_type=F32)
        if g % 2 == 0:
            pair_states.append(part)
        else:
            pair_states[-1] = pair_states[-1] + part

    def state(d, ri):
        c0 = (2 * d + ri) * LANES
        return jnp.concatenate([p[:, c0:c0 + LANES] for p in pair_states], axis=1)

    def coef(d, row):
        return coef_ref[d, 0, row:row + 1, :], coef_ref[d, 1, row:row + 1, :]

    seg = lax.broadcasted_iota(jnp.int32, (SUBLANES, 1), 0)

    def seg_shift(x, sh, forward):
        if forward:
            return jnp.where(seg >= sh, pltpu.roll(x, sh, axis=0), 0.0)
        return jnp.where(seg < SUBLANES - sh, pltpu.roll(x, SUBLANES - sh, axis=0), 0.0)

    yield
    carried = []
    for d in range(2):
        forward = d == 0
        s_re, s_im = state(d, 0), state(d, 1)
        a_re, a_im = coef(d, 0)
        order = list(range(nq)) if forward else list(range(nq - 1, -1, -1))
        t_re, t_im = {}, {}
        for n, q in enumerate(order):
            cur_re, cur_im = tile_rows(s_re, q), tile_rows(s_im, q)
            if n:
                p_re, p_im = t_re[order[n - 1]], t_im[order[n - 1]]
                cur_re = cur_re + (a_re * p_re - a_im * p_im)
                cur_im = cur_im + (a_re * p_im + a_im * p_re)
            t_re[q], t_im[q] = cur_re, cur_im
        e_re, e_im = t_re[order[-1]], t_im[order[-1]]
        for lv in range(SEG_LEVELS):
            b_re, b_im = coef(d, 1 + lv)
            h_re, h_im = seg_shift(e_re, 1 << lv, forward), seg_shift(e_im, 1 << lv, forward)
            e_re, e_im = e_re + (b_re * h_re - b_im * h_im), e_im + (b_re * h_im + b_im * h_re)
        c_re, c_im = seg_shift(e_re, 1, forward), seg_shift(e_im, 1, forward)
        x_re, x_im = [], []
        for q in range(nq):
            n = order.index(q)
            w_re, w_im = coef(d, 1 + SEG_LEVELS + n)
            r_re, r_im = w_re * c_re - w_im * c_im, w_re * c_im + w_im * c_re
            if n:
                r_re, r_im = r_re + t_re[order[n - 1]], r_im + t_im[order[n - 1]]
            x_re.append(r_re)
            x_im.append(r_im)
        carried += [jnp.concatenate(x_re, axis=0), jnp.concatenate(x_im, axis=0)]

    yield
    for g in range(GROUPS_PER_SLAB):
        c0 = (g // 2) * LANES
        xpair = jnp.concatenate([c[:, c0:c0 + LANES] for c in carried], axis=1).astype(BF16)
        yts[g] = yts[g] + lax.dot_general(dect_ref[g], xpair, (((1,), (1,)), ((), ())),
                                          preferred_element_type=F32)
    yield
    dsk = dsk_ref[...]
    for t in range(CHUNK):
        yt = jnp.concatenate([y[t * SSM_GROUP:(t + 1) * SSM_GROUP] for y in yts], axis=0)
        y = yt.T + dsk * x_step(t)
        for q in range(nq):
            store_y(tile(q, t), tile_rows(y, q))


def _s5_kernel(u_ref, mt_ref, enc_ref, dect_ref, coef_ref, dsk_ref, y_ref, y_lo_ref):
    nq = u_ref.shape[0] // (CHUNK * SUBLANES)

    def store_lo(rows, y):
        y_lo_ref[rows, :] = y

    def store_both(rows, y):
        y_ref[rows, :] = _pack2(y_lo_ref[rows, :], y)

    slabs = [_s5_slab(nq, lambda rows, i=i: _unpack2(u_ref[rows, :])[i], mt_ref.at[i], enc_ref.at[i],
                      dect_ref.at[i], coef_ref.at[i], dsk_ref.at[i], store)
             for i, store in enumerate((store_lo, store_both))]
    next(slabs[0], None)
    busy = True
    while busy:
        busy = False
        for s in slabs:
            try:
                next(s)
                busy = True
            except StopIteration:
                pass


def _s5(u4, mt, enc, dect, coef, dsk):
    b, planes, seq, _ = u4.shape
    per_plane = lambda a: pl.BlockSpec((2,) + a.shape[1:], lambda p, bi: (p,) + (0,) * (a.ndim - 1),
                                       pipeline_mode=pl.Buffered(1))
    io = pl.BlockSpec((None, None, seq, LANES), lambda p, bi: (bi, p, 0, 0))
    return pl.pallas_call(
        _s5_kernel,
        grid=(planes, b),
        in_specs=[io, per_plane(mt), per_plane(enc), per_plane(dect), per_plane(coef), per_plane(dsk)],
        out_specs=io,
        out_shape=jax.ShapeDtypeStruct(u4.shape, jnp.uint32),
        scratch_shapes=[pltpu.VMEM((seq, LANES), F32)],
        compiler_params=pltpu.CompilerParams(dimension_semantics=("parallel", "arbitrary"),
                                             vmem_limit_bytes=VMEM_LIMIT),
        name="s5",
    )(u4, mt, enc, dect, coef, dsk)


def _cmul(ar, ai, br, bi):
    return ar * br - ai * bi, ar * bi + ai * br


def _cpowers(a_re, a_im, n):
    p_re, p_im = jnp.ones_like(a_re)[None], jnp.zeros_like(a_im)[None]
    s_re, s_im = a_re, a_im
    while p_re.shape[0] < n:
        h_re, h_im = _cmul(p_re, p_im, s_re[None], s_im[None])
        p_re, p_im = jnp.concatenate([p_re, h_re], 0), jnp.concatenate([p_im, h_im], 0)
        s_re, s_im = _cmul(s_re, s_im, s_re, s_im)
    return p_re[:n], p_im[:n]


def _s5_gen_kernel(bm_ref, cm_ref, apow_ref, enc_ref, dect_ref):
    row_group = lax.broadcasted_iota(jnp.int32, (LANES, SLAB_STATE), 0) // SSM_GROUP
    col_group = lax.broadcasted_iota(jnp.int32, (LANES, SLAB_STATE), 1) // SSM_STATE

    def block_diag(ref, d, ri):
        wide = jnp.concatenate([ref[0, d, ri]] * (SLAB_STATE // LANES), axis=1)
        return jnp.where(row_group == col_group, wide, 0.0)

    bm = [[block_diag(bm_ref, d, ri) for ri in range(2)] for d in range(2)]
    cm = [[block_diag(cm_ref, d, ri) for ri in range(2)] for d in range(2)]

    def put(ref, d, ri, step, tile):
        tile = tile.astype(BF16)
        c0 = (2 * d + ri) * LANES
        for g in range(GROUPS_PER_SLAB):
            p0 = (g // 2) * LANES
            ref[0, g, step * SSM_GROUP:(step + 1) * SSM_GROUP, c0:c0 + LANES] = (
                tile[g * SSM_GROUP:(g + 1) * SSM_GROUP, p0:p0 + LANES])

    for d in range(2):
        for k in range(CHUNK + 1):
            a_re, a_im = apow_ref[0, d, 0, k:k + 1, :], apow_ref[0, d, 1, k:k + 1, :]
            if k < CHUNK:
                s = CHUNK - 1 - k if d == 0 else k
                e_re, e_im = _cmul(bm[d][0], bm[d][1], a_re, a_im)
                put(enc_ref, d, 0, s, e_re)
                put(enc_ref, d, 1, s, e_im)
            if k >= 1:
                t = k - 1 if d == 0 else CHUNK - k
                f_re, f_im = _cmul(cm[d][0], cm[d][1], a_re, a_im)
                put(dect_ref, d, 0, t, f_re)
                put(dect_ref, d, 1, t, -f_im)


def _s5_gen(bm, cm, apow):
    small = lambda a: pl.BlockSpec((1,) + a.shape[1:], lambda sl: (sl,) + (0,) * (a.ndim - 1))
    shape = (SLABS, GROUPS_PER_SLAB, CHUNK * SSM_GROUP, 4 * LANES)
    out = pl.BlockSpec((1,) + shape[1:], lambda sl: (sl, 0, 0, 0))
    return pl.pallas_call(
        _s5_gen_kernel,
        grid=(SLABS,),
        in_specs=[small(bm), small(cm), small(apow)],
        out_specs=[out, out],
        out_shape=[jax.ShapeDtypeStruct(shape, BF16), jax.ShapeDtypeStruct(shape, BF16)],
        compiler_params=pltpu.CompilerParams(dimension_semantics=("parallel",),
                                             vmem_limit_bytes=VMEM_LIMIT),
        name="s5_gen",
    )(bm, cm, apow)


def _slab_tiles(t_re, t_im):
    t = jnp.stack([t_re, t_im], 1).astype(F32)
    t = t.reshape(2, 2, SLABS, LANES, SSM_STATE)
    t = jnp.concatenate([t, t], axis=-1)
    return jnp.transpose(t, (2, 0, 1, 3, 4))


def _s5_weights(lam_re, lam_im, log_dt, b_re, b_im, c_re, c_im, nq):
    G, C = SSM_GROUPS, SSM_GROUP
    dt = jnp.exp(log_dt.astype(F32))[..., None]
    lr, li = lam_re.astype(F32), lam_im.astype(F32)
    mag = jnp.exp(lr * dt)
    ang = li * dt
    a_re, a_im = mag * jnp.cos(ang), mag * jnp.sin(ang)
    den = lr * lr + li * li
    nr, ni = a_re - 1.0, a_im
    z_re = (nr * lr + ni * li) / den
    z_im = (ni * lr - nr * li) / den
    br, bi = b_re.astype(F32), b_im.astype(F32)
    bb_re = z_re[..., None] * br - z_im[..., None] * bi
    bb_im = z_re[..., None] * bi + z_im[..., None] * br
    cr = jnp.swapaxes(c_re.astype(F32), -1, -2)
    ci = jnp.swapaxes(c_im.astype(F32), -1, -2)

    pw_re, pw_im = _cpowers(a_re, a_im, CHUNK + 1)

    ab_re, ab_im = _cmul(pw_re[:CHUNK, ..., None], pw_im[:CHUNK, ..., None], bb_re[None], bb_im[None])
    kern = jnp.einsum('dgpc,kdgpe->kdgce', jnp.concatenate([cr, -ci], 2),
                      jnp.concatenate([ab_re, ab_im], 3), precision='highest')
    lag_tab = jnp.concatenate([kern[::-1, 0][:-1], (kern[0, 0] + kern[0, 1])[None], kern[1:, 1]], 0)
    lag_tab = jnp.transpose(lag_tab, (1, 2, 0, 3)).reshape(G, C, (2 * CHUNK - 1) * C)
    mt = jnp.stack([lag_tab[:, :, (CHUNK - 1 - t) * C:(CHUNK - 1 - t) * C + CHUNK * C]
                    for t in range(CHUNK)], 1).reshape(SLABS, GROUPS_PER_SLAB, CHUNK * C, CHUNK * C)

    apow = jnp.stack([pw_re, pw_im], 0).reshape(2, CHUNK + 1, 2, SLABS, SLAB_STATE)
    apow = jnp.transpose(apow, (3, 2, 0, 1, 4))
    enc, dect = _s5_gen(_slab_tiles(jnp.swapaxes(bb_re, -1, -2), jnp.swapaxes(bb_im, -1, -2)),
                        _slab_tiles(c_re, c_im), apow)

    s_re, s_im = pw_re[CHUNK], pw_im[CHUNK]
    w_re, w_im = _cpowers(s_re, s_im, nq)
    q_re, q_im = _cmul(w_re[nq - 1], w_im[nq - 1], s_re, s_im)
    rows_re, rows_im = [s_re], [s_im]
    for _ in range(SEG_LEVELS):
        rows_re.append(q_re)
        rows_im.append(q_im)
        q_re, q_im = _cmul(q_re, q_im, q_re, q_im)
    c_all = jnp.stack([jnp.concatenate([jnp.stack(rows_re, 0), w_re], 0),
                       jnp.concatenate([jnp.stack(rows_im, 0), w_im], 0)], 0)
    n_rows = c_all.shape[1]
    c_all = c_all.reshape(2, n_rows, 2, SLABS, SLAB_STATE)
    coef = jnp.transpose(c_all, (3, 2, 0, 1, 4))
    return mt.astype(BF16), enc, dect, coef


def _post_kernel(x_ref, ya_ref, ys_ref, g_ref, wglu_ref, bglu_ref, wout_ref, ln2_ref,
                 wup_ref, wdown_ref, o_ref):
    nseg, tq, _ = x_ref.shape
    tm = nseg * tq
    ys = jnp.concatenate(
        [jnp.concatenate([half for p in range(SLABS // 2)
                          for half in _unpack2(ys_ref[p, pl.ds(r, tq, stride=nseg), :])], axis=1)
         for r in range(nseg)], axis=0)
    y = jax.nn.gelu(ys)
    z = jnp.dot(y.astype(BF16), wglu_ref[...], preferred_element_type=F32) + bglu_ref[...]
    y = y * jax.nn.sigmoid(z)
    g = g_ref[...].reshape(tm, 2 * D_MODEL).astype(F32)
    ya = ya_ref[...].reshape(tm, D_MODEL).astype(F32)
    mixed = jax.nn.sigmoid(g[:, :D_MODEL]) * ya + jax.nn.sigmoid(g[:, D_MODEL:]) * y
    x = x_ref[...].reshape(tm, D_MODEL)
    x = x + jnp.dot(mixed.astype(BF16), wout_ref[...], preferred_element_type=F32)
    ms = jnp.mean(x * x, axis=-1, keepdims=True)
    h = (x * ln2_ref[...]).astype(BF16)
    up = jnp.dot(h, wup_ref[...], preferred_element_type=F32) * lax.rsqrt(ms + EPS)
    act = jnp.square(jnp.maximum(up, 0.0)).astype(BF16)
    out = x + jnp.dot(act, wdown_ref[...], preferred_element_type=F32)
    o_ref[...] = out.reshape(nseg, tq, D_MODEL)


def _post(x2, ya, ys4, g, w_glu, b_glu, w_out, ln2, w_up, w_down, tq):
    b, _, seq, _ = ys4.shape
    per_seq = seq // SUBLANES // tq
    seg = lambda c: pl.BlockSpec((None, SUBLANES, tq, c), lambda i: (i // per_seq, 0, i % per_seq, 0))
    interleaved = pl.BlockSpec((None, SLABS // 2, tq * SUBLANES, LANES),
                               lambda i: (i // per_seq, 0, i % per_seq, 0))
    out = pl.pallas_call(
        _post_kernel,
        grid=(b * per_seq,),
        in_specs=[seg(D_MODEL), seg(D_MODEL), interleaved, seg(2 * D_MODEL),
                  _const_spec((D_MODEL, D_MODEL)), _const_spec((1, D_MODEL)),
                  _const_spec((D_MODEL, D_MODEL)), _const_spec((1, D_MODEL)),
                  _const_spec((D_MODEL, D_FF)), _const_spec((D_FF, D_MODEL))],
        out_specs=seg(D_MODEL),
        out_shape=jax.ShapeDtypeStruct((b, SUBLANES, seq // SUBLANES, D_MODEL), F32),
        compiler_params=pltpu.CompilerParams(dimension_semantics=("parallel",),
                                             vmem_limit_bytes=VMEM_LIMIT),
        name="post",
    )(_seg_view(x2, b, seq), _seg_view(ya, b, seq), ys4, _seg_view(g, b, seq),
      w_glu, b_glu, w_out, ln2, w_up, w_down)
    return out.reshape(b, seq, D_MODEL)


def _t5_bucket(rel):
    nb = N_BUCKETS // 2
    ret = (rel > 0).astype(np.int32) * nb
    n = np.abs(rel)
    max_exact = nb // 2
    n_safe = np.maximum(n, 1).astype(np.float32)
    large = max_exact + (np.log(n_safe / max_exact) / math.log(MAX_DISTANCE / max_exact)
                         * (nb - max_exact)).astype(np.int32)
    large = np.minimum(large, nb - 1)
    return (ret + np.where(n < max_exact, n, large)).astype(np.int32)


def _band_bias(rel_table):
    qi = np.arange(BLOCK)[:, None]
    kj = np.arange(3 * BLOCK)[None, :]
    rel = kj - BLOCK - qi
    onehot = (np.arange(N_BUCKETS)[:, None] == _t5_bucket(rel).reshape(1, -1)).astype(np.float32)
    bias = jnp.dot(rel_table.astype(F32).T * LOG2E, jnp.asarray(onehot), precision='highest')
    bias = bias.reshape(N_HEADS, BLOCK, 3 * BLOCK)
    in_band = np.abs(rel) <= WINDOW
    keep = np.stack([in_band & (kj >= BLOCK), in_band, in_band & (kj < 2 * BLOCK)], 0)
    bias = jnp.where(jnp.asarray(keep)[:, None], bias[None], NEG_INF)
    bias = bias.reshape(3, N_KV_HEADS, 2, 2, BLOCK, 3 * BLOCK)
    return jnp.transpose(bias, (0, 1, 2, 4, 3, 5)).reshape(3, N_KV_HEADS, 2 * BLOCK, 6 * BLOCK)


def _sum_mat():
    first = np.arange(6 * BLOCK)[:, None] < 3 * BLOCK
    low = np.arange(LANES)[None, :] < HEAD_DIM
    return jnp.asarray(first == low, BF16)


def _layer(x, bias, sink, ln1, w_in, qg, kg, s5w, dsk, w_glu, b_glu, w_out, ln2, w_up, w_down):
    b, s, _ = x.shape
    x2 = x.reshape(b * s, D_MODEL)
    rows_per_seg = s // SUBLANES
    q, kvx, u4, g = _in_proj(x2, ln1, w_in, qg, kg, b, s, min(TQ_IN, rows_per_seg))
    ya = _attn(q.reshape(b, s, ATTN_Q), kvx.reshape(b, s, 4 * ATTN_KV), bias, _sum_mat(), sink)
    ys4 = _s5(u4, *s5w, dsk)
    return _post(x2, ya.reshape(b * s, ATTN_Q), ys4, g, w_glu, b_glu, w_out, ln2, w_up, w_down,
                 min(TQ_POST, rows_per_seg))


def kernel(x_prompt, x_sample, rel_table, ln1, w_in, q_gain, k_gain, sink, lam_re, lam_im, log_dt,
           b_re, b_im, c_re, c_im, d_skip, w_glu, b_glu, w_out, ln2, w_up, w_down):
    assert ln1.shape[0] == 1, "single layer"
    bias = _band_bias(rel_table)
    outs = []
    s5w_by_nq = {}
    for x in (x_prompt, x_sample):
        seq = x.shape[1]
        nq = seq // (CHUNK * SUBLANES)
        assert nq * CHUNK * SUBLANES == seq and seq % BLOCK == 0
        if nq not in s5w_by_nq:
            s5w_by_nq[nq] = _s5_weights(lam_re[0], lam_im[0], log_dt[0], b_re[0], b_im[0],
                                        c_re[0], c_im[0], nq)
        s5w = s5w_by_nq[nq]
        outs.append(_layer(
            x, bias, sink[0].astype(F32) * LOG2E,
            ln1[0].astype(F32)[None], w_in[0].astype(BF16),
            jnp.tile(q_gain[0].astype(F32), N_HEADS)[None],
            jnp.tile(k_gain[0].astype(F32), N_KV_HEADS)[None],
            s5w, d_skip[0].astype(F32).reshape(SLABS, 1, LANES),
            w_glu[0].astype(BF16), b_glu[0].astype(F32)[None], w_out[0].astype(BF16),
            ln2[0].astype(F32)[None], w_up[0].astype(BF16), w_down[0].astype(BF16)))
    return tuple(outs)
```

```python
import math

import numpy as np
import jax
import jax.numpy as jnp
from jax import lax
from jax.experimental import pallas as pl
from jax.experimental.pallas import tpu as pltpu

D_MODEL = 1024
N_HEADS = 16
N_KV_HEADS = 4
HEAD_DIM = 64
Q_PER_KV = N_HEADS // N_KV_HEADS
WINDOW = 128
BLOCK = 128
N_BUCKETS = 32
MAX_DISTANCE = 128
ATTN_Q = N_HEADS * HEAD_DIM
ATTN_KV = N_KV_HEADS * HEAD_DIM
NEG_INF = -1e30
SSM_WIDTH = D_MODEL
SSM_GROUP = 16
SSM_GROUPS = SSM_WIDTH // SSM_GROUP
SSM_STATE = 64
D_FF = 4 * D_MODEL
EPS = 1e-6
IN_COLS = ATTN_Q + 2 * ATTN_KV + SSM_WIDTH + 2 * D_MODEL

LANES = 128
SUBLANES = 8
SEG_LEVELS = 3
CHUNK = 16
SLABS = SSM_WIDTH // LANES
GROUPS_PER_SLAB = LANES // SSM_GROUP
SLAB_STATE = GROUPS_PER_SLAB * SSM_STATE
VMEM_LIMIT = 56 * 1024 * 1024
TQ_IN = 64
TQ_POST = 64
QBLOCKS = 16

LOG2E = math.log2(math.e)
Q_SCALE = HEAD_DIM ** -0.5 * LOG2E

BF16 = jnp.bfloat16
F32 = jnp.float32


def _const_spec(shape):
    nd = len(shape)
    return pl.BlockSpec(shape, lambda *_: (0,) * nd, pipeline_mode=pl.Buffered(1))


def _pack2(lo, hi):
    lo_bits = lax.bitcast_convert_type(lo.astype(BF16).astype(F32), jnp.uint32)
    hi_bits = lax.bitcast_convert_type(hi.astype(BF16).astype(F32), jnp.uint32)
    return (lo_bits >> 16) | hi_bits


def _unpack2(words):
    lo = lax.bitcast_convert_type(words << 16, F32)
    hi = lax.bitcast_convert_type(words & jnp.uint32(0xFFFF0000), F32)
    return lo, hi


def _in_proj_kernel(x_ref, ln1_ref, w_ref, qg_ref, kg_ref, q_ref, kv_ref, u_ref, g_ref):
    nseg, tq, _ = x_ref.shape
    tm = nseg * tq
    x = x_ref[...].reshape(tm, D_MODEL)
    ms = jnp.mean(x * x, axis=-1, keepdims=True)
    rinv = lax.rsqrt(ms + EPS)
    h = (x * ln1_ref[...]).astype(BF16)
    proj = lambda c0, width: jnp.dot(h, w_ref[:, c0:c0 + width], preferred_element_type=F32) * rinv

    low = lax.broadcasted_iota(jnp.int32, (1, LANES), 1) < HEAD_DIM

    def head_norm(t, gain):
        cols = []
        for j in range(t.shape[1] // LANES):
            blk = t[:, j * LANES:(j + 1) * LANES]
            sq = blk * blk
            m_lo = jnp.sum(jnp.where(low, sq, 0.0), axis=-1, keepdims=True) * (1.0 / HEAD_DIM)
            m_hi = jnp.sum(jnp.where(low, 0.0, sq), axis=-1, keepdims=True) * (1.0 / HEAD_DIM)
            cols.append(blk * jnp.where(low, lax.rsqrt(m_lo + EPS), lax.rsqrt(m_hi + EPS)))
        return jnp.concatenate(cols, axis=1) * gain

    c0 = 0
    q = (head_norm(proj(c0, ATTN_Q), qg_ref[...]) * Q_SCALE).astype(BF16)
    q_ref[...] = q.reshape(nseg, tq, ATTN_Q)
    c0 += ATTN_Q
    k = head_norm(proj(c0, ATTN_KV), kg_ref[...])
    c0 += ATTN_KV
    v = proj(c0, ATTN_KV)
    c0 += ATTN_KV
    kvx = jnp.concatenate([k, pltpu.roll(k, HEAD_DIM, axis=1), v, pltpu.roll(v, HEAD_DIM, axis=1)],
                          axis=1).astype(BF16)
    kv_ref[...] = kvx.reshape(nseg, tq, 4 * ATTN_KV)
    u = proj(c0, SSM_WIDTH)
    for p in range(SLABS // 2):
        words = _pack2(u[:, (2 * p) * LANES:(2 * p + 1) * LANES], u[:, (2 * p + 1) * LANES:(2 * p + 2) * LANES])
        for r in range(nseg):
            u_ref[p, pl.ds(r, tq, stride=nseg), :] = words[r * tq:(r + 1) * tq]
    c0 += SSM_WIDTH
    g = proj(c0, 2 * D_MODEL).astype(BF16)
    g_ref[...] = g.reshape(nseg, tq, 2 * D_MODEL)


def _seg_view(a, b, seq):
    return a.reshape(b, SUBLANES, seq // SUBLANES, a.shape[-1])


def _in_proj(x2, ln1, w_in, qg, kg, b, seq, tq):
    n = x2.shape[0]
    per_seq = seq // SUBLANES // tq
    seg = lambda c: pl.BlockSpec((None, SUBLANES, tq, c), lambda i: (i // per_seq, 0, i % per_seq, 0))
    interleaved = pl.BlockSpec((None, SLABS // 2, tq * SUBLANES, LANES),
                               lambda i: (i // per_seq, 0, i % per_seq, 0))
    seg_shape = lambda c, dt: jax.ShapeDtypeStruct((b, SUBLANES, seq // SUBLANES, c), dt)
    q, kvx, u, g = pl.pallas_call(
        _in_proj_kernel,
        grid=(b * per_seq,),
        in_specs=[seg(D_MODEL), _const_spec((1, D_MODEL)), _const_spec((D_MODEL, IN_COLS)),
                  _const_spec((1, ATTN_Q)), _const_spec((1, ATTN_KV))],
        out_specs=[seg(ATTN_Q), seg(4 * ATTN_KV), interleaved, seg(2 * D_MODEL)],
        out_shape=[seg_shape(ATTN_Q, BF16), seg_shape(4 * ATTN_KV, BF16),
                   jax.ShapeDtypeStruct((b, SLABS // 2, seq, LANES), jnp.uint32),
                   seg_shape(2 * D_MODEL, BF16)],
        compiler_params=pltpu.CompilerParams(dimension_semantics=("parallel",),
                                             vmem_limit_bytes=VMEM_LIMIT,
                                             allow_input_fusion=[False, False, True, False, False]),
        name="in_proj",
    )(_seg_view(x2, b, seq), ln1, w_in, qg, kg)
    return q.reshape(n, ATTN_Q), kvx.reshape(n, 4 * ATTN_KV), u, g.reshape(n, 2 * D_MODEL)


_ROLLED_TILE = (0, 1, 1, 0)


def _attn_kernel(sink_ref, q_ref, kvp_ref, kvc_ref, kvn_ref, bias_ref, ones_ref, o_ref):
    n = pl.program_id(1)
    last = pl.num_programs(1) - 1
    kvx = jnp.concatenate([kvp_ref[0], kvc_ref[0], kvn_ref[0]], axis=0)
    low = lax.broadcasted_iota(jnp.int32, (1, LANES), 1) < HEAD_DIM
    zero = jnp.zeros((), BF16)

    def halves(base, kv):
        plain = kvx[:, base + (kv // 2) * LANES:base + (kv // 2 + 1) * LANES]
        r0 = base + ATTN_KV + _ROLLED_TILE[kv] * LANES
        rolled = kvx[:, r0:r0 + LANES]
        lo_src, hi_src = (plain, rolled) if kv % 2 == 0 else (rolled, plain)
        return jnp.where(low, lo_src, zero), jnp.where(low, zero, hi_src)

    top = lax.broadcasted_iota(jnp.int32, (2 * BLOCK, 1), 0) < BLOCK
    outs = [[] for _ in range(QBLOCKS)]
    for kv in range(N_KV_HEADS):
        ke, ko = halves(0, kv)
        ve, vo = halves(2 * ATTN_KV, kv)
        sink_e = jnp.where(top, sink_ref[4 * kv], sink_ref[4 * kv + 2])
        sink_o = jnp.where(top, sink_ref[4 * kv + 1], sink_ref[4 * kv + 3])
        for qb in range(QBLOCKS):
            rows = slice(qb * BLOCK, (qb + 3) * BLOCK)
            keys = jnp.concatenate([ke[rows], ko[rows]], axis=0)
            vals = jnp.concatenate([jnp.concatenate([ve[rows], vo[rows]], axis=0), ones_ref[...]],
                                   axis=1)
            if qb == 0:
                variant = jnp.where(n == 0, 0, 1)
            elif qb == QBLOCKS - 1:
                variant = jnp.where(n == last, 2, 1)
            else:
                variant = 1
            qrows = slice(qb * BLOCK, (qb + 1) * BLOCK)
            q2 = jnp.concatenate([q_ref[0, qrows, (2 * kv) * LANES:(2 * kv + 1) * LANES],
                                  q_ref[0, qrows, (2 * kv + 1) * LANES:(2 * kv + 2) * LANES]], axis=0)
            s = lax.dot_general(q2, keys, (((1,), (1,)), ((), ())), preferred_element_type=F32)
            s = s + bias_ref[variant, kv]
            se, so = s[:, :3 * BLOCK], s[:, 3 * BLOCK:]
            me = jnp.maximum(jnp.max(se, axis=-1, keepdims=True), sink_e)
            mo = jnp.maximum(jnp.max(so, axis=-1, keepdims=True), sink_o)
            p = jnp.concatenate([jnp.exp2(se - me), jnp.exp2(so - mo)], axis=1).astype(BF16)
            od = jnp.dot(p, vals, preferred_element_type=F32)
            den = od[:, LANES:] + jnp.where(low, jnp.exp2(sink_e - me), jnp.exp2(sink_o - mo))
            o = od[:, :LANES] / den
            outs[qb] += [o[:BLOCK], o[BLOCK:]]
    for qb in range(QBLOCKS):
        o_ref[0, qb * BLOCK:(qb + 1) * BLOCK, :] = jnp.concatenate(outs[qb], axis=-1).astype(BF16)


def _attn(q, kvx, bias, ones, sink):
    b, s, _ = q.shape
    nb = s // BLOCK
    assert nb % QBLOCKS == 0 and QBLOCKS >= 2
    steps = nb // QBLOCKS
    edge_spec = lambda f: pl.BlockSpec((1, BLOCK, 4 * ATTN_KV), f)
    wide = lambda c: pl.BlockSpec((1, QBLOCKS * BLOCK, c), lambda bi, n, *_: (bi, n, 0))
    prev = lambda bi, n, *_: (bi, jnp.maximum(n * QBLOCKS - 1, 0), 0)
    nxt = lambda bi, n, *_: (bi, jnp.minimum((n + 1) * QBLOCKS, nb - 1), 0)
    grid_spec = pltpu.PrefetchScalarGridSpec(
        num_scalar_prefetch=1,
        grid=(b, steps),
        in_specs=[wide(ATTN_Q), edge_spec(prev), wide(4 * ATTN_KV), edge_spec(nxt),
                  _const_spec(bias.shape), _const_spec(ones.shape)],
        out_specs=wide(ATTN_Q),
    )
    return pl.pallas_call(
        _attn_kernel,
        grid_spec=grid_spec,
        out_shape=jax.ShapeDtypeStruct((b, s, ATTN_Q), BF16),
        compiler_params=pltpu.CompilerParams(dimension_semantics=("parallel", "arbitrary"),
                                             vmem_limit_bytes=VMEM_LIMIT),
        name="attn",
    )(sink, q, kvx, kvx, kvx, bias, ones)


def _s5_slab(nq, load_u, mt_ref, enc_ref, dect_ref, coef_ref, dsk_ref, store_y):
    tile_rows = lambda x, q: x[q * SUBLANES:(q + 1) * SUBLANES]
    tile = lambda q, s: pl.ds((q * CHUNK + s) * SUBLANES, SUBLANES)

    x_step = lambda s: jnp.concatenate([load_u(tile(q, s)) for q in range(nq)], axis=0)
    xst = [x_step(s).astype(BF16).T for s in range(CHUNK)]

    yield
    yts, pair_states = [], []
    for g in range(GROUPS_PER_SLAB):
        ugt = jnp.concatenate([xt[g * SSM_GROUP:(g + 1) * SSM_GROUP] for xt in xst], axis=0)
        ug = ugt.T
        yts.append(jnp.dot(mt_ref[g], ugt, preferred_element_type=F32))
        part = jnp.dot(ug, enc_ref[g], preferred_element_type=F32)
        if g % 2 == 0:
            pair_states.append(part)
        else:
            pair_states[-1] = pair_states[-1] + part

    def state(d, ri):
        c0 = (2 * d + ri) * LANES
        return jnp.concatenate([p[:, c0:c0 + LANES] for p in pair_states], axis=1)

    def coef(d, row):
        return coef_ref[d, 0, row:row + 1, :], coef_ref[d, 1, row:row + 1, :]

    seg = lax.broadcasted_iota(jnp.int32, (SUBLANES, 1), 0)

    def seg_shift(x, sh, forward):
        if forward:
            return jnp.where(seg >= sh, pltpu.roll(x, sh, axis=0), 0.0)
        return jnp.where(seg < SUBLANES - sh, pltpu.roll(x, SUBLANES - sh, axis=0), 0.0)

    yield
    carried = []
    for d in range(2):
        forward = d == 0
        s_re, s_im = state(d, 0), state(d, 1)
        a_re, a_im = coef(d, 0)
        order = list(range(nq)) if forward else list(range(nq - 1, -1, -1))
        t_re, t_im = {}, {}
        for n, q in enumerate(order):
            cur_re, cur_im = tile_rows(s_re, q), tile_rows(s_im, q)
            if n:
                p_re, p_im = t_re[order[n - 1]], t_im[order[n - 1]]
                cur_re = cur_re + (a_re * p_re - a_im * p_im)
                cur_im = cur_im + (a_re * p_im + a_im * p_re)
            t_re[q], t_im[q] = cur_re, cur_im
        e_re, e_im = t_re[order[-1]], t_im[order[-1]]
        for lv in range(SEG_LEVELS):
            b_re, b_im = coef(d, 1 + lv)
            h_re, h_im = seg_shift(e_re, 1 << lv, forward), seg_shift(e_im, 1 << lv, forward)
            e_re, e_im = e_re + (b_re * h_re - b_im * h_im), e_im + (b_re * h_im + b_im * h_re)
        c_re, c_im = seg_shift(e_re, 1, forward), seg_shift(e_im, 1, forward)
        x_re, x_im = [], []
        for q in range(nq):
            n = order.index(q)
            w_re, w_im = coef(d, 1 + SEG_LEVELS + n)
            r_re, r_im = w_re * c_re - w_im * c_im, w_re * c_im + w_im * c_re
            if n:
                r_re, r_im = r_re + t_re[order[n - 1]], r_im + t_im[order[n - 1]]
            x_re.append(r_re)
            x_im.append(r_im)
        carried += [jnp.concatenate(x_re, axis=0), jnp.concatenate(x_im, axis=0)]

    yield
    for g in range(GROUPS_PER_SLAB):
        c0 = (g // 2) * LANES
        xpair = jnp.concatenate([c[:, c0:c0 + LANES] for c in carried], axis=1).astype(BF16)
        yts[g] = yts[g] + lax.dot_general(dect_ref[g], xpair, (((1,), (1,)), ((), ())),
                                          preferred_element_type=F32)
    yield
    dsk = dsk_ref[...]
    for t in range(CHUNK):
        yt = jnp.concatenate([y[t * SSM_GROUP:(t + 1) * SSM_GROUP] for y in yts], axis=0)
        y = yt.T + dsk * x_step(t)
        for q in range(nq):
            store_y(tile(q, t), tile_rows(y, q))


def _s5_kernel(u_ref, mt_ref, enc_ref, dect_ref, coef_ref, dsk_ref, y_ref, y_lo_ref):
    nq = u_ref.shape[0] // (CHUNK * SUBLANES)

    def store_lo(rows, y):
        y_lo_ref[rows, :] = y

    def store_both(rows, y):
        y_ref[rows, :] = _pack2(y_lo_ref[rows, :], y)

    slabs = [_s5_slab(nq, lambda rows, i=i: _unpack2(u_ref[rows, :])[i], mt_ref.at[i], enc_ref.at[i],
                      dect_ref.at[i], coef_ref.at[i], dsk_ref.at[i], store)
             for i, store in enumerate((store_lo, store_both))]
    next(slabs[0], None)
    busy = True
    while busy:
        busy = False
        for s in slabs:
            try:
                next(s)
                busy = True
            except StopIteration:
                pass


def _s5(u4, mt, enc, dect, coef, dsk):
    b, planes, seq, _ = u4.shape
    per_plane = lambda a: pl.BlockSpec((2,) + a.shape[1:], lambda p, bi: (p,) + (0,) * (a.ndim - 1),
                                       pipeline_mode=pl.Buffered(1))
    io = pl.BlockSpec((None, None, seq, LANES), lambda p, bi: (bi, p, 0, 0))
    return pl.pallas_call(
        _s5_kernel,
        grid=(planes, b),
        in_specs=[io, per_plane(mt), per_plane(enc), per_plane(dect), per_plane(coef), per_plane(dsk)],
        out_specs=io,
        out_shape=jax.ShapeDtypeStruct(u4.shape, jnp.uint32),
        scratch_shapes=[pltpu.VMEM((seq, LANES), F32)],
        compiler_params=pltpu.CompilerParams(dimension_semantics=("parallel", "arbitrary"),
                                             vmem_limit_bytes=VMEM_LIMIT),
        name="s5",
    )(u4, mt, enc, dect, coef, dsk)


def _cmul(ar, ai, br, bi):
    return ar * br - ai * bi, ar * bi + ai * br


def _cpowers(a_re, a_im, n):
    p_re, p_im = jnp.ones_like(a_re)[None], jnp.zeros_like(a_im)[None]
    s_re, s_im = a_re, a_im
    while p_re.shape[0] < n:
        h_re, h_im = _cmul(p_re, p_im, s_re[None], s_im[None])
        p_re, p_im = jnp.concatenate([p_re, h_re], 0), jnp.concatenate([p_im, h_im], 0)
        s_re, s_im = _cmul(s_re, s_im, s_re, s_im)
    return p_re[:n], p_im[:n]


def _s5_gen_kernel(bm_ref, cm_ref, apow_ref, enc_ref, dect_ref):
    row_group = lax.broadcasted_iota(jnp.int32, (LANES, SLAB_STATE), 0) // SSM_GROUP
    col_group = lax.broadcasted_iota(jnp.int32, (LANES, SLAB_STATE), 1) // SSM_STATE

    def block_diag(ref, d, ri):
        wide = jnp.concatenate([ref[0, d, ri]] * (SLAB_STATE // LANES), axis=1)
        return jnp.where(row_group == col_group, wide, 0.0)

    bm = [[block_diag(bm_ref, d, ri) for ri in range(2)] for d in range(2)]
    cm = [[block_diag(cm_ref, d, ri) for ri in range(2)] for d in range(2)]

    def put(ref, d, ri, step, tile):
        tile = tile.astype(BF16)
        c0 = (2 * d + ri) * LANES
        for g in range(GROUPS_PER_SLAB):
            p0 = (g // 2) * LANES
            ref[0, g, step * SSM_GROUP:(step + 1) * SSM_GROUP, c0:c0 + LANES] = (
                tile[g * SSM_GROUP:(g + 1) * SSM_GROUP, p0:p0 + LANES])

    for d in range(2):
        for k in range(CHUNK + 1):
            a_re, a_im = apow_ref[0, d, 0, k:k + 1, :], apow_ref[0, d, 1, k:k + 1, :]
            if k < CHUNK:
                s = CHUNK - 1 - k if d == 0 else k
                e_re, e_im = _cmul(bm[d][0], bm[d][1], a_re, a_im)
                put(enc_ref, d, 0, s, e_re)
                put(enc_ref, d, 1, s, e_im)
            if k >= 1:
                t = k - 1 if d == 0 else CHUNK - k
                f_re, f_im = _cmul(cm[d][0], cm[d][1], a_re, a_im)
                put(dect_ref, d, 0, t, f_re)
                put(dect_ref, d, 1, t, -f_im)


def _s5_gen(bm, cm, apow):
    small = lambda a: pl.BlockSpec((1,) + a.shape[1:], lambda sl: (sl,) + (0,) * (a.ndim - 1))
    shape = (SLABS, GROUPS_PER_SLAB, CHUNK * SSM_GROUP, 4 * LANES)
    out = pl.BlockSpec((1,) + shape[1:], lambda sl: (sl, 0, 0, 0))
    return pl.pallas_call(
        _s5_gen_kernel,
        grid=(SLABS,),
        in_specs=[small(bm), small(cm), small(apow)],
        out_specs=[out, out],
        out_shape=[jax.ShapeDtypeStruct(shape, BF16), jax.ShapeDtypeStruct(shape, BF16)],
        compiler_params=pltpu.CompilerParams(dimension_semantics=("parallel",),
                                             vmem_limit_bytes=VMEM_LIMIT),
        name="s5_gen",
    )(bm, cm, apow)


def _slab_tiles(t_re, t_im):
    t = jnp.stack([t_re, t_im], 1).astype(F32)
    t = t.reshape(2, 2, SLABS, LANES, SSM_STATE)
    t = jnp.concatenate([t, t], axis=-1)
    return jnp.transpose(t, (2, 0, 1, 3, 4))


def _s5_weights(lam_re, lam_im, log_dt, b_re, b_im, c_re, c_im, nq):
    G, C = SSM_GROUPS, SSM_GROUP
    dt = jnp.exp(log_dt.astype(F32))[..., None]
    lr, li = lam_re.astype(F32), lam_im.astype(F32)
    mag = jnp.exp(lr * dt)
    ang = li * dt
    a_re, a_im = mag * jnp.cos(ang), mag * jnp.sin(ang)
    den = lr * lr + li * li
    nr, ni = a_re - 1.0, a_im
    z_re = (nr * lr + ni * li) / den
    z_im = (ni * lr - nr * li) / den
    br, bi = b_re.astype(F32), b_im.astype(F32)
    bb_re = z_re[..., None] * br - z_im[..., None] * bi
    bb_im = z_re[..., None] * bi + z_im[..., None] * br
    cr = jnp.swapaxes(c_re.astype(F32), -1, -2)
    ci = jnp.swapaxes(c_im.astype(F32), -1, -2)

    pw_re, pw_im = _cpowers(a_re, a_im, CHUNK + 1)

    ab_re, ab_im = _cmul(pw_re[:CHUNK, ..., None], pw_im[:CHUNK, ..., None], bb_re[None], bb_im[None])
    kern = jnp.einsum('dgpc,kdgpe->kdgce', jnp.concatenate([cr, -ci], 2),
                      jnp.concatenate([ab_re, ab_im], 3), precision='highest')
    lag_tab = jnp.concatenate([kern[::-1, 0][:-1], (kern[0, 0] + kern[0, 1])[None], kern[1:, 1]], 0)
    lag_tab = jnp.transpose(lag_tab, (1, 2, 0, 3)).reshape(G, C, (2 * CHUNK - 1) * C)
    mt = jnp.stack([lag_tab[:, :, (CHUNK - 1 - t) * C:(CHUNK - 1 - t) * C + CHUNK * C]
                    for t in range(CHUNK)], 1).reshape(SLABS, GROUPS_PER_SLAB, CHUNK * C, CHUNK * C)

    apow = jnp.stack([pw_re, pw_im], 0).reshape(2, CHUNK + 1, 2, SLABS, SLAB_STATE)
    apow = jnp.transpose(apow, (3, 2, 0, 1, 4))
    enc, dect = _s5_gen(_slab_tiles(jnp.swapaxes(bb_re, -1, -2), jnp.swapaxes(bb_im, -1, -2)),
                        _slab_tiles(c_re, c_im), apow)

    s_re, s_im = pw_re[CHUNK], pw_im[CHUNK]
    w_re, w_im = _cpowers(s_re, s_im, nq)
    q_re, q_im = _cmul(w_re[nq - 1], w_im[nq - 1], s_re, s_im)
    rows_re, rows_im = [s_re], [s_im]
    for _ in range(SEG_LEVELS):
        rows_re.append(q_re)
        rows_im.append(q_im)
        q_re, q_im = _cmul(q_re, q_im, q_re, q_im)
    c_all = jnp.stack([jnp.concatenate([jnp.stack(rows_re, 0), w_re], 0),
                       jnp.concatenate([jnp.stack(rows_im, 0), w_im], 0)], 0)
    n_rows = c_all.shape[1]
    c_all = c_all.reshape(2, n_rows, 2, SLABS, SLAB_STATE)
    coef = jnp.transpose(c_all, (3, 2, 0, 1, 4))
    return mt.astype(BF16), enc, dect, coef


def _post_kernel(x_ref, ya_ref, ys_ref, g_ref, wglu_ref, bglu_ref, wout_ref, ln2_ref,
                 wup_ref, wdown_ref, o_ref):
    nseg, tq, _ = x_ref.shape
    tm = nseg * tq
    ys = jnp.concatenate(
        [jnp.concatenate([half for p in range(SLABS // 2)
                          for half in _unpack2(ys_ref[p, pl.ds(r, tq, stride=nseg), :])], axis=1)
         for r in range(nseg)], axis=0)
    y = jax.nn.gelu(ys)
    z = jnp.dot(y.astype(BF16), wglu_ref[...], preferred_element_type=F32) + bglu_ref[...]
    y = y * jax.nn.sigmoid(z)
    g = g_ref[...].reshape(tm, 2 * D_MODEL).astype(F32)
    ya = ya_ref[...].reshape(tm, D_MODEL).astype(F32)
    mixed = jax.nn.sigmoid(g[:, :D_MODEL]) * ya + jax.nn.sigmoid(g[:, D_MODEL:]) * y
    x = x_ref[...].reshape(tm, D_MODEL)
    x = x + jnp.dot(mixed.astype(BF16), wout_ref[...], preferred_element_type=F32)
    ms = jnp.mean(x * x, axis=-1, keepdims=True)
    h = (x * ln2_ref[...]).astype(BF16)
    up = jnp.dot(h, wup_ref[...], preferred_element_type=F32) * lax.rsqrt(ms + EPS)
    act = jnp.square(jnp.maximum(up, 0.0)).astype(BF16)
    out = x + jnp.dot(act, wdown_ref[...], preferred_element_type=F32)
    o_ref[...] = out.reshape(nseg, tq, D_MODEL)


def _post(x2, ya, ys4, g, w_glu, b_glu, w_out, ln2, w_up, w_down, tq):
    b, _, seq, _ = ys4.shape
    per_seq = seq // SUBLANES // tq
    seg = lambda c: pl.BlockSpec((None, SUBLANES, tq, c), lambda i: (i // per_seq, 0, i % per_seq, 0))
    interleaved = pl.BlockSpec((None, SLABS // 2, tq * SUBLANES, LANES),
                               lambda i: (i // per_seq, 0, i % per_seq, 0))
    out = pl.pallas_call(
        _post_kernel,
        grid=(b * per_seq,),
        in_specs=[seg(D_MODEL), seg(D_MODEL), interleaved, seg(2 * D_MODEL),
                  _const_spec((D_MODEL, D_MODEL)), _const_spec((1, D_MODEL)),
                  _const_spec((D_MODEL, D_MODEL)), _const_spec((1, D_MODEL)),
                  _const_spec((D_MODEL, D_FF)), _const_spec((D_FF, D_MODEL))],
        out_specs=seg(D_MODEL),
        out_shape=jax.ShapeDtypeStruct((b, SUBLANES, seq // SUBLANES, D_MODEL), F32),
        compiler_params=pltpu.CompilerParams(dimension_semantics=("parallel",),
                                             vmem_limit_bytes=VMEM_LIMIT,
                                             allow_input_fusion=[False] * 4 + [True, False, True, False,
                                                                               True, True]),
        name="post",
    )(_seg_view(x2, b, seq), _seg_view(ya, b, seq), ys4, _seg_view(g, b, seq),
      w_glu, b_glu, w_out, ln2, w_up, w_down)
    return out.reshape(b, seq, D_MODEL)


def _t5_bucket(rel):
    nb = N_BUCKETS // 2
    ret = (rel > 0).astype(np.int32) * nb
    n = np.abs(rel)
    max_exact = nb // 2
    n_safe = np.maximum(n, 1).astype(np.float32)
    large = max_exact + (np.log(n_safe / max_exact) / math.log(MAX_DISTANCE / max_exact)
                         * (nb - max_exact)).astype(np.int32)
    large = np.minimum(large, nb - 1)
    return (ret + np.where(n < max_exact, n, large)).astype(np.int32)


def _band_bias(rel_table):
    qi = np.arange(BLOCK)[:, None]
    kj = np.arange(3 * BLOCK)[None, :]
    rel = kj - BLOCK - qi
    onehot = (np.arange(N_BUCKETS)[:, None] == _t5_bucket(rel).reshape(1, -1)).astype(np.float32)
    bias = jnp.dot(rel_table.astype(F32).T * LOG2E, jnp.asarray(onehot), precision='highest')
    bias = bias.reshape(N_HEADS, BLOCK, 3 * BLOCK)
    in_band = np.abs(rel) <= WINDOW
    keep = np.stack([in_band & (kj >= BLOCK), in_band, in_band & (kj < 2 * BLOCK)], 0)
    bias = jnp.where(jnp.asarray(keep)[:, None], bias[None], NEG_INF)
    bias = bias.reshape(3, N_KV_HEADS, 2, 2, BLOCK, 3 * BLOCK)
    return jnp.transpose(bias, (0, 1, 2, 4, 3, 5)).reshape(3, N_KV_HEADS, 2 * BLOCK, 6 * BLOCK)


def _sum_mat():
    first = np.arange(6 * BLOCK)[:, None] < 3 * BLOCK
    low = np.arange(LANES)[None, :] < HEAD_DIM
    return jnp.asarray(first == low, BF16)


def _layer(x, bias, sink, ln1, w_in, qg, kg, s5w, dsk, w_glu, b_glu, w_out, ln2, w_up, w_down):
    b, s, _ = x.shape
    x2 = x.reshape(b * s, D_MODEL)
    rows_per_seg = s // SUBLANES
    q, kvx, u4, g = _in_proj(x2, ln1, w_in, qg, kg, b, s, min(TQ_IN, rows_per_seg))
    ya = _attn(q.reshape(b, s, ATTN_Q), kvx.reshape(b, s, 4 * ATTN_KV), bias, _sum_mat(), sink)
    ys4 = _s5(u4, *s5w, dsk)
    return _post(x2, ya.reshape(b * s, ATTN_Q), ys4, g, w_glu, b_glu, w_out, ln2, w_up, w_down,
                 min(TQ_POST, rows_per_seg))


def kernel(x_prompt, x_sample, rel_table, ln1, w_in, q_gain, k_gain, sink, lam_re, lam_im, log_dt,
           b_re, b_im, c_re, c_im, d_skip, w_glu, b_glu, w_out, ln2, w_up, w_down):
    assert ln1.shape[0] == 1, "single layer"
    bias = _band_bias(rel_table)
    outs = []
    s5w_by_nq = {}
    for x in (x_prompt, x_sample):
        seq = x.shape[1]
        nq = seq // (CHUNK * SUBLANES)
        assert nq * CHUNK * SUBLANES == seq and seq % BLOCK == 0
        if nq not in s5w_by_nq:
            s5w_by_nq[nq] = _s5_weights(lam_re[0], lam_im[0], log_dt[0], b_re[0], b_im[0],
                                        c_re[0], c_im[0], nq)
        s5w = s5w_by_nq[nq]
        outs.append(_layer(
            x, bias, sink[0].astype(F32) * LOG2E,
            ln1[0].astype(F32)[None], w_in[0].astype(BF16),
            jnp.tile(q_gain[0].astype(F32), N_HEADS)[None],
            jnp.tile(k_gain[0].astype(F32), N_KV_HEADS)[None],
            s5w, d_skip[0].astype(F32).reshape(SLABS, 1, LANES),
            w_glu[0].astype(BF16), b_glu[0].astype(F32)[None], w_out[0].astype(BF16),
            ln2[0].astype(F32)[None], w_up[0].astype(BF16), w_down[0].astype(BF16)))
    return tuple(outs)
```
